```python
import jax
import jax.numpy as jnp
from jax import lax
import numpy as np

D_MODEL = 1024
BATCH = 8
SEQ = 4096
DEPTH = 4

N_A = DEPTH // 2
N_B = DEPTH - N_A
ALPHA = (2.0 * DEPTH) ** 0.25
BETA = (8.0 * DEPTH) ** -0.25
LN_EPS = 1e-5

SSM_EXPAND = 2
D_INNER = SSM_EXPAND * D_MODEL
SSM_HEAD_DIM = 64
SSM_HEADS = D_INNER // SSM_HEAD_DIM
SSM_GROUPS = 4
SSM_HPG = SSM_HEADS // SSM_GROUPS
SSM_STATE = 128
CONV_WIDTH = 4
SSM_CHUNK = 128
SSM_BC_DIM = SSM_GROUPS * SSM_STATE
CONV_DIM = D_INNER + 2 * SSM_BC_DIM
SSM_IN_DIM = D_INNER + CONV_DIM + SSM_HEADS
DT_MIN = 1e-3
DT_MAX = 1e-1

NSA_HEAD_DIM = 64
NSA_Q_HEADS = D_MODEL // NSA_HEAD_DIM
NSA_KV_HEADS = 4
NSA_Q_PER_KV = NSA_Q_HEADS // NSA_KV_HEADS
NSA_N_BRANCH = 3
NSA_Q_DIM = NSA_Q_HEADS * NSA_HEAD_DIM
NSA_KV_DIM = NSA_KV_HEADS * NSA_HEAD_DIM
CMP_BLOCK = 32
CMP_STRIDE = 16
SLC_BLOCK = 64
N_SELECT = 16
WINDOW = 512
PHI_HIDDEN = 4 * NSA_HEAD_DIM
SLC_QUERY_BLOCK = 32
WIN_QUERY_BLOCK = 128
ROPE_THETA = 10000.0
ATTN_SCALE = NSA_HEAD_DIM ** -0.5
FORCED_SCORE = 1e9

N_EXPERTS = 32
TOP_K = 4
D_EXPERT = D_MODEL
SWIGLU_LIMIT = 7.0
SWIGLU_ALPHA = 1.702
MOE_ROW_BLOCK = 256

kernel_name = "yoco_mamba2_nsa_moe_deepnorm_adaln"


def _normal(key, shape, scale):
    return jax.random.normal(key, shape, jnp.float32) * scale


def setup_inputs(seed: int = 0) -> dict:
    key = jax.random.key(seed)
    it = iter(jax.random.split(key, 40))
    nk = lambda: next(it)
    D = D_MODEL
    x = _normal(nk(), (BATCH, SEQ, D), 1.0)
    c = _normal(nk(), (BATCH, D), 1.0)
    pos = (jax.random.randint(nk(), (BATCH, 1), 0, 1024, dtype=jnp.int32)
           + jnp.arange(SEQ, dtype=jnp.int32)[None, :])
    ada_w = _normal(nk(), (DEPTH, D, 6 * D), 0.2 * D ** -0.5)
    ada_b = _normal(nk(), (DEPTH, 6 * D), 0.01)
    ln_g = 1.0 + _normal(nk(), (DEPTH, 2, D), 0.01)
    ln_b = _normal(nk(), (DEPTH, 2, D), 0.01)
    ssm_in_w = _normal(nk(), (N_A, D, SSM_IN_DIM), D ** -0.5)
    ssm_conv_w = _normal(nk(), (N_A, CONV_WIDTH, CONV_DIM), CONV_WIDTH ** -0.5)
    ssm_conv_b = _normal(nk(), (N_A, CONV_DIM), 0.01)
    u = jax.random.uniform(nk(), (N_A, SSM_HEADS), jnp.float32)
    dt0 = jnp.exp(u * (np.log(DT_MAX) - np.log(DT_MIN)) + np.log(DT_MIN))
    ssm_dt_bias = dt0 + jnp.log(-jnp.expm1(-dt0))
    ssm_a_log = jnp.log(jax.random.uniform(nk(), (N_A, SSM_HEADS), jnp.float32, 1.0, 16.0))
    ssm_d = 1.0 + _normal(nk(), (N_A, SSM_HEADS), 0.1)
    ssm_norm_w = 1.0 + _normal(nk(), (N_A, D_INNER), 0.01)
    ssm_out_w = _normal(nk(), (N_A, D_INNER, D), BETA * D_INNER ** -0.5)
    kv_ada_w = _normal(nk(), (D, 2 * D), 0.2 * D ** -0.5)
    kv_ada_b = _normal(nk(), (2 * D,), 0.01)
    kv_w = _normal(nk(), (D, 2 * NSA_N_BRANCH * NSA_KV_DIM), D ** -0.5)
    cmp_pos = _normal(nk(), (CMP_BLOCK, NSA_HEAD_DIM), 0.1)
    phi_in = CMP_BLOCK * NSA_HEAD_DIM
    phi_k_w1 = _normal(nk(), (phi_in, PHI_HIDDEN), phi_in ** -0.5)
    phi_k_w2 = _normal(nk(), (PHI_HIDDEN, NSA_HEAD_DIM), PHI_HIDDEN ** -0.5)
    phi_v_w1 = _normal(nk(), (phi_in, PHI_HIDDEN), phi_in ** -0.5)
    phi_v_w2 = _normal(nk(), (PHI_HIDDEN, NSA_HEAD_DIM), PHI_HIDDEN ** -0.5)
    nsa_q_w = _normal(nk(), (N_B, D, NSA_Q_DIM + NSA_N_BRANCH * NSA_Q_HEADS), D ** -0.5)
    nsa_o_w = _normal(nk(), (N_B, NSA_Q_DIM, D), BETA * NSA_Q_DIM ** -0.5)
    router_w = _normal(nk(), (DEPTH, D, N_EXPERTS), D ** -0.5)
    router_b = _normal(nk(), (DEPTH, N_EXPERTS), 0.01)
    moe_w_up = _normal(nk(), (DEPTH, N_EXPERTS, D, 2 * D_EXPERT), D ** -0.5)
    moe_b_up = _normal(nk(), (DEPTH, N_EXPERTS, 2 * D_EXPERT), 0.01)
    moe_w_down = _normal(nk(), (DEPTH, N_EXPERTS, D_EXPERT, D), BETA * D_EXPERT ** -0.5)
    moe_b_down = _normal(nk(), (DEPTH, N_EXPERTS, D), 0.01)
    return {"x": x, "c": c, "pos": pos, "ada_w": ada_w, "ada_b": ada_b, "ln_g": ln_g, "ln_b": ln_b,
            "ssm_in_w": ssm_in_w, "ssm_conv_w": ssm_conv_w, "ssm_conv_b": ssm_conv_b,
            "ssm_dt_bias": ssm_dt_bias, "ssm_a_log": ssm_a_log, "ssm_d": ssm_d,
            "ssm_norm_w": ssm_norm_w, "ssm_out_w": ssm_out_w,
            "kv_ada_w": kv_ada_w, "kv_ada_b": kv_ada_b, "kv_w": kv_w, "cmp_pos": cmp_pos,
            "phi_k_w1": phi_k_w1, "phi_k_w2": phi_k_w2, "phi_v_w1": phi_v_w1, "phi_v_w2": phi_v_w2,
            "nsa_q_w": nsa_q_w, "nsa_o_w": nsa_o_w, "router_w": router_w, "router_b": router_b,
            "moe_w_up": moe_w_up, "moe_b_up": moe_b_up, "moe_w_down": moe_w_down,
            "moe_b_down": moe_b_down}


def layer_norm(x, g, b):
    xf = x.astype(jnp.float32)
    mu = jnp.mean(xf, axis=-1, keepdims=True)
    var = jnp.mean(jnp.square(xf - mu), axis=-1, keepdims=True)
    return ((xf - mu) * lax.rsqrt(var + LN_EPS) * g + b).astype(x.dtype)


def modulate(x, shift, scale):
    return x * (1.0 + scale[:, None, :]) + shift[:, None, :]


def post_norm(x, y, gate, g, b):
    return layer_norm(ALPHA * x + (1.0 + gate[:, None, :]) * y, g, b)


def rope(x, pos):
    half = x.shape[-1] // 2
    inv = ROPE_THETA ** (-jnp.arange(half, dtype=jnp.float32) / half)
    ang = pos.astype(jnp.float32)[..., None] * inv
    cos = jnp.cos(ang)[:, :, None, :]
    sin = jnp.sin(ang)[:, :, None, :]
    x1 = x[..., :half].astype(jnp.float32)
    x2 = x[..., half:].astype(jnp.float32)
    return jnp.concatenate([x1 * cos - x2 * sin, x2 * cos + x1 * sin], axis=-1).astype(x.dtype)


def masked_softmax(s, mask):
    s = jnp.where(mask, s.astype(jnp.float32), -jnp.inf)
    m = jnp.max(s, axis=-1, keepdims=True)
    m = jnp.where(jnp.isfinite(m), m, 0.0)
    p = jnp.exp(s - m)
    return p / jnp.maximum(jnp.sum(p, axis=-1, keepdims=True), jnp.finfo(jnp.float32).tiny)


def ssd_chunked(xs, dt, a_neg, bm, cm):
    bsz, seq = xs.shape[:2]
    nc, L = seq // SSM_CHUNK, SSM_CHUNK
    G, R, P, N = SSM_GROUPS, SSM_HPG, SSM_HEAD_DIM, SSM_STATE
    xdt = (xs.astype(jnp.float32) * dt[..., None]).reshape(bsz, nc, L, G, R, P)
    a_cs = jnp.cumsum((dt * a_neg).reshape(bsz, nc, L, G, R), axis=2)
    bc = bm.astype(jnp.float32).reshape(bsz, nc, L, G, N)
    cc = cm.astype(jnp.float32).reshape(bsz, nc, L, G, N)
    seg = a_cs[:, :, :, None] - a_cs[:, :, None, :]
    tri = jnp.tril(jnp.ones((L, L), dtype=bool))
    decay = jnp.exp(jnp.where(tri[:, :, None, None], seg, -jnp.inf))
    cb = jnp.einsum("bclgn,bcsgn->bclsg", cc, bc)
    y_diag = jnp.einsum("bclsg,bclsgr,bcsgrp->bclgrp", cb, decay, xdt)
    decay_end = jnp.exp(a_cs[:, :, -1:] - a_cs)
    states = jnp.einsum("bclgn,bclgr,bclgrp->bcgrpn", bc, decay_end, xdt)
    chunk_decay = jnp.exp(a_cs[:, :, -1])

    def step(carry, inp):
        st, dec = inp
        return carry * dec[..., None, None] + st, carry

    init = jnp.zeros((bsz, G, R, P, N), jnp.float32)
    _, prev = lax.scan(step, init, (jnp.swapaxes(states, 0, 1), jnp.swapaxes(chunk_decay, 0, 1)))
    prev = jnp.swapaxes(prev, 0, 1)
    y_off = jnp.einsum("bclgn,bcgrpn,bclgr->bclgrp", cc, prev, jnp.exp(a_cs))
    return (y_diag + y_off).reshape(bsz, seq, G, R, P)


def mamba2_mixer(h, in_w, conv_w, conv_b, dt_bias, a_log, d_skip, norm_w, out_w):
    bsz, seq, _ = h.shape
    G, R, P, N = SSM_GROUPS, SSM_HPG, SSM_HEAD_DIM, SSM_STATE
    zxbcdt = h @ in_w
    z = zxbcdt[..., :D_INNER]
    xbc = zxbcdt[..., D_INNER:D_INNER + CONV_DIM]
    dt_raw = zxbcdt[..., D_INNER + CONV_DIM:]
    xbc = lax.conv_general_dilated(xbc, conv_w[:, None, :], window_strides=(1,),
                                   padding=[(CONV_WIDTH - 1, 0)],
                                   dimension_numbers=("NWC", "WIO", "NWC"),
                                   feature_group_count=CONV_DIM)
    xbc = jax.nn.silu(xbc + conv_b)
    xs = xbc[..., :D_INNER].reshape(bsz, seq, G, R, P)
    bm = xbc[..., D_INNER:D_INNER + SSM_BC_DIM].reshape(bsz, seq, G, N)
    cm = xbc[..., D_INNER + SSM_BC_DIM:].reshape(bsz, seq, G, N)
    dt = jax.nn.softplus(dt_raw.astype(jnp.float32) + dt_bias).reshape(bsz, seq, G, R)
    a_neg = -jnp.exp(a_log.astype(jnp.float32)).reshape(G, R)
    y = ssd_chunked(xs, dt, a_neg, bm, cm) + xs.astype(jnp.float32) * d_skip.reshape(G, R)[:, :, None]
    y = y.reshape(bsz, seq, D_INNER) * jax.nn.silu(z.astype(jnp.float32))
    yg = y.reshape(bsz, seq, G, D_INNER // G)
    yg = yg * lax.rsqrt(jnp.mean(jnp.square(yg), axis=-1, keepdims=True) + LN_EPS)
    y = (yg.reshape(bsz, seq, D_INNER) * norm_w).astype(h.dtype)
    return y @ out_w


def compress_blocks(k, cmp_pos, w1, w2):
    bsz, seq = k.shape[:2]
    sub = k.reshape(bsz, seq // CMP_STRIDE, CMP_STRIDE, NSA_KV_HEADS, NSA_HEAD_DIM)
    blocks = jnp.concatenate([sub[:, :-1], sub[:, 1:]], axis=2) + cmp_pos[:, None, :]
    n_cmp = blocks.shape[1]
    flat = blocks.transpose(0, 1, 3, 2, 4).reshape(bsz, n_cmp, NSA_KV_HEADS, CMP_BLOCK * NSA_HEAD_DIM)
    return jax.nn.gelu(flat @ w1) @ w2


def nsa_shared_kv(h, c_act, pos, kv_ada_w, kv_ada_b, kv_w, cmp_pos,
                  phi_k_w1, phi_k_w2, phi_v_w1, phi_v_w2):
    bsz, seq, _ = h.shape
    shift, scale = jnp.split(c_act @ kv_ada_w + kv_ada_b, 2, axis=-1)
    kv = (modulate(h, shift, scale) @ kv_w).reshape(bsz, seq, 2 * NSA_N_BRANCH, NSA_KV_HEADS, NSA_HEAD_DIM)
    k_cmp = compress_blocks(rope(kv[:, :, 0], pos), cmp_pos, phi_k_w1, phi_k_w2)
    v_cmp = compress_blocks(kv[:, :, 1], cmp_pos, phi_v_w1, phi_v_w2)
    k_slc = rope(kv[:, :, 2], pos)
    v_slc = kv[:, :, 3]
    k_win = rope(kv[:, :, 4], pos)
    v_win = kv[:, :, 5]
    return (k_cmp, v_cmp, k_slc, v_slc, k_win, v_win)


def cmp_slc_overlap(n_cmp, n_slc):
    cs = jnp.arange(n_cmp)[:, None] * CMP_STRIDE
    js = jnp.arange(n_slc)[None, :] * SLC_BLOCK
    ov = jnp.minimum(cs + CMP_BLOCK, js + SLC_BLOCK) - jnp.maximum(cs, js)
    return jnp.maximum(ov, 0).astype(jnp.float32) / CMP_BLOCK


def compressed_branch(q, k_cmp, v_cmp):
    seq = q.shape[1]
    n_cmp = k_cmp.shape[1]
    s = jnp.einsum("bsgrd,bngd->bgrsn", q, k_cmp) * ATTN_SCALE
    t = jnp.arange(seq)
    blk_end = jnp.arange(n_cmp) * CMP_STRIDE + CMP_BLOCK - 1
    p = masked_softmax(s, blk_end[None, :] <= t[:, None])
    o = jnp.einsum("bgrsn,bngd->bsgrd", p.astype(v_cmp.dtype), v_cmp)
    importance = jnp.einsum("bgrsn,nj->bgsj", p, cmp_slc_overlap(n_cmp, seq // SLC_BLOCK))
    return o, importance


def select_blocks(importance):
    seq, n_slc = importance.shape[-2:]
    n_sel = min(N_SELECT, n_slc)
    t_blk = jnp.arange(seq)[:, None] // SLC_BLOCK
    j = jnp.arange(n_slc)[None, :]
    forced = (j == 0) | (j == t_blk) | (j == t_blk - 1)
    score = jnp.where(forced, FORCED_SCORE, importance)
    score = jnp.where(j <= t_blk, score, -jnp.inf)
    top_v, idx = lax.top_k(score, n_sel)
    return idx, jnp.isfinite(top_v)


def selected_branch(q, k, v, blk_idx, blk_valid):
    bsz, seq = q.shape[:2]
    G, R, Dh = NSA_KV_HEADS, NSA_Q_PER_KV, NSA_HEAD_DIM
    n_slc = seq // SLC_BLOCK
    n_sel = blk_idx.shape[-1]
    QB = SLC_QUERY_BLOCK
    nq = seq // QB
    kb = k.reshape(bsz, n_slc, SLC_BLOCK, G, Dh).transpose(0, 3, 1, 2, 4)
    vb = v.reshape(bsz, n_slc, SLC_BLOCK, G, Dh).transpose(0, 3, 1, 2, 4)
    q_blocks = jnp.swapaxes(q.reshape(bsz, nq, QB, G, R, Dh), 0, 1)
    idx_blocks = blk_idx.reshape(bsz, G, nq, QB, n_sel).transpose(2, 0, 1, 3, 4)
    val_blocks = blk_valid.reshape(bsz, G, nq, QB, n_sel).transpose(2, 0, 1, 3, 4)
    t_blocks = jnp.arange(seq, dtype=jnp.int32).reshape(nq, QB)
    b_ix = jnp.arange(bsz)[:, None, None, None]
    g_ix = jnp.arange(G)[None, :, None, None]
    offs = jnp.arange(SLC_BLOCK, dtype=jnp.int32)

    def one_block(args):
        qb, ib, valb, tb = args
        kg = kb[b_ix, g_ix, ib]
        vg = vb[b_ix, g_ix, ib]
        s = jnp.einsum("bqgrd,bgqnkd->bgrqnk", qb, kg) * ATTN_SCALE
        kpos = ib[..., None] * SLC_BLOCK + offs
        mask = valb[..., None] & (kpos <= tb[None, None, :, None, None])
        p = masked_softmax(s.reshape(bsz, G, R, QB, n_sel * SLC_BLOCK),
                           mask.reshape(bsz, G, 1, QB, n_sel * SLC_BLOCK))
        p = p.reshape(bsz, G, R, QB, n_sel, SLC_BLOCK).astype(v.dtype)
        return jnp.einsum("bgrqnk,bgqnkd->bqgrd", p, vg)

    out = lax.map(one_block, (q_blocks, idx_blocks, val_blocks, t_blocks))
    return jnp.swapaxes(out, 0, 1).reshape(bsz, seq, G, R, Dh)


def window_branch(q, k, v):
    bsz, seq = q.shape[:2]
    G, R, Dh = NSA_KV_HEADS, NSA_Q_PER_KV, NSA_HEAD_DIM
    QB = WIN_QUERY_BLOCK
    nb = seq // QB
    span = WINDOW + QB
    k_pad = jnp.pad(k, ((0, 0), (WINDOW, 0), (0, 0), (0, 0)))
    v_pad = jnp.pad(v, ((0, 0), (WINDOW, 0), (0, 0), (0, 0)))
    q_off = jnp.arange(QB, dtype=jnp.int32)
    k_off = jnp.arange(span, dtype=jnp.int32)

    def one_block(i):
        start = i * QB
        qb = lax.dynamic_slice_in_dim(q, start, QB, axis=1)
        kw = lax.dynamic_slice_in_dim(k_pad, start, span, axis=1)
        vw = lax.dynamic_slice_in_dim(v_pad, start, span, axis=1)
        s = jnp.einsum("bqgrd,bkgd->bgrqk", qb, kw) * ATTN_SCALE
        t = start + q_off
        kpos = start - WINDOW + k_off
        diff = t[:, None] - kpos[None, :]
        mask = (diff >= 0) & (diff < WINDOW) & (kpos >= 0)[None, :]
        p = masked_softmax(s, mask).astype(v.dtype)
        return jnp.einsum("bgrqk,bkgd->bqgrd", p, vw)

    out = lax.map(one_block, jnp.arange(nb, dtype=jnp.int32))
    return jnp.swapaxes(out, 0, 1).reshape(bsz, seq, G, R, Dh)


def nsa_mixer(h, pos, q_w, o_w, k_cmp, v_cmp, k_slc, v_slc, k_win, v_win):
    bsz, seq, _ = h.shape
    G, R, Dh = NSA_KV_HEADS, NSA_Q_PER_KV, NSA_HEAD_DIM
    qg = h @ q_w
    q = rope(qg[..., :NSA_Q_DIM].reshape(bsz, seq, NSA_Q_HEADS, Dh), pos).reshape(bsz, seq, G, R, Dh)
    gates = jax.nn.sigmoid(qg[..., NSA_Q_DIM:].astype(jnp.float32)).reshape(bsz, seq, G, R, NSA_N_BRANCH)
    o_cmp, importance = compressed_branch(q, k_cmp, v_cmp)
    blk_idx, blk_valid = select_blocks(importance)
    o_slc = selected_branch(q, k_slc, v_slc, blk_idx, blk_valid)
    o_win = window_branch(q, k_win, v_win)
    o = gates[..., 0:1] * o_cmp + gates[..., 1:2] * o_slc + gates[..., 2:3] * o_win
    return o.astype(h.dtype).reshape(bsz, seq, NSA_Q_DIM) @ o_w


def clamped_swiglu(u):
    glu = jnp.minimum(u[..., :D_EXPERT], SWIGLU_LIMIT)
    lin = jnp.clip(u[..., D_EXPERT:], -SWIGLU_LIMIT, SWIGLU_LIMIT)
    return glu * jax.nn.sigmoid(SWIGLU_ALPHA * glu) * (lin + 1.0)


def moe(h, router_w, router_b, w_up, b_up, w_down, b_down):
    bsz, seq, D = h.shape
    T = bsz * seq
    xt = h.reshape(T, D)
    logits = (xt @ router_w + router_b).astype(jnp.float32)
    top_v, top_i = lax.top_k(logits, TOP_K)
    gate = jax.nn.softmax(top_v, axis=-1).astype(h.dtype)
    A = T * TOP_K
    eid = top_i.reshape(A)
    tok = jnp.arange(A, dtype=jnp.int32) // TOP_K
    wts = gate.reshape(A)
    order = jnp.argsort(eid)
    se = eid[order]
    counts = jnp.bincount(eid, length=N_EXPERTS)
    padded = (counts + MOE_ROW_BLOCK - 1) // MOE_ROW_BLOCK * MOE_ROW_BLOCK
    off = jnp.cumsum(counts) - counts
    poff = jnp.cumsum(padded) - padded
    dest = poff[se] + jnp.arange(A, dtype=jnp.int32) - off[se]
    R_rows = A + N_EXPERTS * MOE_ROW_BLOCK
    n_blk = R_rows // MOE_ROW_BLOCK
    row_tok = jnp.full((R_rows,), T, jnp.int32).at[dest].set(tok[order])
    row_w = jnp.zeros((R_rows,), h.dtype).at[dest].set(wts[order])
    blk_start = jnp.arange(n_blk, dtype=jnp.int32) * MOE_ROW_BLOCK
    blk_e = jnp.minimum(jnp.searchsorted(jnp.cumsum(padded), blk_start, side="right"), N_EXPERTS - 1)
    x_pad = jnp.concatenate([xt, jnp.zeros((1, D), h.dtype)], axis=0)
    xg = x_pad[row_tok].reshape(n_blk, MOE_ROW_BLOCK, D)

    def expert_rows(args):
        xb, e = args
        return clamped_swiglu(xb @ w_up[e] + b_up[e]) @ w_down[e] + b_down[e]

    y = lax.map(expert_rows, (xg, blk_e)).reshape(R_rows, D)
    out = jnp.zeros((T + 1, D), h.dtype).at[row_tok].add(y * row_w[:, None])[:T]
    return out.reshape(bsz, seq, D)


def reference(x, c, pos, ada_w, ada_b, ln_g, ln_b, ssm_in_w, ssm_conv_w, ssm_conv_b, ssm_dt_bias,
              ssm_a_log, ssm_d, ssm_norm_w, ssm_out_w, kv_ada_w, kv_ada_b, kv_w, cmp_pos,
              phi_k_w1, phi_k_w2, phi_v_w1, phi_v_w2, nsa_q_w, nsa_o_w, router_w, router_b,
              moe_w_up, moe_b_up, moe_w_down, moe_b_down):
    c_act = jax.nn.silu(c)
    shared = None
    for i in range(DEPTH):
        mod = c_act @ ada_w[i] + ada_b[i]
        sh_t, sc_t, g_t, sh_c, sc_c, g_c = jnp.split(mod, 6, axis=-1)
        h = modulate(x, sh_t, sc_t)
        if i < N_A:
            y = mamba2_mixer(h, ssm_in_w[i], ssm_conv_w[i], ssm_conv_b[i], ssm_dt_bias[i],
                             ssm_a_log[i], ssm_d[i], ssm_norm_w[i], ssm_out_w[i])
        else:
            j = i - N_A
            y = nsa_mixer(h, pos, nsa_q_w[j], nsa_o_w[j], *shared)
        x = post_norm(x, y, g_t, ln_g[i, 0], ln_b[i, 0])
        y = moe(modulate(x, sh_c, sc_c), router_w[i], router_b[i], moe_w_up[i], moe_b_up[i],
                moe_w_down[i], moe_b_down[i])
        x = post_norm(x, y, g_c, ln_g[i, 1], ln_b[i, 1])
        if i == N_A - 1:
            shared = nsa_shared_kv(x, c_act, pos, kv_ada_w, kv_ada_b, kv_w, cmp_pos,
                                   phi_k_w1, phi_k_w2, phi_v_w1, phi_v_w2)
    return x
```

```python
import functools

import jax
import jax.numpy as jnp
import numpy as np
from jax import lax
from jax.experimental import pallas as pl
from jax.experimental.pallas import tpu as pltpu

F32 = jnp.float32
BF16 = jnp.bfloat16

DEPTH = 4
N_A = DEPTH // 2
ALPHA = (2.0 * DEPTH) ** 0.25
LN_EPS = 1e-5

SSM_HEAD_DIM = 64
SSM_GROUPS = 4
SSM_STATE = 128
CONV_WIDTH = 4
SSM_CHUNK = 128

NSA_HEAD_DIM = 64
NSA_KV_HEADS = 4
NSA_N_BRANCH = 3
CMP_BLOCK = 32
CMP_STRIDE = 16
SLC_BLOCK = 64
N_SELECT = 16
WINDOW = 512
SLC_QUERY_BLOCK = 32
WIN_QUERY_BLOCK = 128
ROPE_THETA = 10000.0
ATTN_SCALE = NSA_HEAD_DIM ** -0.5
FORCED_SCORE = 1e9

TOP_K = 4
SWIGLU_LIMIT = 7.0
SWIGLU_ALPHA = 1.702

VMEM_LIMIT_BYTES_V7X = 56 * 1024 * 1024
LANES = 128


def _cparams(sem):
    return pltpu.CompilerParams(dimension_semantics=sem, vmem_limit_bytes=VMEM_LIMIT_BYTES_V7X)


def _router_kernel(x_ref, sc_ref, sh_ref, rw_ref, rb_ref, h_ref, idx_ref, gate_ref):
    h = x_ref[0] * (1.0 + sc_ref[0]) + sh_ref[0]
    hb = h.astype(BF16)
    h_ref[0] = hb
    logits = jnp.dot(hb, rw_ref[...], preferred_element_type=F32) + rb_ref[...]
    n_exp = logits.shape[-1]
    lane_e = lax.broadcasted_iota(jnp.int32, logits.shape, 1)
    lane_o = lax.broadcasted_iota(jnp.int32, idx_ref.shape[1:], 1)
    vals, idxs = [], []
    l = logits
    for _ in range(TOP_K):
        m = jnp.max(l, axis=-1, keepdims=True)
        idx = jnp.min(jnp.where(l == m, lane_e, n_exp), axis=-1, keepdims=True)
        vals.append(m)
        idxs.append(idx)
        l = jnp.where(lane_e == idx, -jnp.inf, l)
    es = [jnp.exp(v - vals[0]) for v in vals]
    denom = es[0] + es[1] + es[2] + es[3]
    idx_out = jnp.zeros(idx_ref.shape[1:], jnp.int32)
    gate_out = jnp.zeros(gate_ref.shape[1:], F32)
    for k in range(TOP_K):
        idx_out = jnp.where(lane_o == k, idxs[k], idx_out)
        gate_out = jnp.where(lane_o == k, es[k] / denom, gate_out)
    idx_ref[0] = idx_out
    gate_ref[0] = gate_out


def _router(x, sc, sh, rw, rb, tm=512):
    B, S, D = x.shape
    E = rw.shape[-1]
    grid = (B, S // tm)
    return pl.pallas_call(
        _router_kernel,
        grid=grid,
        in_specs=[
            pl.BlockSpec((1, tm, D), lambda b, s: (b, s, 0)),
            pl.BlockSpec((1, 1, D), lambda b, s: (b, 0, 0)),
            pl.BlockSpec((1, 1, D), lambda b, s: (b, 0, 0)),
            pl.BlockSpec((D, E), lambda b, s: (0, 0)),
            pl.BlockSpec((1, E), lambda b, s: (0, 0)),
        ],
        out_specs=[
            pl.BlockSpec((1, tm, D), lambda b, s: (b, s, 0)),
            pl.BlockSpec((1, tm, LANES), lambda b, s: (b, s, 0)),
            pl.BlockSpec((1, tm, LANES), lambda b, s: (b, s, 0)),
        ],
        out_shape=[
            jax.ShapeDtypeStruct((B, S, D), BF16),
            jax.ShapeDtypeStruct((B, S, LANES), jnp.int32),
            jax.ShapeDtypeStruct((B, S, LANES), F32),
        ],
        compiler_params=_cparams(("parallel", "parallel")),
        name="moe_router",
    )(x, sc, sh, rw.astype(BF16), rb.reshape(1, E))


def _ffn_kernel(blk_e_ref, first_ref, nused_ref, x_ref, wu_ref, bu_ref, wd_ref, bd_ref, o_ref, wu_s, wd_s):
    i = pl.program_id(0)
    d_exp = wd_s.shape[0]

    @pl.when(i < nused_ref[0])
    def _():
        @pl.when(first_ref[i] == 1)
        def _():
            wu_s[...] = wu_ref[0].astype(BF16)
            wd_s[...] = wd_ref[0].astype(BF16)

        u = jnp.dot(x_ref[...], wu_s[...], preferred_element_type=F32) + bu_ref[0]
        glu = jnp.minimum(u[:, :d_exp], SWIGLU_LIMIT)
        lin = jnp.clip(u[:, d_exp:], -SWIGLU_LIMIT, SWIGLU_LIMIT)
        act = glu * jax.nn.sigmoid(SWIGLU_ALPHA * glu) * (lin + 1.0)
        o_ref[...] = jnp.dot(act.astype(BF16), wd_s[...], preferred_element_type=F32) + bd_ref[0]

    @pl.when(i >= nused_ref[0])
    def _():
        o_ref[...] = jnp.zeros_like(o_ref)


def _expert_ffn(xg, blk_e, first, nused, w_up, b_up, w_down, b_down, tm):
    R, D = xg.shape
    E, _, N2 = w_up.shape
    d_exp = w_down.shape[1]
    n_blk = R // tm
    grid_spec = pltpu.PrefetchScalarGridSpec(
        num_scalar_prefetch=3,
        grid=(n_blk,),
        in_specs=[
            pl.BlockSpec((tm, D), lambda i, be, fi, nu: (i, 0)),
            pl.BlockSpec((1, D, N2), lambda i, be, fi, nu: (be[i], 0, 0)),
            pl.BlockSpec((1, 1, N2), lambda i, be, fi, nu: (be[i], 0, 0)),
            pl.BlockSpec((1, d_exp, D), lambda i, be, fi, nu: (be[i], 0, 0)),
            pl.BlockSpec((1, 1, D), lambda i, be, fi, nu: (be[i], 0, 0)),
        ],
        out_specs=pl.BlockSpec((tm, D), lambda i, be, fi, nu: (i, 0)),
        scratch_shapes=[pltpu.VMEM((D, N2), BF16), pltpu.VMEM((d_exp, D), BF16)],
    )
    return pl.pallas_call(
        _ffn_kernel,
        grid_spec=grid_spec,
        out_shape=jax.ShapeDtypeStruct((R, D), F32),
        compiler_params=_cparams(("arbitrary",)),
        name="moe_expert_ffn",
    )(blk_e, first, nused, xg, w_up, b_up.reshape(E, 1, N2), w_down, b_down.reshape(E, 1, D))


def _layer_norm_rows(v, g, b):
    mu = jnp.mean(v, axis=-1, keepdims=True)
    d = v - mu
    var = jnp.mean(d * d, axis=-1, keepdims=True)
    return d * lax.rsqrt(var + LN_EPS) * g + b


def _combine_norm_kernel(x_ref, yg_ref, gate_ref, gc_ref, g_ref, b_ref, o_ref):
    gate = gate_ref[0]
    y = yg_ref[0, 0] * gate[:, 0:1]
    for k in range(1, TOP_K):
        y = y + yg_ref[k, 0] * gate[:, k:k + 1]
    v = ALPHA * x_ref[0] + (1.0 + gc_ref[0]) * y
    o_ref[0] = _layer_norm_rows(v, g_ref[...], b_ref[...])


def _combine_norm(x, yg, gate, gate_c, ln_g, ln_b, tm=256):
    B, S, D = x.shape
    return pl.pallas_call(
        _combine_norm_kernel,
        grid=(B, S // tm),
        in_specs=[
            pl.BlockSpec((1, tm, D), lambda b, s: (b, s, 0)),
            pl.BlockSpec((TOP_K, 1, tm, D), lambda b, s: (0, b, s, 0)),
            pl.BlockSpec((1, tm, LANES), lambda b, s: (b, s, 0)),
            pl.BlockSpec((1, 1, D), lambda b, s: (b, 0, 0)),
            pl.BlockSpec((1, D), lambda b, s: (0, 0)),
            pl.BlockSpec((1, D), lambda b, s: (0, 0)),
        ],
        out_specs=pl.BlockSpec((1, tm, D), lambda b, s: (b, s, 0)),
        out_shape=jax.ShapeDtypeStruct((B, S, D), F32),
        compiler_params=_cparams(("parallel", "parallel")),
        name="moe_combine_norm",
    )(x, yg, gate, gate_c, ln_g.reshape(1, D), ln_b.reshape(1, D))


def _moe_layer(x, sh, sc, gate_c, router_w, router_b, w_up, b_up, w_down, b_down, ln_g, ln_b, row_block=256):
    B, S, D = x.shape
    T = B * S
    E = router_w.shape[-1]
    h, top_i, gate = _router(x, sc, sh, router_w, router_b)
    A = T * TOP_K
    eid = top_i[:, :, :TOP_K].reshape(A)
    order = jnp.argsort(eid).astype(jnp.int32)
    se = eid[order]
    counts = jnp.bincount(eid, length=E).astype(jnp.int32)
    padded = (counts + row_block - 1) // row_block * row_block
    off = jnp.cumsum(counts) - counts
    cum_padded = jnp.cumsum(padded)
    poff = cum_padded - padded
    dest = poff[se] + jnp.arange(A, dtype=jnp.int32) - off[se]
    R_rows = A + E * row_block
    n_blk = R_rows // row_block
    row_tok = jnp.full((R_rows,), T, jnp.int32).at[dest].set(order // TOP_K)
    pos = jnp.zeros((A,), jnp.int32).at[order].set(dest)
    nused = (cum_padded[-1] // row_block).astype(jnp.int32)
    blk_start = jnp.arange(n_blk, dtype=jnp.int32) * row_block
    blk_e = jnp.minimum(jnp.searchsorted(cum_padded, blk_start, side="right"), E - 1).astype(jnp.int32)
    last_e = blk_e[jnp.maximum(nused - 1, 0)]
    blk_e = jnp.where(jnp.arange(n_blk) < nused, blk_e, last_e)
    first = jnp.concatenate([jnp.ones((1,), jnp.int32), (blk_e[1:] != blk_e[:-1]).astype(jnp.int32)])
    h_pad = jnp.concatenate([h.reshape(T, D), jnp.zeros((1, D), h.dtype)], axis=0)
    xg = h_pad[row_tok]
    y = _expert_ffn(xg, blk_e, first, nused.reshape(1), w_up, b_up, w_down, b_down, row_block)
    yg = y[pos.reshape(T, TOP_K).T].reshape(TOP_K, B, S, D)
    return _combine_norm(x, yg, gate, gate_c, ln_g, ln_b)


def _layer_norm(x, g, b):
    mu = jnp.mean(x, axis=-1, keepdims=True)
    var = jnp.mean(jnp.square(x - mu), axis=-1, keepdims=True)
    return (x - mu) * lax.rsqrt(var + LN_EPS) * g + b


def _modulate(x, shift, scale):
    return x * (1.0 + scale[:, None, :]) + shift[:, None, :]


def _post_norm(x, y, gate, g, b):
    return _layer_norm(ALPHA * x + (1.0 + gate[:, None, :]) * y, g, b)


def _rope(x, pos):
    half = x.shape[-1] // 2
    inv = ROPE_THETA ** (-jnp.arange(half, dtype=F32) / half)
    ang = pos.astype(F32)[..., None] * inv
    cos = jnp.cos(ang)[:, :, None, :]
    sin = jnp.sin(ang)[:, :, None, :]
    x1 = x[..., :half]
    x2 = x[..., half:]
    return jnp.concatenate([x1 * cos - x2 * sin, x2 * cos + x1 * sin], axis=-1)


def _masked_softmax(s, mask):
    s = jnp.where(mask, s, -jnp.inf)
    m = jnp.max(s, axis=-1, keepdims=True)
    m = jnp.where(jnp.isfinite(m), m, 0.0)
    p = jnp.exp(s - m)
    return p / jnp.maximum(jnp.sum(p, axis=-1, keepdims=True), jnp.finfo(F32).tiny)


def _ssd_chunked(xs, dt, a_neg, bm, cm):
    bsz, seq = xs.shape[:2]
    G, R, P = xs.shape[2:]
    N = bm.shape[-1]
    nc, L = seq // SSM_CHUNK, SSM_CHUNK
    xdt = (xs * dt[..., None]).reshape(bsz, nc, L, G, R, P)
    a_cs = jnp.cumsum((dt * a_neg).reshape(bsz, nc, L, G, R), axis=2)
    bc = bm.reshape(bsz, nc, L, G, N)
    cc = cm.reshape(bsz, nc, L, G, N)
    seg = a_cs[:, :, :, None] - a_cs[:, :, None, :]
    tri = jnp.tril(jnp.ones((L, L), dtype=bool))
    decay = jnp.exp(jnp.where(tri[:, :, None, None], seg, -jnp.inf))
    cb = jnp.einsum("bclgn,bcsgn->bclsg", cc, bc)
    y_diag = jnp.einsum("bclsg,bclsgr,bcsgrp->bclgrp", cb, decay, xdt)
    decay_end = jnp.exp(a_cs[:, :, -1:] - a_cs)
    states = jnp.einsum("bclgn,bclgr,bclgrp->bcgrpn", bc, decay_end, xdt)
    chunk_decay = jnp.exp(a_cs[:, :, -1])

    def step(carry, inp):
        st, dec = inp
        return carry * dec[..., None, None] + st, carry

    init = jnp.zeros((bsz, G, R, P, N), F32)
    _, prev = lax.scan(step, init, (jnp.swapaxes(states, 0, 1), jnp.swapaxes(chunk_decay, 0, 1)))
    prev = jnp.swapaxes(prev, 0, 1)
    y_off = jnp.einsum("bclgn,bcgrpn,bclgr->bclgrp", cc, prev, jnp.exp(a_cs))
    return (y_diag + y_off).reshape(bsz, seq, G, R, P)


def _mamba2_mixer(h, in_w, conv_w, conv_b, dt_bias, a_log, d_skip, norm_w, out_w):
    bsz, seq, _ = h.shape
    d_inner = out_w.shape[0]
    heads = dt_bias.shape[0]
    G, P, N = SSM_GROUPS, SSM_HEAD_DIM, SSM_STATE
    R = heads // G
    bc_dim = G * N
    conv_dim = d_inner + 2 * bc_dim
    zxbcdt = h @ in_w
    z = zxbcdt[..., :d_inner]
    xbc = zxbcdt[..., d_inner:d_inner + conv_dim]
    dt_raw = zxbcdt[..., d_inner + conv_dim:]
    xbc = lax.conv_general_dilated(xbc, conv_w[:, None, :], window_strides=(1,),
                                   padding=[(CONV_WIDTH - 1, 0)],
                                   dimension_numbers=("NWC", "WIO", "NWC"),
                                   feature_group_count=conv_dim)
    xbc = jax.nn.silu(xbc + conv_b)
    xs = xbc[..., :d_inner].reshape(bsz, seq, G, R, P)
    bm = xbc[..., d_inner:d_inner + bc_dim].reshape(bsz, seq, G, N)
    cm = xbc[..., d_inner + bc_dim:].reshape(bsz, seq, G, N)
    dt = jax.nn.softplus(dt_raw + dt_bias).reshape(bsz, seq, G, R)
    a_neg = -jnp.exp(a_log).reshape(G, R)
    y = _ssd_chunked(xs, dt, a_neg, bm, cm) + xs * d_skip.reshape(G, R)[:, :, None]
    y = y.reshape(bsz, seq, d_inner) * jax.nn.silu(z)
    yg = y.reshape(bsz, seq, G, d_inner // G)
    yg = yg * lax.rsqrt(jnp.mean(jnp.square(yg), axis=-1, keepdims=True) + LN_EPS)
    y = yg.reshape(bsz, seq, d_inner) * norm_w
    return y @ out_w


def _compress_blocks(k, cmp_pos, w1, w2):
    bsz, seq = k.shape[:2]
    sub = k.reshape(bsz, seq // CMP_STRIDE, CMP_STRIDE, NSA_KV_HEADS, NSA_HEAD_DIM)
    blocks = jnp.concatenate([sub[:, :-1], sub[:, 1:]], axis=2) + cmp_pos[:, None, :]
    n_cmp = blocks.shape[1]
    flat = blocks.transpose(0, 1, 3, 2, 4).reshape(bsz, n_cmp, NSA_KV_HEADS, CMP_BLOCK * NSA_HEAD_DIM)
    return jax.nn.gelu(flat @ w1) @ w2


def _nsa_shared_kv(h, c_act, pos, kv_ada_w, kv_ada_b, kv_w, cmp_pos, phi_k_w1, phi_k_w2, phi_v_w1, phi_v_w2):
    bsz, seq, _ = h.shape
    shift, scale = jnp.split(c_act @ kv_ada_w + kv_ada_b, 2, axis=-1)
    kv = (_modulate(h, shift, scale) @ kv_w).reshape(bsz, seq, 2 * NSA_N_BRANCH, NSA_KV_HEADS, NSA_HEAD_DIM)
    k_cmp = _compress_blocks(_rope(kv[:, :, 0], pos), cmp_pos, phi_k_w1, phi_k_w2)
    v_cmp = _compress_blocks(kv[:, :, 1], cmp_pos, phi_v_w1, phi_v_w2)
    return (k_cmp, v_cmp, _rope(kv[:, :, 2], pos), kv[:, :, 3], _rope(kv[:, :, 4], pos), kv[:, :, 5])


def _cmp_slc_overlap(n_cmp, n_slc):
    cs = jnp.arange(n_cmp)[:, None] * CMP_STRIDE
    js = jnp.arange(n_slc)[None, :] * SLC_BLOCK
    ov = jnp.minimum(cs + CMP_BLOCK, js + SLC_BLOCK) - jnp.maximum(cs, js)
    return jnp.maximum(ov, 0).astype(F32) / CMP_BLOCK


def _compressed_branch(q, k_cmp, v_cmp):
    seq = q.shape[1]
    n_cmp = k_cmp.shape[1]
    s = jnp.einsum("bsgrd,bngd->bgrsn", q, k_cmp) * ATTN_SCALE
    t = jnp.arange(seq)
    blk_end = jnp.arange(n_cmp) * CMP_STRIDE + CMP_BLOCK - 1
    p = _masked_softmax(s, blk_end[None, :] <= t[:, None])
    o = jnp.einsum("bgrsn,bngd->bsgrd", p, v_cmp)
    importance = jnp.einsum("bgrsn,nj->bgsj", p, _cmp_slc_overlap(n_cmp, seq // SLC_BLOCK))
    return o, importance


def _select_blocks(importance):
    seq, n_slc = importance.shape[-2:]
    n_sel = min(N_SELECT, n_slc)
    t_blk = jnp.arange(seq)[:, None] // SLC_BLOCK
    j = jnp.arange(n_slc)[None, :]
    forced = (j == 0) | (j == t_blk) | (j == t_blk - 1)
    score = jnp.where(forced, FORCED_SCORE, importance)
    score = jnp.where(j <= t_blk, score, -jnp.inf)
    top_v, idx = lax.top_k(score, n_sel)
    return idx, jnp.isfinite(top_v)


def _selected_branch(q, k, v, blk_idx, blk_valid):
    bsz, seq = q.shape[:2]
    G, R, Dh = q.shape[2:]
    n_slc = seq // SLC_BLOCK
    n_sel = blk_idx.shape[-1]
    QB = SLC_QUERY_BLOCK
    nq = seq // QB
    kb = k.reshape(bsz, n_slc, SLC_BLOCK, G, Dh).transpose(0, 3, 1, 2, 4)
    vb = v.reshape(bsz, n_slc, SLC_BLOCK, G, Dh).transpose(0, 3, 1, 2, 4)
    q_blocks = jnp.swapaxes(q.reshape(bsz, nq, QB, G, R, Dh), 0, 1)
    idx_blocks = blk_idx.reshape(bsz, G, nq, QB, n_sel).transpose(2, 0, 1, 3, 4)
    val_blocks = blk_valid.reshape(bsz, G, nq, QB, n_sel).transpose(2, 0, 1, 3, 4)
    t_blocks = jnp.arange(seq, dtype=jnp.int32).reshape(nq, QB)
    b_ix = jnp.arange(bsz)[:, None, None, None]
    g_ix = jnp.arange(G)[None, :, None, None]
    offs = jnp.arange(SLC_BLOCK, dtype=jnp.int32)

    def one_block(args):
        qb, ib, valb, tb = args
        kg = kb[b_ix, g_ix, ib]
        vg = vb[b_ix, g_ix, ib]
        s = jnp.einsum("bqgrd,bgqnkd->bgrqnk", qb, kg) * ATTN_SCALE
        kpos = ib[..., None] * SLC_BLOCK + offs
        mask = valb[..., None] & (kpos <= tb[None, None, :, None, None])
        p = _masked_softmax(s.reshape(bsz, G, R, QB, n_sel * SLC_BLOCK),
                            mask.reshape(bsz, G, 1, QB, n_sel * SLC_BLOCK))
        p = p.reshape(bsz, G, R, QB, n_sel, SLC_BLOCK)
        return jnp.einsum("bgrqnk,bgqnkd->bqgrd", p, vg)

    out = lax.map(one_block, (q_blocks, idx_blocks, val_blocks, t_blocks))
    return jnp.swapaxes(out, 0, 1).reshape(bsz, seq, G, R, Dh)


def _window_branch(q, k, v):
    bsz, seq = q.shape[:2]
    QB = WIN_QUERY_BLOCK
    nb = seq // QB
    span = WINDOW + QB
    k_pad = jnp.pad(k, ((0, 0), (WINDOW, 0), (0, 0), (0, 0)))
    v_pad = jnp.pad(v, ((0, 0), (WINDOW, 0), (0, 0), (0, 0)))
    q_off = jnp.arange(QB, dtype=jnp.int32)
    k_off = jnp.arange(span, dtype=jnp.int32)

    def one_block(i):
        start = i * QB
        qb = lax.dynamic_slice_in_dim(q, start, QB, axis=1)
        kw = lax.dynamic_slice_in_dim(k_pad, start, span, axis=1)
        vw = lax.dynamic_slice_in_dim(v_pad, start, span, axis=1)
        s = jnp.einsum("bqgrd,bkgd->bgrqk", qb, kw) * ATTN_SCALE
        t = start + q_off
        kpos = start - WINDOW + k_off
        diff = t[:, None] - kpos[None, :]
        mask = (diff >= 0) & (diff < WINDOW) & (kpos >= 0)[None, :]
        p = _masked_softmax(s, mask)
        return jnp.einsum("bgrqk,bkgd->bqgrd", p, vw)

    out = lax.map(one_block, jnp.arange(nb, dtype=jnp.int32))
    return jnp.swapaxes(out, 0, 1).reshape(bsz, seq, *q.shape[2:])


def _nsa_mixer(h, pos, q_w, o_w, k_cmp, v_cmp, k_slc, v_slc, k_win, v_win):
    bsz, seq, _ = h.shape
    G, Dh = NSA_KV_HEADS, NSA_HEAD_DIM
    q_dim = o_w.shape[0]
    heads = q_dim // Dh
    R = heads // G
    qg = h @ q_w
    q = _rope(qg[..., :q_dim].reshape(bsz, seq, heads, Dh), pos).reshape(bsz, seq, G, R, Dh)
    gates = jax.nn.sigmoid(qg[..., q_dim:]).reshape(bsz, seq, G, R, NSA_N_BRANCH)
    o_cmp, importance = _compressed_branch(q, k_cmp, v_cmp)
    blk_idx, blk_valid = _select_blocks(importance)
    o_slc = _selected_branch(q, k_slc, v_slc, blk_idx, blk_valid)
    o_win = _window_branch(q, k_win, v_win)
    o = gates[..., 0:1] * o_cmp + gates[..., 1:2] * o_slc + gates[..., 2:3] * o_win
    return o.reshape(bsz, seq, q_dim) @ o_w


def kernel(x, c, pos, ada_w, ada_b, ln_g, ln_b, ssm_in_w, ssm_conv_w, ssm_conv_b, ssm_dt_bias, ssm_a_log, ssm_d, ssm_norm_w, ssm_out_w, kv_ada_w, kv_ada_b, kv_w, cmp_pos, phi_k_w1, phi_k_w2, phi_v_w1, phi_v_w2, nsa_q_w, nsa_o_w, router_w, router_b, moe_w_up, moe_b_up, moe_w_down, moe_b_down):
    c_act = jax.nn.silu(c)
    shared = None
    for i in range(DEPTH):
        mod = c_act @ ada_w[i] + ada_b[i]
        sh_t, sc_t, g_t, sh_c, sc_c, g_c = jnp.split(mod, 6, axis=-1)
        h = _modulate(x, sh_t, sc_t)
        if i < N_A:
            y = _mamba2_mixer(h, ssm_in_w[i], ssm_conv_w[i], ssm_conv_b[i], ssm_dt_bias[i],
                              ssm_a_log[i], ssm_d[i], ssm_norm_w[i], ssm_out_w[i])
        else:
            j = i - N_A
            y = _nsa_mixer(h, pos, nsa_q_w[j], nsa_o_w[j], *shared)
        x = _post_norm(x, y, g_t, ln_g[i, 0], ln_b[i, 0])
        x = _moe_layer(x, sh_c[:, None, :], sc_c[:, None, :], g_c[:, None, :], router_w[i], router_b[i],
                       moe_w_up[i], moe_b_up[i], moe_w_down[i], moe_b_down[i], ln_g[i, 1], ln_b[i, 1])
        if i == N_A - 1:
            shared = _nsa_shared_kv(x, c_act, pos, kv_ada_w, kv_ada_b, kv_w, cmp_pos,
                                    phi_k_w1, phi_k_w2, phi_v_w1, phi_v_w2)
    return x
```

```python
import functools

import jax
import jax.numpy as jnp
import numpy as np
from jax import lax
from jax.experimental import pallas as pl
from jax.experimental.pallas import tpu as pltpu

F32 = jnp.float32
BF16 = jnp.bfloat16

DEPTH = 4
N_A = DEPTH // 2
ALPHA = (2.0 * DEPTH) ** 0.25
LN_EPS = 1e-5

SSM_HEAD_DIM = 64
SSM_GROUPS = 4
SSM_STATE = 128
CONV_WIDTH = 4
SSM_CHUNK = 128

NSA_HEAD_DIM = 64
NSA_KV_HEADS = 4
NSA_N_BRANCH = 3
CMP_BLOCK = 32
CMP_STRIDE = 16
SLC_BLOCK = 64
N_SELECT = 16
WINDOW = 512
SLC_QUERY_BLOCK = 32
WIN_QUERY_BLOCK = 128
ROPE_THETA = 10000.0
ATTN_SCALE = NSA_HEAD_DIM ** -0.5
FORCED_SCORE = 1e9

TOP_K = 4
SWIGLU_LIMIT = 7.0
SWIGLU_ALPHA = 1.702

VMEM_LIMIT_BYTES_V7X = 56 * 1024 * 1024
LANES = 128


def _cparams(sem):
    return pltpu.CompilerParams(dimension_semantics=sem, vmem_limit_bytes=VMEM_LIMIT_BYTES_V7X)


def _router_kernel(x_ref, sc_ref, sh_ref, rw_ref, rb_ref, h_ref, idx_ref, gate_ref):
    h = x_ref[0] * (1.0 + sc_ref[0]) + sh_ref[0]
    hb = h.astype(BF16)
    h_ref[0] = hb
    logits = jnp.dot(hb, rw_ref[...], preferred_element_type=F32) + rb_ref[...]
    n_exp = logits.shape[-1]
    lane_e = lax.broadcasted_iota(jnp.int32, logits.shape, 1)
    lane_o = lax.broadcasted_iota(jnp.int32, idx_ref.shape[1:], 1)
    vals, idxs = [], []
    l = logits
    for _ in range(TOP_K):
        m = jnp.max(l, axis=-1, keepdims=True)
        idx = jnp.min(jnp.where(l == m, lane_e, n_exp), axis=-1, keepdims=True)
        vals.append(m)
        idxs.append(idx)
        l = jnp.where(lane_e == idx, -jnp.inf, l)
    es = [jnp.exp(v - vals[0]) for v in vals]
    denom = es[0] + es[1] + es[2] + es[3]
    idx_out = jnp.zeros(idx_ref.shape[1:], jnp.int32)
    gate_out = jnp.zeros(gate_ref.shape[1:], F32)
    for k in range(TOP_K):
        idx_out = jnp.where(lane_o == k, idxs[k], idx_out)
        gate_out = jnp.where(lane_o == k, es[k] / denom, gate_out)
    idx_ref[0] = idx_out
    gate_ref[0] = gate_out


def _router(x, sc, sh, rw, rb, tm=512):
    B, S, D = x.shape
    E = rw.shape[-1]
    grid = (B, S // tm)
    return pl.pallas_call(
        _router_kernel,
        grid=grid,
        in_specs=[
            pl.BlockSpec((1, tm, D), lambda b, s: (b, s, 0)),
            pl.BlockSpec((1, 1, D), lambda b, s: (b, 0, 0)),
            pl.BlockSpec((1, 1, D), lambda b, s: (b, 0, 0)),
            pl.BlockSpec((D, E), lambda b, s: (0, 0)),
            pl.BlockSpec((1, E), lambda b, s: (0, 0)),
        ],
        out_specs=[
            pl.BlockSpec((1, tm, D), lambda b, s: (b, s, 0)),
            pl.BlockSpec((1, tm, LANES), lambda b, s: (b, s, 0)),
            pl.BlockSpec((1, tm, LANES), lambda b, s: (b, s, 0)),
        ],
        out_shape=[
            jax.ShapeDtypeStruct((B, S, D), BF16),
            jax.ShapeDtypeStruct((B, S, LANES), jnp.int32),
            jax.ShapeDtypeStruct((B, S, LANES), F32),
        ],
        compiler_params=_cparams(("parallel", "parallel")),
        name="moe_router",
    )(x, sc, sh, rw.astype(BF16), rb.reshape(1, E))


def _ffn_kernel(blk_e_ref, first_ref, nused_ref, x_ref, wu_ref, bu_ref, wd_ref, bd_ref, o_ref, wu_s, wd_s):
    i = pl.program_id(0)
    d_exp = wd_s.shape[0]

    @pl.when(i < nused_ref[0])
    def _():
        @pl.when(first_ref[i] == 1)
        def _():
            wu_s[...] = wu_ref[0].astype(BF16)
            wd_s[...] = wd_ref[0].astype(BF16)

        u = jnp.dot(x_ref[...], wu_s[...], preferred_element_type=F32) + bu_ref[0]
        glu = jnp.minimum(u[:, :d_exp], SWIGLU_LIMIT)
        lin = jnp.clip(u[:, d_exp:], -SWIGLU_LIMIT, SWIGLU_LIMIT)
        act = glu * jax.nn.sigmoid(SWIGLU_ALPHA * glu) * (lin + 1.0)
        o_ref[...] = jnp.dot(act.astype(BF16), wd_s[...], preferred_element_type=F32) + bd_ref[0]

    @pl.when(i >= nused_ref[0])
    def _():
        o_ref[...] = jnp.zeros_like(o_ref)


def _expert_ffn(xg, blk_e, first, nused, w_up, b_up, w_down, b_down, tm):
    R, D = xg.shape
    E, _, N2 = w_up.shape
    d_exp = w_down.shape[1]
    n_blk = R // tm
    grid_spec = pltpu.PrefetchScalarGridSpec(
        num_scalar_prefetch=3,
        grid=(n_blk,),
        in_specs=[
            pl.BlockSpec((tm, D), lambda i, be, fi, nu: (i, 0)),
            pl.BlockSpec((1, D, N2), lambda i, be, fi, nu: (be[i], 0, 0)),
            pl.BlockSpec((1, 1, N2), lambda i, be, fi, nu: (be[i], 0, 0)),
            pl.BlockSpec((1, d_exp, D), lambda i, be, fi, nu: (be[i], 0, 0)),
            pl.BlockSpec((1, 1, D), lambda i, be, fi, nu: (be[i], 0, 0)),
        ],
        out_specs=pl.BlockSpec((tm, D), lambda i, be, fi, nu: (i, 0)),
        scratch_shapes=[pltpu.VMEM((D, N2), BF16), pltpu.VMEM((d_exp, D), BF16)],
    )
    return pl.pallas_call(
        _ffn_kernel,
        grid_spec=grid_spec,
        out_shape=jax.ShapeDtypeStruct((R, D), F32),
        compiler_params=_cparams(("arbitrary",)),
        name="moe_expert_ffn",
    )(blk_e, first, nused, xg, w_up, b_up.reshape(E, 1, N2), w_down, b_down.reshape(E, 1, D))


def _layer_norm_rows(v, g, b):
    mu = jnp.mean(v, axis=-1, keepdims=True)
    d = v - mu
    var = jnp.mean(d * d, axis=-1, keepdims=True)
    return d * lax.rsqrt(var + LN_EPS) * g + b


def _combine_norm_kernel(x_ref, yg_ref, gate_ref, gc_ref, g_ref, b_ref, o_ref):
    gate = gate_ref[0]
    y = yg_ref[0, 0] * gate[:, 0:1]
    for k in range(1, TOP_K):
        y = y + yg_ref[k, 0] * gate[:, k:k + 1]
    v = ALPHA * x_ref[0] + (1.0 + gc_ref[0]) * y
    o_ref[0] = _layer_norm_rows(v, g_ref[...], b_ref[...])


def _combine_norm(x, yg, gate, gate_c, ln_g, ln_b, tm=256):
    B, S, D = x.shape
    return pl.pallas_call(
        _combine_norm_kernel,
        grid=(B, S // tm),
        in_specs=[
            pl.BlockSpec((1, tm, D), lambda b, s: (b, s, 0)),
            pl.BlockSpec((TOP_K, 1, tm, D), lambda b, s: (0, b, s, 0)),
            pl.BlockSpec((1, tm, LANES), lambda b, s: (b, s, 0)),
            pl.BlockSpec((1, 1, D), lambda b, s: (b, 0, 0)),
            pl.BlockSpec((1, D), lambda b, s: (0, 0)),
            pl.BlockSpec((1, D), lambda b, s: (0, 0)),
        ],
        out_specs=pl.BlockSpec((1, tm, D), lambda b, s: (b, s, 0)),
        out_shape=jax.ShapeDtypeStruct((B, S, D), F32),
        compiler_params=_cparams(("parallel", "parallel")),
        name="moe_combine_norm",
    )(x, yg, gate, gate_c, ln_g.reshape(1, D), ln_b.reshape(1, D))


def _moe_layer(x, sh, sc, gate_c, router_w, router_b, w_up, b_up, w_down, b_down, ln_g, ln_b, row_block=256):
    B, S, D = x.shape
    T = B * S
    E = router_w.shape[-1]
    h, top_i, gate = _router(x, sc, sh, router_w, router_b)
    A = T * TOP_K
    eid = top_i[:, :, :TOP_K].reshape(A)
    order = jnp.argsort(eid).astype(jnp.int32)
    se = eid[order]
    counts = jnp.bincount(eid, length=E).astype(jnp.int32)
    padded = (counts + row_block - 1) // row_block * row_block
    off = jnp.cumsum(counts) - counts
    cum_padded = jnp.cumsum(padded)
    poff = cum_padded - padded
    dest = poff[se] + jnp.arange(A, dtype=jnp.int32) - off[se]
    R_rows = A + E * row_block
    n_blk = R_rows // row_block
    row_tok = jnp.full((R_rows,), T, jnp.int32).at[dest].set(order // TOP_K)
    pos = jnp.zeros((A,), jnp.int32).at[order].set(dest)
    nused = (cum_padded[-1] // row_block).astype(jnp.int32)
    blk_start = jnp.arange(n_blk, dtype=jnp.int32) * row_block
    blk_e = jnp.minimum(jnp.searchsorted(cum_padded, blk_start, side="right"), E - 1).astype(jnp.int32)
    last_e = blk_e[jnp.maximum(nused - 1, 0)]
    blk_e = jnp.where(jnp.arange(n_blk) < nused, blk_e, last_e)
    first = jnp.concatenate([jnp.ones((1,), jnp.int32), (blk_e[1:] != blk_e[:-1]).astype(jnp.int32)])
    h_pad = jnp.concatenate([h.reshape(T, D), jnp.zeros((1, D), h.dtype)], axis=0)
    xg = h_pad[row_tok]
    y = _expert_ffn(xg, blk_e, first, nused.reshape(1), w_up, b_up, w_down, b_down, row_block)
    yg = y[pos.reshape(T, TOP_K).T].reshape(TOP_K, B, S, D)
    return _combine_norm(x, yg, gate, gate_c, ln_g, ln_b)


NEG_BIG = -1e30


def _attn_kernel(*refs, mode, tq, tk, n_rep):
    if mode == "slc":
        q_ref, sel_ref, et_ref, k_ref, v_ref, o_ref, m_s, l_s, acc_s = refs
    else:
        q_ref, k_ref, v_ref, o_ref, m_s, l_s, acc_s = refs
    i = pl.program_id(2)
    dh = k_ref.shape[-1]
    q = q_ref[0]
    qs = [(q[:, r * dh:(r + 1) * dh] * ATTN_SCALE).astype(BF16) for r in range(n_rep)]
    m_s[...] = jnp.full(m_s.shape, NEG_BIG, F32)
    l_s[...] = jnp.zeros(l_s.shape, F32)
    acc_s[...] = jnp.zeros(acc_s.shape, F32)
    t = i * tq + lax.broadcasted_iota(jnp.int32, (tq, 1), 0)
    nt = (((1,), (1,)), ((), ()))
    if mode == "slc":
        sel = sel_ref[0, 0].astype(BF16)
        c_lo = 0
    else:
        c_lo = jnp.maximum(i * tq - (WINDOW - 1), 0) // tk
    c_hi = ((i + 1) * tq + tk - 1) // tk

    def body(c, carry):
        k0 = pl.multiple_of(c * tk, tk)
        kc = k_ref[0, 0, pl.ds(k0, tk), :]
        vc = v_ref[0, 0, pl.ds(k0, tk), :]
        kpos = k0 + lax.broadcasted_iota(jnp.int32, (1, tk), 1)
        mask = kpos <= t
        if mode == "slc":
            hit = lax.dot_general(sel, et_ref[pl.ds(k0, tk), :], nt, preferred_element_type=F32)
            mask = mask & (hit > 0.5)
        else:
            mask = mask & (t - kpos < WINDOW)
        for r in range(n_rep):
            s = lax.dot_general(qs[r], kc, nt, preferred_element_type=F32)
            s = jnp.where(mask, s, NEG_BIG)
            m_old = m_s[r]
            m_new = jnp.maximum(m_old, jnp.max(s, axis=-1, keepdims=True))
            p = jnp.where(mask, jnp.exp(s - m_new), 0.0)
            alpha = jnp.exp(m_old - m_new)
            l_s[r] = alpha * l_s[r] + jnp.sum(p, axis=-1, keepdims=True)
            acc_s[r] = alpha * acc_s[r] + jnp.dot(p.astype(BF16), vc, preferred_element_type=F32)
            m_s[r] = m_new
        return carry

    lax.fori_loop(c_lo, c_hi, body, 0)
    tiny = float(np.finfo(np.float32).tiny)
    for r in range(n_rep):
        o_ref[0, :, r * dh:(r + 1) * dh] = acc_s[r] / jnp.maximum(l_s[r], tiny)


def _attention(q, k, v, mode, sel=None, tq=128, tk=256):
    B, S, QD = q.shape
    G, Dh = k.shape[1], k.shape[3]
    n_rep = QD // (G * Dh)
    gw = n_rep * Dh
    in_specs = [pl.BlockSpec((1, tq, gw), lambda b, g, i: (b, i, g))]
    args = [q]
    if mode == "slc":
        n_slc = sel.shape[-1]
        e_t = (jnp.arange(S, dtype=jnp.int32)[:, None] // SLC_BLOCK == jnp.arange(n_slc, dtype=jnp.int32)[None, :]).astype(BF16)
        in_specs += [pl.BlockSpec((1, 1, tq, n_slc), lambda b, g, i: (b, g, i, 0)),
                     pl.BlockSpec((S, n_slc), lambda b, g, i: (0, 0))]
        args += [sel, e_t]
    in_specs += [pl.BlockSpec((1, 1, S, Dh), lambda b, g, i: (b, g, 0, 0)),
                 pl.BlockSpec((1, 1, S, Dh), lambda b, g, i: (b, g, 0, 0))]
    args += [k, v]
    return pl.pallas_call(
        functools.partial(_attn_kernel, mode=mode, tq=tq, tk=tk, n_rep=n_rep),
        grid=(B, G, S // tq),
        in_specs=in_specs,
        out_specs=pl.BlockSpec((1, tq, gw), lambda b, g, i: (b, i, g)),
        out_shape=jax.ShapeDtypeStruct((B, S, QD), F32),
        scratch_shapes=[pltpu.VMEM((n_rep, tq, 1), F32), pltpu.VMEM((n_rep, tq, 1), F32),
                        pltpu.VMEM((n_rep, tq, Dh), F32)],
        compiler_params=_cparams(("parallel", "parallel", "arbitrary")),
        name="nsa_attn_" + mode,
    )(*args)


def _layer_norm(x, g, b):
    mu = jnp.mean(x, axis=-1, keepdims=True)
    var = jnp.mean(jnp.square(x - mu), axis=-1, keepdims=True)
    return (x - mu) * lax.rsqrt(var + LN_EPS) * g + b


def _modulate(x, shift, scale):
    return x * (1.0 + scale[:, None, :]) + shift[:, None, :]


def _post_norm(x, y, gate, g, b):
    return _layer_norm(ALPHA * x + (1.0 + gate[:, None, :]) * y, g, b)


def _rope(x, pos):
    half = x.shape[-1] // 2
    inv = ROPE_THETA ** (-jnp.arange(half, dtype=F32) / half)
    ang = pos.astype(F32)[..., None] * inv
    cos = jnp.cos(ang)[:, :, None, :]
    sin = jnp.sin(ang)[:, :, None, :]
    x1 = x[..., :half]
    x2 = x[..., half:]
    return jnp.concatenate([x1 * cos - x2 * sin, x2 * cos + x1 * sin], axis=-1)


def _masked_softmax(s, mask):
    s = jnp.where(mask, s, -jnp.inf)
    m = jnp.max(s, axis=-1, keepdims=True)
    m = jnp.where(jnp.isfinite(m), m, 0.0)
    p = jnp.exp(s - m)
    return p / jnp.maximum(jnp.sum(p, axis=-1, keepdims=True), jnp.finfo(F32).tiny)


def _ssd_chunked(xs, dt, a_neg, bm, cm):
    bsz, seq = xs.shape[:2]
    G, R, P = xs.shape[2:]
    N = bm.shape[-1]
    nc, L = seq // SSM_CHUNK, SSM_CHUNK
    xdt = (xs * dt[..., None]).reshape(bsz, nc, L, G, R, P)
    a_cs = jnp.cumsum((dt * a_neg).reshape(bsz, nc, L, G, R), axis=2)
    bc = bm.reshape(bsz, nc, L, G, N)
    cc = cm.reshape(bsz, nc, L, G, N)
    seg = a_cs[:, :, :, None] - a_cs[:, :, None, :]
    tri = jnp.tril(jnp.ones((L, L), dtype=bool))
    decay = jnp.exp(jnp.where(tri[:, :, None, None], seg, -jnp.inf))
    cb = jnp.einsum("bclgn,bcsgn->bclsg", cc, bc)
    y_diag = jnp.einsum("bclsg,bclsgr,bcsgrp->bclgrp", cb, decay, xdt)
    decay_end = jnp.exp(a_cs[:, :, -1:] - a_cs)
    states = jnp.einsum("bclgn,bclgr,bclgrp->bcgrpn", bc, decay_end, xdt)
    chunk_decay = jnp.exp(a_cs[:, :, -1])

    def step(carry, inp):
        st, dec = inp
        return carry * dec[..., None, None] + st, carry

    init = jnp.zeros((bsz, G, R, P, N), F32)
    _, prev = lax.scan(step, init, (jnp.swapaxes(states, 0, 1), jnp.swapaxes(chunk_decay, 0, 1)))
    prev = jnp.swapaxes(prev, 0, 1)
    y_off = jnp.einsum("bclgn,bcgrpn,bclgr->bclgrp", cc, prev, jnp.exp(a_cs))
    return (y_diag + y_off).reshape(bsz, seq, G, R, P)


def _mamba2_mixer(h, in_w, conv_w, conv_b, dt_bias, a_log, d_skip, norm_w, out_w):
    bsz, seq, _ = h.shape
    d_inner = out_w.shape[0]
    heads = dt_bias.shape[0]
    G, P, N = SSM_GROUPS, SSM_HEAD_DIM, SSM_STATE
    R = heads // G
    bc_dim = G * N
    conv_dim = d_inner + 2 * bc_dim
    zxbcdt = h @ in_w
    z = zxbcdt[..., :d_inner]
    xbc = zxbcdt[..., d_inner:d_inner + conv_dim]
    dt_raw = zxbcdt[..., d_inner + conv_dim:]
    xbc = lax.conv_general_dilated(xbc, conv_w[:, None, :], window_strides=(1,),
                                   padding=[(CONV_WIDTH - 1, 0)],
                                   dimension_numbers=("NWC", "WIO", "NWC"),
                                   feature_group_count=conv_dim)
    xbc = jax.nn.silu(xbc + conv_b)
    xs = xbc[..., :d_inner].reshape(bsz, seq, G, R, P)
    bm = xbc[..., d_inner:d_inner + bc_dim].reshape(bsz, seq, G, N)
    cm = xbc[..., d_inner + bc_dim:].reshape(bsz, seq, G, N)
    dt = jax.nn.softplus(dt_raw + dt_bias).reshape(bsz, seq, G, R)
    a_neg = -jnp.exp(a_log).reshape(G, R)
    y = _ssd_chunked(xs, dt, a_neg, bm, cm) + xs * d_skip.reshape(G, R)[:, :, None]
    y = y.reshape(bsz, seq, d_inner) * jax.nn.silu(z)
    yg = y.reshape(bsz, seq, G, d_inner // G)
    yg = yg * lax.rsqrt(jnp.mean(jnp.square(yg), axis=-1, keepdims=True) + LN_EPS)
    y = yg.reshape(bsz, seq, d_inner) * norm_w
    return y @ out_w


def _compress_blocks(k, cmp_pos, w1, w2):
    bsz, seq = k.shape[:2]
    sub = k.reshape(bsz, seq // CMP_STRIDE, CMP_STRIDE, NSA_KV_HEADS, NSA_HEAD_DIM)
    blocks = jnp.concatenate([sub[:, :-1], sub[:, 1:]], axis=2) + cmp_pos[:, None, :]
    n_cmp = blocks.shape[1]
    flat = blocks.transpose(0, 1, 3, 2, 4).reshape(bsz, n_cmp, NSA_KV_HEADS, CMP_BLOCK * NSA_HEAD_DIM)
    return jax.nn.gelu(flat @ w1) @ w2


def _nsa_shared_kv(h, c_act, pos, kv_ada_w, kv_ada_b, kv_w, cmp_pos, phi_k_w1, phi_k_w2, phi_v_w1, phi_v_w2):
    bsz, seq, _ = h.shape
    shift, scale = jnp.split(c_act @ kv_ada_w + kv_ada_b, 2, axis=-1)
    kv = (_modulate(h, shift, scale) @ kv_w).reshape(bsz, seq, 2 * NSA_N_BRANCH, NSA_KV_HEADS, NSA_HEAD_DIM)
    k_cmp = _compress_blocks(_rope(kv[:, :, 0], pos), cmp_pos, phi_k_w1, phi_k_w2)
    v_cmp = _compress_blocks(kv[:, :, 1], cmp_pos, phi_v_w1, phi_v_w2)
    return (k_cmp, v_cmp, _rope(kv[:, :, 2], pos), kv[:, :, 3], _rope(kv[:, :, 4], pos), kv[:, :, 5])


def _cmp_slc_overlap(n_cmp, n_slc):
    cs = jnp.arange(n_cmp)[:, None] * CMP_STRIDE
    js = jnp.arange(n_slc)[None, :] * SLC_BLOCK
    ov = jnp.minimum(cs + CMP_BLOCK, js + SLC_BLOCK) - jnp.maximum(cs, js)
    return jnp.maximum(ov, 0).astype(F32) / CMP_BLOCK


def _compressed_branch(q, k_cmp, v_cmp):
    seq = q.shape[1]
    n_cmp = k_cmp.shape[1]
    s = jnp.einsum("bsgrd,bngd->bgrsn", q, k_cmp) * ATTN_SCALE
    t = jnp.arange(seq)
    blk_end = jnp.arange(n_cmp) * CMP_STRIDE + CMP_BLOCK - 1
    p = _masked_softmax(s, blk_end[None, :] <= t[:, None])
    o = jnp.einsum("bgrsn,bngd->bsgrd", p, v_cmp)
    importance = jnp.einsum("bgrsn,nj->bgsj", p, _cmp_slc_overlap(n_cmp, seq // SLC_BLOCK))
    return o, importance


def _select_blocks(importance):
    seq, n_slc = importance.shape[-2:]
    n_sel = min(N_SELECT, n_slc)
    t_blk = jnp.arange(seq)[:, None] // SLC_BLOCK
    j = jnp.arange(n_slc)[None, :]
    forced = (j == 0) | (j == t_blk) | (j == t_blk - 1)
    score = jnp.where(forced, FORCED_SCORE, importance)
    score = jnp.where(j <= t_blk, score, -jnp.inf)
    top_v, idx = lax.top_k(score, n_sel)
    return idx, jnp.isfinite(top_v)


def _selected_branch(q, k, v, blk_idx, blk_valid):
    bsz, seq = q.shape[:2]
    G, R, Dh = q.shape[2:]
    n_slc = seq // SLC_BLOCK
    n_sel = blk_idx.shape[-1]
    QB = SLC_QUERY_BLOCK
    nq = seq // QB
    kb = k.reshape(bsz, n_slc, SLC_BLOCK, G, Dh).transpose(0, 3, 1, 2, 4)
    vb = v.reshape(bsz, n_slc, SLC_BLOCK, G, Dh).transpose(0, 3, 1, 2, 4)
    q_blocks = jnp.swapaxes(q.reshape(bsz, nq, QB, G, R, Dh), 0, 1)
    idx_blocks = blk_idx.reshape(bsz, G, nq, QB, n_sel).transpose(2, 0, 1, 3, 4)
    val_blocks = blk_valid.reshape(bsz, G, nq, QB, n_sel).transpose(2, 0, 1, 3, 4)
    t_blocks = jnp.arange(seq, dtype=jnp.int32).reshape(nq, QB)
    b_ix = jnp.arange(bsz)[:, None, None, None]
    g_ix = jnp.arange(G)[None, :, None, None]
    offs = jnp.arange(SLC_BLOCK, dtype=jnp.int32)

    def one_block(args):
        qb, ib, valb, tb = args
        kg = kb[b_ix, g_ix, ib]
        vg = vb[b_ix, g_ix, ib]
        s = jnp.einsum("bqgrd,bgqnkd->bgrqnk", qb, kg) * ATTN_SCALE
        kpos = ib[..., None] * SLC_BLOCK + offs
        mask = valb[..., None] & (kpos <= tb[None, None, :, None, None])
        p = _masked_softmax(s.reshape(bsz, G, R, QB, n_sel * SLC_BLOCK),
                            mask.reshape(bsz, G, 1, QB, n_sel * SLC_BLOCK))
        p = p.reshape(bsz, G, R, QB, n_sel, SLC_BLOCK)
        return jnp.einsum("bgrqnk,bgqnkd->bqgrd", p, vg)

    out = lax.map(one_block, (q_blocks, idx_blocks, val_blocks, t_blocks))
    return jnp.swapaxes(out, 0, 1).reshape(bsz, seq, G, R, Dh)


def _window_branch(q, k, v):
    bsz, seq = q.shape[:2]
    QB = WIN_QUERY_BLOCK
    nb = seq // QB
    span = WINDOW + QB
    k_pad = jnp.pad(k, ((0, 0), (WINDOW, 0), (0, 0), (0, 0)))
    v_pad = jnp.pad(v, ((0, 0), (WINDOW, 0), (0, 0), (0, 0)))
    q_off = jnp.arange(QB, dtype=jnp.int32)
    k_off = jnp.arange(span, dtype=jnp.int32)

    def one_block(i):
        start = i * QB
        qb = lax.dynamic_slice_in_dim(q, start, QB, axis=1)
        kw = lax.dynamic_slice_in_dim(k_pad, start, span, axis=1)
        vw = lax.dynamic_slice_in_dim(v_pad, start, span, axis=1)
        s = jnp.einsum("bqgrd,bkgd->bgrqk", qb, kw) * ATTN_SCALE
        t = start + q_off
        kpos = start - WINDOW + k_off
        diff = t[:, None] - kpos[None, :]
        mask = (diff >= 0) & (diff < WINDOW) & (kpos >= 0)[None, :]
        p = _masked_softmax(s, mask)
        return jnp.einsum("bgrqk,bkgd->bqgrd", p, vw)

    out = lax.map(one_block, jnp.arange(nb, dtype=jnp.int32))
    return jnp.swapaxes(out, 0, 1).reshape(bsz, seq, *q.shape[2:])


def _nsa_mixer(h, pos, q_w, o_w, k_cmp, v_cmp, k_slc, v_slc, k_win, v_win):
    bsz, seq, _ = h.shape
    G, Dh = NSA_KV_HEADS, NSA_HEAD_DIM
    q_dim = o_w.shape[0]
    heads = q_dim // Dh
    R = heads // G
    qg = h @ q_w
    q = _rope(qg[..., :q_dim].reshape(bsz, seq, heads, Dh), pos).reshape(bsz, seq, G, R, Dh)
    gates = jax.nn.sigmoid(qg[..., q_dim:]).reshape(bsz, seq, G, R, NSA_N_BRANCH)
    o_cmp, importance = _compressed_branch(q, k_cmp, v_cmp)
    blk_idx, blk_valid = _select_blocks(importance)
    n_slc = seq // SLC_BLOCK
    sel = jnp.sum(jax.nn.one_hot(blk_idx, n_slc, dtype=BF16) * blk_valid[..., None].astype(BF16), axis=-2)
    q_flat = q.reshape(bsz, seq, q_dim)
    to_gs = lambda a: jnp.swapaxes(a, 1, 2).astype(BF16)
    o_slc = _attention(q_flat, to_gs(k_slc), to_gs(v_slc), "slc", sel=sel).reshape(bsz, seq, G, R, Dh)
    o_win = _attention(q_flat, to_gs(k_win), to_gs(v_win), "win").reshape(bsz, seq, G, R, Dh)
    o = gates[..., 0:1] * o_cmp + gates[..., 1:2] * o_slc + gates[..., 2:3] * o_win
    return o.reshape(bsz, seq, q_dim) @ o_w


def kernel(x, c, pos, ada_w, ada_b, ln_g, ln_b, ssm_in_w, ssm_conv_w, ssm_conv_b, ssm_dt_bias, ssm_a_log, ssm_d, ssm_norm_w, ssm_out_w, kv_ada_w, kv_ada_b, kv_w, cmp_pos, phi_k_w1, phi_k_w2, phi_v_w1, phi_v_w2, nsa_q_w, nsa_o_w, router_w, router_b, moe_w_up, moe_b_up, moe_w_down, moe_b_down):
    c_act = jax.nn.silu(c)
    shared = None
    for i in range(DEPTH):
        mod = c_act @ ada_w[i] + ada_b[i]
        sh_t, sc_t, g_t, sh_c, sc_c, g_c = jnp.split(mod, 6, axis=-1)
        h = _modulate(x, sh_t, sc_t)
        if i < N_A:
            y = _mamba2_mixer(h, ssm_in_w[i], ssm_conv_w[i], ssm_conv_b[i], ssm_dt_bias[i],
                              ssm_a_log[i], ssm_d[i], ssm_norm_w[i], ssm_out_w[i])
        else:
            j = i - N_A
            y = _nsa_mixer(h, pos, nsa_q_w[j], nsa_o_w[j], *shared)
        x = _post_norm(x, y, g_t, ln_g[i, 0], ln_b[i, 0])
        x = _moe_layer(x, sh_c[:, None, :], sc_c[:, None, :], g_c[:, None, :], router_w[i], router_b[i],
                       moe_w_up[i], moe_b_up[i], moe_w_down[i], moe_b_down[i], ln_g[i, 1], ln_b[i, 1])
        if i == N_A - 1:
            shared = _nsa_shared_kv(x, c_act, pos, kv_ada_w, kv_ada_b, kv_w, cmp_pos,
                                    phi_k_w1, phi_k_w2, phi_v_w1, phi_v_w2)
    return x
```

```python
import functools

import jax
import jax.numpy as jnp
import numpy as np
from jax import lax
from jax.experimental import pallas as pl
from jax.experimental.pallas import tpu as pltpu

F32 = jnp.float32
BF16 = jnp.bfloat16

DEPTH = 4
N_A = DEPTH // 2
ALPHA = (2.0 * DEPTH) ** 0.25
LN_EPS = 1e-5

SSM_HEAD_DIM = 64
SSM_GROUPS = 4
SSM_STATE = 128
CONV_WIDTH = 4
SSM_CHUNK = 128

NSA_HEAD_DIM = 64
NSA_KV_HEADS = 4
NSA_N_BRANCH = 3
CMP_BLOCK = 32
CMP_STRIDE = 16
SLC_BLOCK = 64
N_SELECT = 16
WINDOW = 512
ROPE_THETA = 10000.0
ATTN_SCALE = NSA_HEAD_DIM ** -0.5
FORCED_SCORE = 1e9

TOP_K = 4
SWIGLU_LIMIT = 7.0
SWIGLU_ALPHA = 1.702

VMEM_LIMIT_BYTES_V7X = 56 * 1024 * 1024
LANES = 128
SUBLANES = 8
NEG_BIG = -1e30
F32_TINY = float(np.finfo(np.float32).tiny)
NT_DIMS = (((1,), (1,)), ((), ()))


def _cparams(sem):
    return pltpu.CompilerParams(dimension_semantics=sem, vmem_limit_bytes=VMEM_LIMIT_BYTES_V7X)


def _dot(a, b):
    return jnp.dot(a, b, preferred_element_type=F32)


def _split_bf16(a, parts):
    out = []
    r = a
    for _ in range(parts):
        p = r.astype(BF16)
        out.append(p)
        r = r - p.astype(F32)
    return out


def _layer_norm_rows(v, g, b):
    mu = jnp.mean(v, axis=-1, keepdims=True)
    d = v - mu
    var = jnp.mean(d * d, axis=-1, keepdims=True)
    return d * lax.rsqrt(var + LN_EPS) * g + b


def _modulated_bf16(x_ref, sc_ref, sh_ref):
    return (x_ref[0] * (1.0 + sc_ref[0]) + sh_ref[0]).astype(BF16)


def _row_spec(tm, width):
    return pl.BlockSpec((1, tm, width), lambda b, s: (b, s, 0))


def _batch_vec_spec(width):
    return pl.BlockSpec((1, 1, width), lambda b, s: (b, 0, 0))


def _const_spec(shape):
    nd = len(shape)
    return pl.BlockSpec(shape, lambda b, s: (0,) * nd)


def _router_kernel(x_ref, sc_ref, sh_ref, rw_ref, rb_ref, h_ref, idx_ref, gate_ref):
    hb = _modulated_bf16(x_ref, sc_ref, sh_ref)
    h_ref[0] = hb
    logits = _dot(hb, rw_ref[...]) + rb_ref[...]
    n_exp = logits.shape[-1]
    lane_e = lax.broadcasted_iota(jnp.int32, logits.shape, 1)
    lane_o = lax.broadcasted_iota(jnp.int32, idx_ref.shape[1:], 1)
    vals, idxs = [], []
    l = logits
    for _ in range(TOP_K):
        m = jnp.max(l, axis=-1, keepdims=True)
        idx = jnp.min(jnp.where(l == m, lane_e, n_exp), axis=-1, keepdims=True)
        vals.append(m)
        idxs.append(idx)
        l = jnp.where(lane_e == idx, -jnp.inf, l)
    es = [jnp.exp(v - vals[0]) for v in vals]
    denom = es[0] + es[1] + es[2] + es[3]
    idx_out = jnp.zeros(idx_ref.shape[1:], jnp.int32)
    gate_out = jnp.zeros(gate_ref.shape[1:], F32)
    for k in range(TOP_K):
        idx_out = jnp.where(lane_o == k, idxs[k], idx_out)
        gate_out = jnp.where(lane_o == k, es[k] / denom, gate_out)
    idx_ref[0] = idx_out
    gate_ref[0] = gate_out


def _router(x, sc, sh, rw, rb, tm=512):
    B, S, D = x.shape
    E = rw.shape[-1]
    return pl.pallas_call(
        _router_kernel,
        grid=(B, S // tm),
        in_specs=[_row_spec(tm, D), _batch_vec_spec(D), _batch_vec_spec(D), _const_spec((D, E)), _const_spec((1, E))],
        out_specs=[_row_spec(tm, D), _row_spec(tm, LANES), _row_spec(tm, LANES)],
        out_shape=[
            jax.ShapeDtypeStruct((B, S, D), BF16),
            jax.ShapeDtypeStruct((B, S, LANES), jnp.int32),
            jax.ShapeDtypeStruct((B, S, LANES), F32),
        ],
        compiler_params=_cparams(("parallel", "parallel")),
        name="moe_router",
    )(x, sc, sh, rw.astype(BF16), rb.reshape(1, E))


def _ffn_kernel(blk_e_ref, first_ref, nused_ref, x_ref, wu_ref, bu_ref, wd_ref, bd_ref, o_ref, wu_s, wd_s):
    i = pl.program_id(0)
    d_exp = wd_s.shape[0]

    @pl.when(i < nused_ref[0])
    def _():
        @pl.when(first_ref[i] == 1)
        def _():
            wu_s[...] = wu_ref[0].astype(BF16)
            wd_s[...] = wd_ref[0].astype(BF16)

        u = _dot(x_ref[...], wu_s[...]) + bu_ref[0]
        glu = jnp.minimum(u[:, :d_exp], SWIGLU_LIMIT)
        lin = jnp.clip(u[:, d_exp:], -SWIGLU_LIMIT, SWIGLU_LIMIT)
        act = glu * jax.nn.sigmoid(SWIGLU_ALPHA * glu) * (lin + 1.0)
        o_ref[...] = _dot(act.astype(BF16), wd_s[...]) + bd_ref[0]

    @pl.when(i >= nused_ref[0])
    def _():
        o_ref[...] = jnp.zeros_like(o_ref)


def _expert_ffn(xg, blk_e, first, nused, w_up, b_up, w_down, b_down, tm):
    R, D = xg.shape
    E, _, N2 = w_up.shape
    d_exp = w_down.shape[1]
    n_blk = R // tm
    grid_spec = pltpu.PrefetchScalarGridSpec(
        num_scalar_prefetch=3,
        grid=(n_blk,),
        in_specs=[
            pl.BlockSpec((tm, D), lambda i, be, fi, nu: (i, 0)),
            pl.BlockSpec((1, D, N2), lambda i, be, fi, nu: (be[i], 0, 0)),
            pl.BlockSpec((1, 1, N2), lambda i, be, fi, nu: (be[i], 0, 0)),
            pl.BlockSpec((1, d_exp, D), lambda i, be, fi, nu: (be[i], 0, 0)),
            pl.BlockSpec((1, 1, D), lambda i, be, fi, nu: (be[i], 0, 0)),
        ],
        out_specs=pl.BlockSpec((tm, D), lambda i, be, fi, nu: (i, 0)),
        scratch_shapes=[pltpu.VMEM((D, N2), BF16), pltpu.VMEM((d_exp, D), BF16)],
    )
    return pl.pallas_call(
        _ffn_kernel,
        grid_spec=grid_spec,
        out_shape=jax.ShapeDtypeStruct((R, D), F32),
        compiler_params=_cparams(("arbitrary",)),
        name="moe_expert_ffn",
    )(blk_e, first, nused, xg, w_up, b_up.reshape(E, 1, N2), w_down, b_down.reshape(E, 1, D))


def _combine_norm_kernel(x_ref, yg_ref, gate_ref, gc_ref, g_ref, b_ref, o_ref):
    gate = gate_ref[0]
    y = yg_ref[0, 0] * gate[:, 0:1]
    for k in range(1, TOP_K):
        y = y + yg_ref[k, 0] * gate[:, k:k + 1]
    v = ALPHA * x_ref[0] + (1.0 + gc_ref[0]) * y
    o_ref[0] = _layer_norm_rows(v, g_ref[...], b_ref[...])


def _combine_norm(x, yg, gate, gate_c, ln_g, ln_b, tm=256):
    B, S, D = x.shape
    return pl.pallas_call(
        _combine_norm_kernel,
        grid=(B, S // tm),
        in_specs=[
            _row_spec(tm, D),
            pl.BlockSpec((TOP_K, 1, tm, D), lambda b, s: (0, b, s, 0)),
            _row_spec(tm, LANES),
            _batch_vec_spec(D),
            _const_spec((1, D)),
            _const_spec((1, D)),
        ],
        out_specs=_row_spec(tm, D),
        out_shape=jax.ShapeDtypeStruct((B, S, D), F32),
        compiler_params=_cparams(("parallel", "parallel")),
        name="moe_combine_norm",
    )(x, yg, gate, gate_c, ln_g.reshape(1, D), ln_b.reshape(1, D))


def _moe_layer(x, sh, sc, gate_c, router_w, router_b, w_up, b_up, w_down, b_down, ln_g, ln_b, row_block=256):
    B, S, D = x.shape
    T = B * S
    E = router_w.shape[-1]
    h, top_i, gate = _router(x, sc, sh, router_w, router_b)
    A = T * TOP_K
    eid = top_i[:, :, :TOP_K].reshape(A)
    order = jnp.argsort(eid).astype(jnp.int32)
    se = eid[order]
    counts = jnp.bincount(eid, length=E).astype(jnp.int32)
    padded = (counts + row_block - 1) // row_block * row_block
    off = jnp.cumsum(counts) - counts
    cum_padded = jnp.cumsum(padded)
    poff = cum_padded - padded
    dest = poff[se] + jnp.arange(A, dtype=jnp.int32) - off[se]
    R_rows = A + E * row_block
    n_blk = R_rows // row_block
    row_tok = jnp.full((R_rows,), T, jnp.int32).at[dest].set(order // TOP_K)
    pos = jnp.zeros((A,), jnp.int32).at[order].set(dest)
    nused = (cum_padded[-1] // row_block).astype(jnp.int32)
    blk_start = jnp.arange(n_blk, dtype=jnp.int32) * row_block
    blk_e = jnp.minimum(jnp.searchsorted(cum_padded, blk_start, side="right"), E - 1).astype(jnp.int32)
    last_e = blk_e[jnp.maximum(nused - 1, 0)]
    blk_e = jnp.where(jnp.arange(n_blk) < nused, blk_e, last_e)
    first = jnp.concatenate([jnp.ones((1,), jnp.int32), (blk_e[1:] != blk_e[:-1]).astype(jnp.int32)])
    h_pad = jnp.concatenate([h.reshape(T, D), jnp.zeros((1, D), h.dtype)], axis=0)
    xg = h_pad[row_tok]
    y = _expert_ffn(xg, blk_e, first, nused.reshape(1), w_up, b_up, w_down, b_down, row_block)
    yg = y[pos.reshape(T, TOP_K).T].reshape(TOP_K, B, S, D)
    return _combine_norm(x, yg, gate, gate_c, ln_g, ln_b)


def _outproj_norm_kernel(y_ref, w_ref, x_ref, gt_ref, g_ref, b_ref, o_ref):
    y = _dot(y_ref[0], w_ref[...])
    v = ALPHA * x_ref[0] + (1.0 + gt_ref[0]) * y
    o_ref[0] = _layer_norm_rows(v, g_ref[...], b_ref[...])


def _outproj_norm(y, w, x, gate_t, ln_g, ln_b, tm=512):
    B, S, D = x.shape
    K = y.shape[-1]
    return pl.pallas_call(
        _outproj_norm_kernel,
        grid=(B, S // tm),
        in_specs=[_row_spec(tm, K), _const_spec((K, D)), _row_spec(tm, D), _batch_vec_spec(D),
                  _const_spec((1, D)), _const_spec((1, D))],
        out_specs=_row_spec(tm, D),
        out_shape=jax.ShapeDtypeStruct((B, S, D), F32),
        compiler_params=_cparams(("parallel", "parallel")),
        name="outproj_norm",
    )(y, w.astype(BF16), x, gate_t, ln_g.reshape(1, D), ln_b.reshape(1, D))


def _ssm_inproj_kernel(x_ref, sc_ref, sh_ref, wz_ref, wx_ref, wdt_ref, cw_ref, cb_ref, dtb_ref,
                       z_ref, xs_ref, bm_ref, cm_ref, dt_ref, xbuf, *, col_chunk):
    tm = x_ref.shape[1]
    d_inner = xs_ref.shape[-1]
    bc = bm_ref.shape[-1]
    conv_dim = wx_ref.shape[-1]
    halo = SUBLANES

    @pl.when(pl.program_id(1) == 0)
    def _():
        xbuf[0:halo, :] = jnp.zeros((halo, conv_dim), F32)

    hb = _modulated_bf16(x_ref, sc_ref, sh_ref)
    z_ref[0] = _dot(hb, wz_ref[...])
    dt_ref[0] = jax.nn.softplus(_dot(hb, wdt_ref[...]) + dtb_ref[...])
    for c0 in range(0, conv_dim, col_chunk):
        cs = slice(c0, c0 + col_chunk)
        xbuf[halo:halo + tm, cs] = _dot(hb, wx_ref[:, cs])
        acc = cb_ref[:, cs] + cw_ref[CONV_WIDTH - 1:CONV_WIDTH, cs] * xbuf[halo:halo + tm, cs]
        for w in range(CONV_WIDTH - 1):
            lo = halo - (CONV_WIDTH - 1) + w
            acc = acc + cw_ref[w:w + 1, cs] * xbuf[lo:lo + tm, cs]
        y = acc * jax.nn.sigmoid(acc)
        if c0 + col_chunk <= d_inner:
            xs_ref[0, :, cs] = y
        elif c0 + col_chunk <= d_inner + bc:
            bm_ref[0, :, c0 - d_inner:c0 - d_inner + col_chunk] = y.astype(BF16)
        else:
            cm_ref[0, :, c0 - d_inner - bc:c0 - d_inner - bc + col_chunk] = y.astype(BF16)
        xbuf[0:halo, cs] = xbuf[tm:tm + halo, cs]


def _ssm_inproj(x, sc, sh, in_w, conv_w, conv_b, dt_bias, d_inner, bc_dim, tm=256, col_chunk=512):
    B, S, D = x.shape
    conv_dim = d_inner + 2 * bc_dim
    heads = dt_bias.shape[0]
    wz = in_w[:, :d_inner].astype(BF16)
    wx = in_w[:, d_inner:d_inner + conv_dim].astype(BF16)
    wdt = jnp.pad(in_w[:, d_inner + conv_dim:], ((0, 0), (0, LANES - heads))).astype(BF16)
    dtb = jnp.pad(dt_bias, (0, LANES - heads)).reshape(1, LANES)
    assert d_inner % col_chunk == 0 and bc_dim % col_chunk == 0
    return pl.pallas_call(
        functools.partial(_ssm_inproj_kernel, col_chunk=col_chunk),
        grid=(B, S // tm),
        in_specs=[_row_spec(tm, D), _batch_vec_spec(D), _batch_vec_spec(D),
                  _const_spec((D, d_inner)), _const_spec((D, conv_dim)), _const_spec((D, LANES)),
                  _const_spec((CONV_WIDTH, conv_dim)), _const_spec((1, conv_dim)), _const_spec((1, LANES))],
        out_specs=[_row_spec(tm, d_inner), _row_spec(tm, d_inner), _row_spec(tm, bc_dim), _row_spec(tm, bc_dim),
                   _row_spec(tm, LANES)],
        out_shape=[
            jax.ShapeDtypeStruct((B, S, d_inner), F32),
            jax.ShapeDtypeStruct((B, S, d_inner), F32),
            jax.ShapeDtypeStruct((B, S, bc_dim), BF16),
            jax.ShapeDtypeStruct((B, S, bc_dim), BF16),
            jax.ShapeDtypeStruct((B, S, LANES), F32),
        ],
        scratch_shapes=[pltpu.VMEM((SUBLANES + tm + SUBLANES, conv_dim), F32)],
        compiler_params=_cparams(("parallel", "arbitrary")),
        name="ssm_inproj_conv",
    )(x, sc, sh, wz, wx, wdt, conv_w, conv_b.reshape(1, conv_dim), dtb)


def _ssd_kernel(xs_ref, bm_ref, cm_ref, z_ref, dt_ref, dtt_ref, aneg_ref, anegt_ref, dskip_ref, nw_ref,
                y_ref, state, *, heads_per_group):
    L = xs_ref.shape[1]
    P = SSM_HEAD_DIM
    N = SSM_STATE
    n_groups = bm_ref.shape[-1] // N
    gw = heads_per_group * P

    @pl.when(pl.program_id(1) == 0)
    def _():
        state[...] = jnp.zeros(state.shape, F32)

    row = lax.broadcasted_iota(jnp.int32, (L, L), 0)
    col = lax.broadcasted_iota(jnp.int32, (L, L), 1)
    lower = row >= col
    tri = jnp.where(lower, 1.0, 0.0).astype(BF16)
    tri_t = jnp.where(row <= col, 1.0, 0.0).astype(BF16)
    dt = dt_ref[0]
    a_cs = sum(_dot(tri, p) for p in _split_bf16(dt * aneg_ref[...], 3))
    a_cs_t = sum(_dot(p, tri_t) for p in _split_bf16(dtt_ref[0] * anegt_ref[...], 3))
    for g in range(n_groups):
        bg = bm_ref[0, :, g * N:(g + 1) * N]
        cg = cm_ref[0, :, g * N:(g + 1) * N]
        cb = lax.dot_general(cg, bg, NT_DIMS, preferred_element_type=F32)
        bg_t = bg.astype(F32).T
        ssq = jnp.zeros((L, 1), F32)
        ys = []
        for r in range(heads_per_group):
            h = g * heads_per_group + r
            hs = slice(h * P, (h + 1) * P)
            a_col = a_cs[:, h:h + 1]
            a_row = a_cs_t[h:h + 1, :]
            a_end = a_cs_t[h:h + 1, L - 1:L]
            x_h = xs_ref[0, :, hs]
            xdt = (x_h * dt[:, h:h + 1]).astype(BF16)
            decay = jnp.exp(jnp.where(lower, a_col - a_row, NEG_BIG))
            y = _dot((cb * decay).astype(BF16), xdt)
            s_h = state[h]
            y = y + _dot(cg, s_h.astype(BF16)) * jnp.exp(a_col)
            y = y + x_h * dskip_ref[:, hs]
            state[h] = s_h * jnp.exp(a_end) + _dot((bg_t * jnp.exp(a_end - a_row)).astype(BF16), xdt)
            zz = z_ref[0, :, hs]
            y = y * (zz * jax.nn.sigmoid(zz))
            ssq = ssq + jnp.sum(y * y, axis=-1, keepdims=True)
            ys.append(y)
        scale = lax.rsqrt(ssq / gw + LN_EPS)
        for r in range(heads_per_group):
            h = g * heads_per_group + r
            hs = slice(h * P, (h + 1) * P)
            y_ref[0, :, hs] = (ys[r] * scale * nw_ref[:, hs]).astype(BF16)


def _ssd(xs, bm, cm, z, dt, a_log, d_skip, norm_w):
    B, S, d_inner = xs.shape
    heads = a_log.shape[0]
    bc_dim = bm.shape[-1]
    L = SSM_CHUNK
    a_neg = -jnp.exp(a_log)
    dt_t = jnp.swapaxes(dt[:, :, :heads], 1, 2)
    aneg = jnp.pad(a_neg, (0, LANES - heads)).reshape(1, LANES)
    anegt = jnp.broadcast_to(a_neg[:, None], (heads, L))
    dskip = jnp.repeat(d_skip, SSM_HEAD_DIM).reshape(1, d_inner)
    return pl.pallas_call(
        functools.partial(_ssd_kernel, heads_per_group=heads // SSM_GROUPS),
        grid=(B, S // L),
        in_specs=[_row_spec(L, d_inner), _row_spec(L, bc_dim), _row_spec(L, bc_dim), _row_spec(L, d_inner),
                  _row_spec(L, LANES), pl.BlockSpec((1, heads, L), lambda b, c: (b, 0, c)),
                  _const_spec((1, LANES)), _const_spec((heads, L)), _const_spec((1, d_inner)), _const_spec((1, d_inner))],
        out_specs=_row_spec(L, d_inner),
        out_shape=jax.ShapeDtypeStruct((B, S, d_inner), BF16),
        scratch_shapes=[pltpu.VMEM((heads, SSM_STATE, SSM_HEAD_DIM), F32)],
        compiler_params=_cparams(("parallel", "arbitrary")),
        name="ssd_chunk_scan",
    )(xs, bm, cm, z, dt, dt_t, aneg, anegt, dskip, norm_w.reshape(1, d_inner))


def _mamba_layer(x, sh, sc, gate_t, in_w, conv_w, conv_b, dt_bias, a_log, d_skip, norm_w, out_w, ln_g, ln_b):
    d_inner = out_w.shape[0]
    bc_dim = SSM_GROUPS * SSM_STATE
    z, xs, bm, cm, dt = _ssm_inproj(x, sc, sh, in_w, conv_w, conv_b, dt_bias, d_inner, bc_dim)
    y = _ssd(xs, bm, cm, z, dt, a_log, d_skip, norm_w)
    return _outproj_norm(y, out_w, x, gate_t, ln_g, ln_b)


def _rope_tables(pos):
    half = NSA_HEAD_DIM // 2
    inv = ROPE_THETA ** (-jnp.arange(half, dtype=F32) / half)
    ang = pos.astype(F32)[..., None] * inv
    cos, sin = jnp.cos(ang), jnp.sin(ang)
    return jnp.concatenate([cos, cos, cos, cos], axis=-1), jnp.concatenate([-sin, sin, -sin, sin], axis=-1)


def _rope_slab(x, cos, sin):
    half = NSA_HEAD_DIM // 2
    lane = lax.broadcasted_iota(jnp.int32, (1, LANES), 1)
    first_half = (lane % NSA_HEAD_DIM) < half
    partner = jnp.where(first_half, pltpu.roll(x, LANES - half, 1), pltpu.roll(x, half, 1))
    return x * cos + partner * sin


def _kvproj_kernel(x_ref, sc_ref, sh_ref, w_ref, cos_ref, sin_ref, kc_ref, ka_ref):
    hb = _modulated_bf16(x_ref, sc_ref, sh_ref)
    cos, sin = cos_ref[0], sin_ref[0]
    dh = NSA_HEAD_DIM
    heads_per_slab = LANES // dh
    n_slab = w_ref.shape[-1] // LANES
    slabs_per_branch = NSA_KV_HEADS // heads_per_slab
    for j in range(n_slab):
        slab = _dot(hb, w_ref[:, j * LANES:(j + 1) * LANES])
        branch = j // slabs_per_branch
        if branch % 2 == 0:
            slab = _rope_slab(slab, cos, sin)
        for u in range(heads_per_slab):
            head = (j % slabs_per_branch) * heads_per_slab + u
            piece = slab[:, u * dh:(u + 1) * dh]
            if branch < 2:
                kc_ref[0, branch * NSA_KV_HEADS + head] = piece
            else:
                ka_ref[0, (branch - 2) * NSA_KV_HEADS + head] = piece.astype(BF16)


def _kvproj(x, sc, sh, kv_w, cos, sin, tm=512):
    B, S, D = x.shape
    N = kv_w.shape[-1]
    n_cmp_arr = 2 * NSA_KV_HEADS
    n_att_arr = 4 * NSA_KV_HEADS
    return pl.pallas_call(
        _kvproj_kernel,
        grid=(B, S // tm),
        in_specs=[_row_spec(tm, D), _batch_vec_spec(D), _batch_vec_spec(D), _const_spec((D, N)),
                  _row_spec(tm, LANES), _row_spec(tm, LANES)],
        out_specs=[pl.BlockSpec((1, n_cmp_arr, tm, NSA_HEAD_DIM), lambda b, s: (b, 0, s, 0)),
                   pl.BlockSpec((1, n_att_arr, tm, NSA_HEAD_DIM), lambda b, s: (b, 0, s, 0))],
        out_shape=[jax.ShapeDtypeStruct((B, n_cmp_arr, S, NSA_HEAD_DIM), F32),
                   jax.ShapeDtypeStruct((B, n_att_arr, S, NSA_HEAD_DIM), BF16)],
        compiler_params=_cparams(("parallel", "parallel")),
        name="nsa_kv_proj",
    )(x, sc, sh, kv_w.astype(BF16), cos, sin)


def _compress_kernel(k_ref, pos_ref, w1_ref, w2_ref, o_ref):
    ks = k_ref[0, 0]
    n_sub, half_in = ks.shape
    top = _dot((ks + pos_ref[0:1, :]).astype(BF16), w1_ref[0, :half_in, :])
    bot = _dot((ks + pos_ref[1:2, :]).astype(BF16), w1_ref[0, half_in:, :])
    hid = jax.nn.gelu(top + pltpu.roll(bot, n_sub - 1, 0))
    o_ref[0, 0] = _dot(hid.astype(BF16), w2_ref[0]).astype(o_ref.dtype)


def _compress(kc, cmp_pos, w1s, w2s):
    B, A, S, Dh = kc.shape
    n_sub = S // CMP_STRIDE
    half_in = CMP_STRIDE * Dh
    hid = w1s.shape[-1]
    ksub = kc.reshape(B, A, n_sub, half_in)
    pos2 = cmp_pos.reshape(2, half_in)
    return pl.pallas_call(
        _compress_kernel,
        grid=(B, A),
        in_specs=[pl.BlockSpec((1, 1, n_sub, half_in), lambda b, a: (b, a, 0, 0)),
                  pl.BlockSpec((2, half_in), lambda b, a: (0, 0)),
                  pl.BlockSpec((1, 2 * half_in, hid), lambda b, a: (a // NSA_KV_HEADS, 0, 0)),
                  pl.BlockSpec((1, hid, Dh), lambda b, a: (a // NSA_KV_HEADS, 0, 0))],
        out_specs=pl.BlockSpec((1, 1, n_sub, Dh), lambda b, a: (b, a, 0, 0)),
        out_shape=jax.ShapeDtypeStruct((B, A, n_sub, Dh), BF16),
        compiler_params=_cparams(("parallel", "parallel")),
        name="nsa_compress",
    )(ksub, pos2, w1s.astype(BF16), w2s.astype(BF16))


def _qproj_kernel(x_ref, sc_ref, sh_ref, wq_ref, wg_ref, cos_ref, sin_ref, q_ref, g_ref):
    hb = _modulated_bf16(x_ref, sc_ref, sh_ref)
    cos, sin = cos_ref[0], sin_ref[0]
    dh = NSA_HEAD_DIM
    heads_per_slab = LANES // dh
    for j in range(wq_ref.shape[-1] // LANES):
        slab = _rope_slab(_dot(hb, wq_ref[:, j * LANES:(j + 1) * LANES]), cos, sin) * ATTN_SCALE
        for u in range(heads_per_slab):
            q_ref[0, j * heads_per_slab + u] = slab[:, u * dh:(u + 1) * dh].astype(BF16)
    g_ref[0] = jax.nn.sigmoid(_dot(hb, wg_ref[...]))


def _qproj(x, sc, sh, q_w, q_dim, cos, sin, tm=512):
    B, S, D = x.shape
    heads = q_dim // NSA_HEAD_DIM
    n_gate = q_w.shape[-1] - q_dim
    wq = q_w[:, :q_dim].astype(BF16)
    wg = jnp.pad(q_w[:, q_dim:], ((0, 0), (0, LANES - n_gate))).astype(BF16)
    return pl.pallas_call(
        _qproj_kernel,
        grid=(B, S // tm),
        in_specs=[_row_spec(tm, D), _batch_vec_spec(D), _batch_vec_spec(D), _const_spec((D, q_dim)),
                  _const_spec((D, LANES)), _row_spec(tm, LANES), _row_spec(tm, LANES)],
        out_specs=[pl.BlockSpec((1, heads, tm, NSA_HEAD_DIM), lambda b, s: (b, 0, s, 0)), _row_spec(tm, LANES)],
        out_shape=[jax.ShapeDtypeStruct((B, heads, S, NSA_HEAD_DIM), BF16),
                   jax.ShapeDtypeStruct((B, S, LANES), F32)],
        compiler_params=_cparams(("parallel", "parallel")),
        name="nsa_q_proj",
    )(x, sc, sh, wq, wg, cos, sin)


def _cmp_select_kernel(q_ref, k_ref, v_ref, ov_ref, o_ref, sel_ref):
    i = pl.program_id(2)
    n_rep, tq, dh = q_ref.shape[1:]
    n_cmp = k_ref.shape[2]
    n_slc = ov_ref.shape[-1]
    kc = k_ref[0, 0]
    vc = v_ref[0, 0]
    t = i * tq + lax.broadcasted_iota(jnp.int32, (tq, 1), 0)
    blk_end = lax.broadcasted_iota(jnp.int32, (1, n_cmp), 1) * CMP_STRIDE + (CMP_BLOCK - 1)
    mask = blk_end <= t
    imp = jnp.zeros((tq, n_slc), F32)
    for r in range(n_rep):
        s = lax.dot_general(q_ref[0, r], kc, NT_DIMS, preferred_element_type=F32)
        s = jnp.where(mask, s, NEG_BIG)
        m = jnp.max(s, axis=-1, keepdims=True)
        p = jnp.where(mask, jnp.exp(s - m), 0.0)
        p = p / jnp.maximum(jnp.sum(p, axis=-1, keepdims=True), F32_TINY)
        pb = p.astype(BF16)
        o_ref[0, :, r * dh:(r + 1) * dh] = _dot(pb, vc)
        imp = imp + _dot(pb, ov_ref[...])
    t_blk = t // SLC_BLOCK
    j = lax.broadcasted_iota(jnp.int32, (1, n_slc), 1)
    forced = (j == 0) | (j == t_blk) | (j == t_blk - 1)
    valid = j <= t_blk
    score = jnp.where(valid, jnp.where(forced, FORCED_SCORE, imp), -jnp.inf)
    rank = jnp.zeros((tq, n_slc), jnp.int32)
    for jp in range(n_slc):
        c = score[:, jp:jp + 1]
        beats = (c > score) | ((c == score) & (jp < j))
        rank = rank + beats.astype(jnp.int32)
    n_sel = min(N_SELECT, n_slc)
    sel_ref[0, 0] = jnp.where((rank < n_sel) & valid, 1.0, 0.0).astype(sel_ref.dtype)


def _cmp_select(q, kvc, n_rep, tq=128):
    B, H, S, Dh = q.shape
    G = H // n_rep
    n_cmp = kvc.shape[2]
    n_slc = S // SLC_BLOCK
    cs = jnp.arange(n_cmp)[:, None] * CMP_STRIDE
    js = jnp.arange(n_slc)[None, :] * SLC_BLOCK
    ov = jnp.maximum(jnp.minimum(cs + CMP_BLOCK, js + SLC_BLOCK) - jnp.maximum(cs, js), 0).astype(F32) / CMP_BLOCK
    return pl.pallas_call(
        _cmp_select_kernel,
        grid=(B, G, S // tq),
        in_specs=[pl.BlockSpec((1, n_rep, tq, Dh), lambda b, g, i: (b, g, i, 0)),
                  pl.BlockSpec((1, 1, n_cmp, Dh), lambda b, g, i: (b, g, 0, 0)),
                  pl.BlockSpec((1, 1, n_cmp, Dh), lambda b, g, i: (b, G + g, 0, 0)),
                  pl.BlockSpec((n_cmp, n_slc), lambda b, g, i: (0, 0))],
        out_specs=[pl.BlockSpec((1, tq, n_rep * Dh), lambda b, g, i: (b, i, g)),
                   pl.BlockSpec((1, 1, tq, n_slc), lambda b, g, i: (b, g, i, 0))],
        out_shape=[jax.ShapeDtypeStruct((B, S, H * Dh), F32), jax.ShapeDtypeStruct((B, G, S, n_slc), BF16)],
        compiler_params=_cparams(("parallel", "parallel", "parallel")),
        name="nsa_cmp_select",
    )(q, kvc, kvc, ov.astype(BF16))


def _attn_kernel(*refs, mode, tk):
    if mode == "slc":
        q_ref, sel_ref, et_ref, k_ref, v_ref, o_ref, m_s, l_s, acc_s = refs
    else:
        q_ref, k_ref, v_ref, o_ref, m_s, l_s, acc_s = refs
    i = pl.program_id(2)
    n_rep, tq, dh = q_ref.shape[1:]
    q = q_ref[0].reshape(n_rep * tq, dh)
    m_s[...] = jnp.full(m_s.shape, NEG_BIG, F32)
    l_s[...] = jnp.zeros(l_s.shape, F32)
    acc_s[...] = jnp.zeros(acc_s.shape, F32)
    t = i * tq + lax.broadcasted_iota(jnp.int32, (tq, 1), 0)
    if mode == "slc":
        sel = sel_ref[0, 0]
        c_lo = 0
    else:
        c_lo = jnp.maximum(i * tq - (WINDOW - 1), 0) // tk
    c_hi = ((i + 1) * tq + tk - 1) // tk

    def body(c, carry):
        k0 = pl.multiple_of(c * tk, tk)
        kc = k_ref[0, 0, pl.ds(k0, tk), :]
        vc = v_ref[0, 0, pl.ds(k0, tk), :]
        kpos = k0 + lax.broadcasted_iota(jnp.int32, (1, tk), 1)
        allowed = kpos <= t
        if mode == "slc":
            hit = lax.dot_general(sel, et_ref[pl.ds(k0, tk), :], NT_DIMS, preferred_element_type=F32)
            allowed = allowed & (hit > 0.5)
        else:
            allowed = allowed & (t - kpos < WINDOW)
        bias = jnp.where(allowed, 0.0, NEG_BIG)
        s = lax.dot_general(q, kc, NT_DIMS, preferred_element_type=F32)
        s = (s.reshape(n_rep, tq, tk) + bias[None]).reshape(n_rep * tq, tk)
        m_old = m_s[...]
        m_new = jnp.maximum(m_old, jnp.max(s, axis=-1, keepdims=True))
        p = jnp.exp(s - m_new)
        alpha = jnp.exp(m_old - m_new)
        l_s[...] = alpha * l_s[...] + jnp.sum(p, axis=-1, keepdims=True)
        acc_s[...] = alpha * acc_s[...] + _dot(p.astype(BF16), vc)
        m_s[...] = m_new
        return carry

    lax.fori_loop(c_lo, c_hi, body, 0)
    out = acc_s[...] / jnp.maximum(l_s[...], F32_TINY)
    for r in range(n_rep):
        o_ref[0, :, r * dh:(r + 1) * dh] = out[r * tq:(r + 1) * tq, :]


def _attention(q, k, v, n_rep, mode, sel=None, tq=128, tk=512):
    B, H, S, Dh = q.shape
    G = H // n_rep
    tk = min(tk, S)
    in_specs = [pl.BlockSpec((1, n_rep, tq, Dh), lambda b, g, i: (b, g, i, 0))]
    args = [q]
    if mode == "slc":
        n_slc = sel.shape[-1]
        e_t = (jnp.arange(S, dtype=jnp.int32)[:, None] // SLC_BLOCK == jnp.arange(n_slc, dtype=jnp.int32)[None, :]).astype(BF16)
        in_specs += [pl.BlockSpec((1, 1, tq, n_slc), lambda b, g, i: (b, g, i, 0)),
                     pl.BlockSpec((S, n_slc), lambda b, g, i: (0, 0))]
        args += [sel, e_t]
    in_specs += [pl.BlockSpec((1, 1, S, Dh), lambda b, g, i: (b, g, 0, 0)),
                 pl.BlockSpec((1, 1, S, Dh), lambda b, g, i: (b, g, 0, 0))]
    args += [k, v]
    return pl.pallas_call(
        functools.partial(_attn_kernel, mode=mode, tk=tk),
        grid=(B, G, S // tq),
        in_specs=in_specs,
        out_specs=pl.BlockSpec((1, tq, n_rep * Dh), lambda b, g, i: (b, i, g)),
        out_shape=jax.ShapeDtypeStruct((B, S, H * Dh), F32),
        scratch_shapes=[pltpu.VMEM((n_rep * tq, 1), F32), pltpu.VMEM((n_rep * tq, 1), F32),
                        pltpu.VMEM((n_rep * tq, Dh), F32)],
        compiler_params=_cparams(("parallel", "parallel", "arbitrary")),
        name="nsa_attn_" + mode,
    )(*args)


def _gate_oproj_norm_kernel(oc_ref, os_ref, ow_ref, gate_ref, ex_ref, w_ref, x_ref, gt_ref, g_ref, b_ref, out_ref):
    g_parts = _split_bf16(gate_ref[0], 2)
    o = None
    for br, o_ref in enumerate((oc_ref, os_ref, ow_ref)):
        ge = _dot(g_parts[0], ex_ref[br]) + _dot(g_parts[1], ex_ref[br])
        o = ge * o_ref[0] if o is None else o + ge * o_ref[0]
    y = _dot(o.astype(BF16), w_ref[...])
    v = ALPHA * x_ref[0] + (1.0 + gt_ref[0]) * y
    out_ref[0] = _layer_norm_rows(v, g_ref[...], b_ref[...])


def _gate_oproj_norm(o_cmp, o_slc, o_win, gates, o_w, x, gate_t, ln_g, ln_b, tm=256):
    B, S, D = x.shape
    QD = o_w.shape[0]
    lane = jnp.arange(LANES)[:, None]
    head = jnp.arange(QD)[None, :] // NSA_HEAD_DIM
    expand = jnp.stack([(lane == head * NSA_N_BRANCH + br) for br in range(NSA_N_BRANCH)]).astype(BF16)
    return pl.pallas_call(
        _gate_oproj_norm_kernel,
        grid=(B, S // tm),
        in_specs=[_row_spec(tm, QD), _row_spec(tm, QD), _row_spec(tm, QD), _row_spec(tm, LANES),
                  _const_spec((NSA_N_BRANCH, LANES, QD)), _const_spec((QD, D)), _row_spec(tm, D),
                  _batch_vec_spec(D), _const_spec((1, D)), _const_spec((1, D))],
        out_specs=_row_spec(tm, D),
        out_shape=jax.ShapeDtypeStruct((B, S, D), F32),
        compiler_params=_cparams(("parallel", "parallel")),
        name="nsa_gate_oproj_norm",
    )(o_cmp, o_slc, o_win, gates, expand, o_w.astype(BF16), x, gate_t, ln_g.reshape(1, D), ln_b.reshape(1, D))


def _nsa_shared_kv(x, sc, sh, kv_w, cmp_pos, phi_k_w1, phi_k_w2, phi_v_w1, phi_v_w2, cos, sin):
    kc, ka = _kvproj(x, sc, sh, kv_w, cos, sin)
    kvc = _compress(kc, cmp_pos, jnp.stack([phi_k_w1, phi_v_w1]), jnp.stack([phi_k_w2, phi_v_w2]))
    return kvc, ka


def _nsa_layer(x, sh, sc, gate_t, q_w, o_w, kvc, ka, cos, sin, ln_g, ln_b):
    G = NSA_KV_HEADS
    q_dim = o_w.shape[0]
    n_rep = q_dim // NSA_HEAD_DIM // G
    q, gates = _qproj(x, sc, sh, q_w, q_dim, cos, sin)
    o_cmp, sel = _cmp_select(q, kvc, n_rep)
    o_slc = _attention(q, ka[:, 0:G], ka[:, G:2 * G], n_rep, "slc", sel=sel)
    o_win = _attention(q, ka[:, 2 * G:3 * G], ka[:, 3 * G:4 * G], n_rep, "win")
    return _gate_oproj_norm(o_cmp, o_slc, o_win, gates, o_w, x, gate_t, ln_g, ln_b)


def kernel(x, c, pos, ada_w, ada_b, ln_g, ln_b, ssm_in_w, ssm_conv_w, ssm_conv_b, ssm_dt_bias, ssm_a_log, ssm_d, ssm_norm_w, ssm_out_w, kv_ada_w, kv_ada_b, kv_w, cmp_pos, phi_k_w1, phi_k_w2, phi_v_w1, phi_v_w2, nsa_q_w, nsa_o_w, router_w, router_b, moe_w_up, moe_b_up, moe_w_down, moe_b_down):
    c_act = jax.nn.silu(c)
    cos, sin = _rope_tables(pos)
    shared = None
    for i in range(DEPTH):
        mod = c_act @ ada_w[i] + ada_b[i]
        sh_t, sc_t, g_t, sh_c, sc_c, g_c = [m[:, None, :] for m in jnp.split(mod, 6, axis=-1)]
        if i < N_A:
            x = _mamba_layer(x, sh_t, sc_t, g_t, ssm_in_w[i], ssm_conv_w[i], ssm_conv_b[i], ssm_dt_bias[i],
                             ssm_a_log[i], ssm_d[i], ssm_norm_w[i], ssm_out_w[i], ln_g[i, 0], ln_b[i, 0])
        else:
            j = i - N_A
            x = _nsa_layer(x, sh_t, sc_t, g_t, nsa_q_w[j], nsa_o_w[j], *shared, cos, sin, ln_g[i, 0], ln_b[i, 0])
        x = _moe_layer(x, sh_c, sc_c, g_c, router_w[i], router_b[i],
                       moe_w_up[i], moe_b_up[i], moe_w_down[i], moe_b_down[i], ln_g[i, 1], ln_b[i, 1])
        if i == N_A - 1:
            kv_sh, kv_sc = [m[:, None, :] for m in jnp.split(c_act @ kv_ada_w + kv_ada_b, 2, axis=-1)]
            shared = _nsa_shared_kv(x, kv_sc, kv_sh, kv_w, cmp_pos, phi_k_w1, phi_k_w2, phi_v_w1, phi_v_w2, cos, sin)
    return x
```

```python
import functools

import jax
import jax.numpy as jnp
import numpy as np
from jax import lax
from jax.experimental import pallas as pl
from jax.experimental.pallas import tpu as pltpu

F32 = jnp.float32
BF16 = jnp.bfloat16

DEPTH = 4
N_A = DEPTH // 2
ALPHA = (2.0 * DEPTH) ** 0.25
LN_EPS = 1e-5

SSM_HEAD_DIM = 64
SSM_GROUPS = 4
SSM_STATE = 128
CONV_WIDTH = 4
SSM_CHUNK = 128

NSA_HEAD_DIM = 64
NSA_KV_HEADS = 4
NSA_N_BRANCH = 3
CMP_BLOCK = 32
CMP_STRIDE = 16
SLC_BLOCK = 64
N_SELECT = 16
WINDOW = 512
ROPE_THETA = 10000.0
ATTN_SCALE = NSA_HEAD_DIM ** -0.5
FORCED_SCORE = 1e9

TOP_K = 4
SWIGLU_LIMIT = 7.0
SWIGLU_ALPHA = 1.702

VMEM_LIMIT_BYTES_V7X = 56 * 1024 * 1024
LANES = 128
SUBLANES = 8
NEG_BIG = -1e30
F32_TINY = float(np.finfo(np.float32).tiny)
NT_DIMS = (((1,), (1,)), ((), ()))
TN_DIMS = (((0,), (0,)), ((), ()))


def _cparams(sem):
    return pltpu.CompilerParams(dimension_semantics=sem, vmem_limit_bytes=VMEM_LIMIT_BYTES_V7X)


def _dot(a, b):
    return jnp.dot(a, b, preferred_element_type=F32)


def _split_bf16(a, parts):
    out = []
    r = a
    for _ in range(parts):
        p = r.astype(BF16)
        out.append(p)
        r = r - p.astype(F32)
    return out


def _layer_norm_rows(v, g, b):
    mu = jnp.mean(v, axis=-1, keepdims=True)
    d = v - mu
    var = jnp.mean(d * d, axis=-1, keepdims=True)
    return d * lax.rsqrt(var + LN_EPS) * g + b


def _modulated_bf16(x_ref, sc_ref, sh_ref):
    return (x_ref[0] * (1.0 + sc_ref[0]) + sh_ref[0]).astype(BF16)


def _row_spec(tm, width):
    return pl.BlockSpec((1, tm, width), lambda b, s: (b, s, 0))


def _batch_vec_spec(width):
    return pl.BlockSpec((1, 1, width), lambda b, s: (b, 0, 0))


def _const_spec(shape):
    nd = len(shape)
    return pl.BlockSpec(shape, lambda b, s: (0,) * nd)


def _router_kernel(x_ref, sc_ref, sh_ref, rw_ref, rb_ref, h_ref, idx_ref, gate_ref, cnt_ref, running):
    first_step = (pl.program_id(0) == 0) & (pl.program_id(1) == 0)

    @pl.when(first_step)
    def _():
        running[...] = jnp.zeros(running.shape, F32)

    hb = _modulated_bf16(x_ref, sc_ref, sh_ref)
    h_ref[0] = hb
    logits = _dot(hb, rw_ref[...]) + rb_ref[...]
    tm, n_exp = logits.shape
    lane_e = lax.broadcasted_iota(jnp.int32, logits.shape, 1)
    lane_o = lax.broadcasted_iota(jnp.int32, idx_ref.shape[1:], 1)
    vals, idxs = [], []
    l = logits
    for _ in range(TOP_K):
        m = jnp.max(l, axis=-1, keepdims=True)
        idx = jnp.min(jnp.where(l == m, lane_e, n_exp), axis=-1, keepdims=True)
        vals.append(m)
        idxs.append(idx)
        l = jnp.where(lane_e == idx, -jnp.inf, l)
    es = [jnp.exp(v - vals[0]) for v in vals]
    denom = es[0] + es[1] + es[2] + es[3]
    idx_out = jnp.zeros(idx_ref.shape[1:], jnp.int32)
    gate_out = jnp.zeros(gate_ref.shape[1:], F32)
    chosen = sum(jnp.where(lane_e == ix, 1.0, 0.0) for ix in idxs)
    row = lax.broadcasted_iota(jnp.int32, (tm, tm), 0)
    col = lax.broadcasted_iota(jnp.int32, (tm, tm), 1)
    earlier = jnp.where(col < row, 1.0, 0.0).astype(BF16)
    before = _dot(earlier, chosen.astype(BF16)) + running[...]
    for k in range(TOP_K):
        rank_k = jnp.sum(jnp.where(lane_e == idxs[k], before, 0.0), axis=-1, keepdims=True).astype(jnp.int32)
        idx_out = jnp.where(lane_o == k, idxs[k], idx_out)
        idx_out = jnp.where(lane_o == TOP_K + k, rank_k, idx_out)
        gate_out = jnp.where(lane_o == k, es[k] / denom, gate_out)
    idx_ref[0] = idx_out
    gate_ref[0] = gate_out
    running[...] = running[...] + jnp.sum(chosen, axis=0, keepdims=True)
    cnt_ref[...] = jnp.broadcast_to(running[...], cnt_ref.shape)


def _router(x, sc, sh, rw, rb, tm=512):
    B, S, D = x.shape
    E = rw.shape[-1]
    return pl.pallas_call(
        _router_kernel,
        grid=(B, S // tm),
        in_specs=[_row_spec(tm, D), _batch_vec_spec(D), _batch_vec_spec(D), _const_spec((D, E)), _const_spec((1, E))],
        out_specs=[_row_spec(tm, D), _row_spec(tm, LANES), _row_spec(tm, LANES), _const_spec((SUBLANES, E))],
        out_shape=[
            jax.ShapeDtypeStruct((B, S, D), BF16),
            jax.ShapeDtypeStruct((B, S, LANES), jnp.int32),
            jax.ShapeDtypeStruct((B, S, LANES), F32),
            jax.ShapeDtypeStruct((SUBLANES, E), F32),
        ],
        scratch_shapes=[pltpu.VMEM((1, E), F32)],
        compiler_params=_cparams(("arbitrary", "arbitrary")),
        name="moe_router",
    )(x, sc, sh, rw.astype(BF16), rb.reshape(1, E))


def _ffn_kernel(blk_e_ref, first_ref, nused_ref, x_ref, wu_ref, bu_ref, wd_ref, bd_ref, o_ref, wu_s, wd_s):
    i = pl.program_id(0)
    d_exp = wd_s.shape[0]

    @pl.when(i < nused_ref[0])
    def _():
        @pl.when(first_ref[i] == 1)
        def _():
            wu_s[...] = wu_ref[0].astype(BF16)
            wd_s[...] = wd_ref[0].astype(BF16)

        u = _dot(x_ref[...], wu_s[...]) + bu_ref[0]
        glu = jnp.minimum(u[:, :d_exp], SWIGLU_LIMIT)
        lin = jnp.clip(u[:, d_exp:], -SWIGLU_LIMIT, SWIGLU_LIMIT)
        act = glu * jax.nn.sigmoid(SWIGLU_ALPHA * glu) * (lin + 1.0)
        o_ref[...] = _dot(act.astype(BF16), wd_s[...]) + bd_ref[0]

    @pl.when(i >= nused_ref[0])
    def _():
        o_ref[...] = jnp.zeros_like(o_ref)


def _expert_ffn(xg, blk_e, first, nused, layer, w_up, b_up, w_down, b_down, tm):
    R, D = xg.shape
    n_layers, E, _, N2 = w_up.shape
    d_exp = w_down.shape[2]
    n_blk = R // tm
    grid_spec = pltpu.PrefetchScalarGridSpec(
        num_scalar_prefetch=3,
        grid=(n_blk,),
        in_specs=[
            pl.BlockSpec((tm, D), lambda i, be, fi, nu: (i, 0)),
            pl.BlockSpec((None, 1, D, N2), lambda i, be, fi, nu: (layer, be[i], 0, 0)),
            pl.BlockSpec((None, 1, 1, N2), lambda i, be, fi, nu: (layer, be[i], 0, 0)),
            pl.BlockSpec((None, 1, d_exp, D), lambda i, be, fi, nu: (layer, be[i], 0, 0)),
            pl.BlockSpec((None, 1, 1, D), lambda i, be, fi, nu: (layer, be[i], 0, 0)),
        ],
        out_specs=pl.BlockSpec((tm, D), lambda i, be, fi, nu: (i, 0)),
        scratch_shapes=[pltpu.VMEM((D, N2), BF16), pltpu.VMEM((d_exp, D), BF16)],
    )
    return pl.pallas_call(
        _ffn_kernel,
        grid_spec=grid_spec,
        out_shape=jax.ShapeDtypeStruct((R, D), F32),
        compiler_params=_cparams(("arbitrary",)),
        name="moe_expert_ffn",
    )(blk_e, first, nused, xg, w_up, b_up.reshape(n_layers, E, 1, N2), w_down, b_down.reshape(n_layers, E, 1, D))


def _combine_norm_kernel(x_ref, yg_ref, gate_ref, gc_ref, g_ref, b_ref, o_ref):
    gate = gate_ref[0]
    y = yg_ref[0, 0] * gate[:, 0:1]
    for k in range(1, TOP_K):
        y = y + yg_ref[k, 0] * gate[:, k:k + 1]
    v = ALPHA * x_ref[0] + (1.0 + gc_ref[0]) * y
    o_ref[0] = _layer_norm_rows(v, g_ref[...], b_ref[...])


def _combine_norm(x, yg, gate, gate_c, ln_g, ln_b, tm=256):
    B, S, D = x.shape
    return pl.pallas_call(
        _combine_norm_kernel,
        grid=(B, S // tm),
        in_specs=[
            _row_spec(tm, D),
            pl.BlockSpec((TOP_K, 1, tm, D), lambda b, s: (0, b, s, 0)),
            _row_spec(tm, LANES),
            _batch_vec_spec(D),
            _const_spec((1, D)),
            _const_spec((1, D)),
        ],
        out_specs=_row_spec(tm, D),
        out_shape=jax.ShapeDtypeStruct((B, S, D), F32),
        compiler_params=_cparams(("parallel", "parallel")),
        name="moe_combine_norm",
    )(x, yg, gate, gate_c, ln_g.reshape(1, D), ln_b.reshape(1, D))


def _moe_layer(x, sh, sc, gate_c, layer, router_w, router_b, w_up, b_up, w_down, b_down, ln_g, ln_b, row_block=256):
    B, S, D = x.shape
    T = B * S
    E = router_w.shape[-1]
    h, route, gate, cnt = _router(x, sc, sh, router_w, router_b)
    A = T * TOP_K
    eid = route[:, :, :TOP_K].reshape(A)
    rank = route[:, :, TOP_K:2 * TOP_K].reshape(A)
    counts = cnt[0].astype(jnp.int32)
    padded = (counts + row_block - 1) // row_block * row_block
    off = jnp.cumsum(counts) - counts
    cum_padded = jnp.cumsum(padded)
    poff = cum_padded - padded
    pos = poff[eid] + rank
    R_rows = A + E * row_block
    n_blk = R_rows // row_block
    nused = (cum_padded[-1] // row_block).astype(jnp.int32)
    blk_start = jnp.arange(n_blk, dtype=jnp.int32) * row_block
    blk_e = jnp.minimum(jnp.sum(cum_padded[None, :] <= blk_start[:, None], axis=1), E - 1).astype(jnp.int32)
    last_e = blk_e[jnp.maximum(nused - 1, 0)]
    blk_e = jnp.where(jnp.arange(n_blk) < nused, blk_e, last_e)
    first = jnp.concatenate([jnp.ones((1,), jnp.int32), (blk_e[1:] != blk_e[:-1]).astype(jnp.int32)])
    order = jnp.argsort(eid).astype(jnp.int32)
    row = jnp.arange(R_rows, dtype=jnp.int32)
    row_e = jnp.repeat(blk_e, row_block)
    j = row - poff[row_e]
    row_tok = jnp.where(j < counts[row_e], order[jnp.minimum(off[row_e] + j, A - 1)] // TOP_K, 0)
    xg = h.reshape(T, D)[row_tok]
    y = _expert_ffn(xg, blk_e, first, nused.reshape(1), layer, w_up, b_up, w_down, b_down, row_block)
    yg = y[pos.reshape(T, TOP_K).T].reshape(TOP_K, B, S, D)
    return _combine_norm(x, yg, gate, gate_c, ln_g, ln_b)


def _outproj_norm_kernel(y_ref, w_ref, x_ref, gt_ref, g_ref, b_ref, o_ref):
    y = _dot(y_ref[0], w_ref[...])
    v = ALPHA * x_ref[0] + (1.0 + gt_ref[0]) * y
    o_ref[0] = _layer_norm_rows(v, g_ref[...], b_ref[...])


def _outproj_norm(y, w, x, gate_t, ln_g, ln_b, tm=512):
    B, S, D = x.shape
    K = y.shape[-1]
    return pl.pallas_call(
        _outproj_norm_kernel,
        grid=(B, S // tm),
        in_specs=[_row_spec(tm, K), _const_spec((K, D)), _row_spec(tm, D), _batch_vec_spec(D),
                  _const_spec((1, D)), _const_spec((1, D))],
        out_specs=_row_spec(tm, D),
        out_shape=jax.ShapeDtypeStruct((B, S, D), F32),
        compiler_params=_cparams(("parallel", "parallel")),
        name="outproj_norm",
    )(y, w.astype(BF16), x, gate_t, ln_g.reshape(1, D), ln_b.reshape(1, D))


def _ssm_inproj_kernel(x_ref, sc_ref, sh_ref, wz_ref, wx_ref, wdt_ref, cw_ref, cb_ref, dtb_ref,
                       z_ref, xs_ref, bm_ref, cm_ref, dt_ref, xbuf, *, col_chunk):
    tm = x_ref.shape[1]
    d_inner = xs_ref.shape[-1]
    bc = bm_ref.shape[-1]
    conv_dim = wx_ref.shape[-1]
    halo = SUBLANES

    @pl.when(pl.program_id(1) == 0)
    def _():
        xbuf[0:halo, :] = jnp.zeros((halo, conv_dim), F32)

    hb = _modulated_bf16(x_ref, sc_ref, sh_ref)
    z_ref[0] = _dot(hb, wz_ref[...])
    dt_ref[0] = jax.nn.softplus(_dot(hb, wdt_ref[...]) + dtb_ref[...])
    for c0 in range(0, conv_dim, col_chunk):
        cs = slice(c0, c0 + col_chunk)
        xbuf[halo:halo + tm, cs] = _dot(hb, wx_ref[:, cs])
        acc = cb_ref[:, cs] + cw_ref[CONV_WIDTH - 1:CONV_WIDTH, cs] * xbuf[halo:halo + tm, cs]
        for w in range(CONV_WIDTH - 1):
            lo = halo - (CONV_WIDTH - 1) + w
            acc = acc + cw_ref[w:w + 1, cs] * xbuf[lo:lo + tm, cs]
        y = acc * jax.nn.sigmoid(acc)
        if c0 + col_chunk <= d_inner:
            xs_ref[0, :, cs] = y
        elif c0 + col_chunk <= d_inner + bc:
            bm_ref[0, :, c0 - d_inner:c0 - d_inner + col_chunk] = y.astype(BF16)
        else:
            cm_ref[0, :, c0 - d_inner - bc:c0 - d_inner - bc + col_chunk] = y.astype(BF16)
        xbuf[0:halo, cs] = xbuf[tm:tm + halo, cs]


def _ssm_inproj(x, sc, sh, in_w, conv_w, conv_b, dt_bias, d_inner, bc_dim, tm=256, col_chunk=512):
    B, S, D = x.shape
    conv_dim = d_inner + 2 * bc_dim
    heads = dt_bias.shape[0]
    wz = in_w[:, :d_inner].astype(BF16)
    wx = in_w[:, d_inner:d_inner + conv_dim].astype(BF16)
    wdt = jnp.pad(in_w[:, d_inner + conv_dim:], ((0, 0), (0, LANES - heads))).astype(BF16)
    dtb = jnp.pad(dt_bias, (0, LANES - heads)).reshape(1, LANES)
    assert d_inner % col_chunk == 0 and bc_dim % col_chunk == 0
    return pl.pallas_call(
        functools.partial(_ssm_inproj_kernel, col_chunk=col_chunk),
        grid=(B, S // tm),
        in_specs=[_row_spec(tm, D), _batch_vec_spec(D), _batch_vec_spec(D),
                  _const_spec((D, d_inner)), _const_spec((D, conv_dim)), _const_spec((D, LANES)),
                  _const_spec((CONV_WIDTH, conv_dim)), _const_spec((1, conv_dim)), _const_spec((1, LANES))],
        out_specs=[_row_spec(tm, d_inner), _row_spec(tm, d_inner), _row_spec(tm, bc_dim), _row_spec(tm, bc_dim),
                   _row_spec(tm, LANES)],
        out_shape=[
            jax.ShapeDtypeStruct((B, S, d_inner), F32),
            jax.ShapeDtypeStruct((B, S, d_inner), F32),
            jax.ShapeDtypeStruct((B, S, bc_dim), BF16),
            jax.ShapeDtypeStruct((B, S, bc_dim), BF16),
            jax.ShapeDtypeStruct((B, S, LANES), F32),
        ],
        scratch_shapes=[pltpu.VMEM((SUBLANES + tm + SUBLANES, conv_dim), F32)],
        compiler_params=_cparams(("parallel", "arbitrary")),
        name="ssm_inproj_conv",
    )(x, sc, sh, wz, wx, wdt, conv_w, conv_b.reshape(1, conv_dim), dtb)


def _ssd_kernel(xs_ref, bm_ref, cm_ref, z_ref, dt_ref, dtt_ref, aneg_ref, anegt_ref, dskip_ref, nw_ref,
                y_ref, state, *, heads_per_group):
    L = xs_ref.shape[1]
    P = SSM_HEAD_DIM
    N = SSM_STATE
    n_groups = bm_ref.shape[-1] // N
    gw = heads_per_group * P

    @pl.when(pl.program_id(1) == 0)
    def _():
        state[...] = jnp.zeros(state.shape, F32)

    row = lax.broadcasted_iota(jnp.int32, (L, L), 0)
    col = lax.broadcasted_iota(jnp.int32, (L, L), 1)
    lower = row >= col
    tri = jnp.where(lower, 1.0, 0.0).astype(BF16)
    tri_t = jnp.where(row <= col, 1.0, 0.0).astype(BF16)
    dt = dt_ref[0]
    a_cs = sum(_dot(tri, p) for p in _split_bf16(dt * aneg_ref[...], 3))
    a_cs_t = sum(_dot(p, tri_t) for p in _split_bf16(dtt_ref[0] * anegt_ref[...], 3))
    for g in range(n_groups):
        bg = bm_ref[0, :, g * N:(g + 1) * N]
        cg = cm_ref[0, :, g * N:(g + 1) * N]
        cb = lax.dot_general(cg, bg, NT_DIMS, preferred_element_type=F32)
        bg_t = bg.astype(F32).T
        ssq = jnp.zeros((L, 1), F32)
        ys = []
        for r in range(heads_per_group):
            h = g * heads_per_group + r
            hs = slice(h * P, (h + 1) * P)
            a_col = a_cs[:, h:h + 1]
            a_row = a_cs_t[h:h + 1, :]
            a_end = a_cs_t[h:h + 1, L - 1:L]
            x_h = xs_ref[0, :, hs]
            xdt = (x_h * dt[:, h:h + 1]).astype(BF16)
            decay = jnp.exp(jnp.where(lower, a_col - a_row, NEG_BIG))
            y = _dot((cb * decay).astype(BF16), xdt)
            s_h = state[h]
            y = y + _dot(cg, s_h.astype(BF16)) * jnp.exp(a_col)
            y = y + x_h * dskip_ref[:, hs]
            state[h] = s_h * jnp.exp(a_end) + _dot((bg_t * jnp.exp(a_end - a_row)).astype(BF16), xdt)
            zz = z_ref[0, :, hs]
            y = y * (zz * jax.nn.sigmoid(zz))
            ssq = ssq + jnp.sum(y * y, axis=-1, keepdims=True)
            ys.append(y)
        scale = lax.rsqrt(ssq / gw + LN_EPS)
        for r in range(heads_per_group):
            h = g * heads_per_group + r
            hs = slice(h * P, (h + 1) * P)
            y_ref[0, :, hs] = (ys[r] * scale * nw_ref[:, hs]).astype(BF16)


def _ssd(xs, bm, cm, z, dt, a_log, d_skip, norm_w):
    B, S, d_inner = xs.shape
    heads = a_log.shape[0]
    bc_dim = bm.shape[-1]
    L = SSM_CHUNK
    a_neg = -jnp.exp(a_log)
    dt_t = jnp.swapaxes(dt[:, :, :heads], 1, 2)
    aneg = jnp.pad(a_neg, (0, LANES - heads)).reshape(1, LANES)
    anegt = jnp.broadcast_to(a_neg[:, None], (heads, L))
    dskip = jnp.repeat(d_skip, SSM_HEAD_DIM).reshape(1, d_inner)
    return pl.pallas_call(
        functools.partial(_ssd_kernel, heads_per_group=heads // SSM_GROUPS),
        grid=(B, S // L),
        in_specs=[_row_spec(L, d_inner), _row_spec(L, bc_dim), _row_spec(L, bc_dim), _row_spec(L, d_inner),
                  _row_spec(L, LANES), pl.BlockSpec((1, heads, L), lambda b, c: (b, 0, c)),
                  _const_spec((1, LANES)), _const_spec((heads, L)), _const_spec((1, d_inner)), _const_spec((1, d_inner))],
        out_specs=_row_spec(L, d_inner),
        out_shape=jax.ShapeDtypeStruct((B, S, d_inner), BF16),
        scratch_shapes=[pltpu.VMEM((heads, SSM_STATE, SSM_HEAD_DIM), F32)],
        compiler_params=_cparams(("parallel", "arbitrary")),
        name="ssd_chunk_scan",
    )(xs, bm, cm, z, dt, dt_t, aneg, anegt, dskip, norm_w.reshape(1, d_inner))


def _mamba_layer(x, sh, sc, gate_t, in_w, conv_w, conv_b, dt_bias, a_log, d_skip, norm_w, out_w, ln_g, ln_b):
    d_inner = out_w.shape[0]
    bc_dim = SSM_GROUPS * SSM_STATE
    z, xs, bm, cm, dt = _ssm_inproj(x, sc, sh, in_w, conv_w, conv_b, dt_bias, d_inner, bc_dim)
    y = _ssd(xs, bm, cm, z, dt, a_log, d_skip, norm_w)
    return _outproj_norm(y, out_w, x, gate_t, ln_g, ln_b)


def _rope_tables(pos):
    half = NSA_HEAD_DIM // 2
    inv = ROPE_THETA ** (-jnp.arange(half, dtype=F32) / half)
    ang = pos.astype(F32)[..., None] * inv
    cos, sin = jnp.cos(ang), jnp.sin(ang)
    return jnp.concatenate([cos, cos, cos, cos], axis=-1), jnp.concatenate([-sin, sin, -sin, sin], axis=-1)


def _rope_slab(x, cos, sin):
    half = NSA_HEAD_DIM // 2
    lane = lax.broadcasted_iota(jnp.int32, (1, LANES), 1)
    first_half = (lane % NSA_HEAD_DIM) < half
    partner = jnp.where(first_half, pltpu.roll(x, LANES - half, 1), pltpu.roll(x, half, 1))
    return x * cos + partner * sin


def _kvproj_kernel(x_ref, sc_ref, sh_ref, w_ref, cos_ref, sin_ref, kc_ref, ka_ref):
    hb = _modulated_bf16(x_ref, sc_ref, sh_ref)
    cos, sin = cos_ref[0], sin_ref[0]
    dh = NSA_HEAD_DIM
    heads_per_slab = LANES // dh
    n_slab = w_ref.shape[-1] // LANES
    slabs_per_branch = NSA_KV_HEADS // heads_per_slab
    for j in range(n_slab):
        slab = _dot(hb, w_ref[:, j * LANES:(j + 1) * LANES])
        branch = j // slabs_per_branch
        if branch % 2 == 0:
            slab = _rope_slab(slab, cos, sin)
        for u in range(heads_per_slab):
            head = (j % slabs_per_branch) * heads_per_slab + u
            piece = slab[:, u * dh:(u + 1) * dh]
            if branch < 2:
                kc_ref[0, branch * NSA_KV_HEADS + head] = piece
            else:
                ka_ref[0, (branch - 2) * NSA_KV_HEADS + head] = piece.astype(BF16)


def _kvproj(x, sc, sh, kv_w, cos, sin, tm=512):
    B, S, D = x.shape
    N = kv_w.shape[-1]
    n_cmp_arr = 2 * NSA_KV_HEADS
    n_att_arr = 4 * NSA_KV_HEADS
    return pl.pallas_call(
        _kvproj_kernel,
        grid=(B, S // tm),
        in_specs=[_row_spec(tm, D), _batch_vec_spec(D), _batch_vec_spec(D), _const_spec((D, N)),
                  _row_spec(tm, LANES), _row_spec(tm, LANES)],
        out_specs=[pl.BlockSpec((1, n_cmp_arr, tm, NSA_HEAD_DIM), lambda b, s: (b, 0, s, 0)),
                   pl.BlockSpec((1, n_att_arr, tm, NSA_HEAD_DIM), lambda b, s: (b, 0, s, 0))],
        out_shape=[jax.ShapeDtypeStruct((B, n_cmp_arr, S, NSA_HEAD_DIM), F32),
                   jax.ShapeDtypeStruct((B, n_att_arr, S, NSA_HEAD_DIM), BF16)],
        compiler_params=_cparams(("parallel", "parallel")),
        name="nsa_kv_proj",
    )(x, sc, sh, kv_w.astype(BF16), cos, sin)


def _compress_kernel(k_ref, pos_ref, w1_ref, w2_ref, o_ref):
    ks = k_ref[0, 0]
    n_sub, half_in = ks.shape
    top = _dot((ks + pos_ref[0:1, :]).astype(BF16), w1_ref[0, :half_in, :])
    bot = _dot((ks + pos_ref[1:2, :]).astype(BF16), w1_ref[0, half_in:, :])
    hid = jax.nn.gelu(top + pltpu.roll(bot, n_sub - 1, 0))
    o_ref[0, 0] = _dot(hid.astype(BF16), w2_ref[0]).astype(o_ref.dtype)


def _compress(kc, cmp_pos, w1s, w2s):
    B, A, S, Dh = kc.shape
    n_sub = S // CMP_STRIDE
    half_in = CMP_STRIDE * Dh
    hid = w1s.shape[-1]
    ksub = kc.reshape(B, A, n_sub, half_in)
    pos2 = cmp_pos.reshape(2, half_in)
    return pl.pallas_call(
        _compress_kernel,
        grid=(B, A),
        in_specs=[pl.BlockSpec((1, 1, n_sub, half_in), lambda b, a: (b, a, 0, 0)),
                  pl.BlockSpec((2, half_in), lambda b, a: (0, 0)),
                  pl.BlockSpec((1, 2 * half_in, hid), lambda b, a: (a // NSA_KV_HEADS, 0, 0)),
                  pl.BlockSpec((1, hid, Dh), lambda b, a: (a // NSA_KV_HEADS, 0, 0))],
        out_specs=pl.BlockSpec((1, 1, n_sub, Dh), lambda b, a: (b, a, 0, 0)),
        out_shape=jax.ShapeDtypeStruct((B, A, n_sub, Dh), BF16),
        compiler_params=_cparams(("parallel", "parallel")),
        name="nsa_compress",
    )(ksub, pos2, w1s.astype(BF16), w2s.astype(BF16))


def _qproj_kernel(x_ref, sc_ref, sh_ref, wq_ref, wg_ref, cos_ref, sin_ref, q_ref, g_ref):
    hb = _modulated_bf16(x_ref, sc_ref, sh_ref)
    cos, sin = cos_ref[0], sin_ref[0]
    dh = NSA_HEAD_DIM
    heads_per_slab = LANES // dh
    for j in range(wq_ref.shape[-1] // LANES):
        slab = _rope_slab(_dot(hb, wq_ref[:, j * LANES:(j + 1) * LANES]), cos, sin) * ATTN_SCALE
        for u in range(heads_per_slab):
            q_ref[0, j * heads_per_slab + u] = slab[:, u * dh:(u + 1) * dh].astype(BF16)
    g_ref[0] = jax.nn.sigmoid(_dot(hb, wg_ref[...]))


def _qproj(x, sc, sh, q_w, q_dim, cos, sin, tm=512):
    B, S, D = x.shape
    heads = q_dim // NSA_HEAD_DIM
    n_gate = q_w.shape[-1] - q_dim
    wq = q_w[:, :q_dim].astype(BF16)
    wg = jnp.pad(q_w[:, q_dim:], ((0, 0), (0, LANES - n_gate))).astype(BF16)
    return pl.pallas_call(
        _qproj_kernel,
        grid=(B, S // tm),
        in_specs=[_row_spec(tm, D), _batch_vec_spec(D), _batch_vec_spec(D), _const_spec((D, q_dim)),
                  _const_spec((D, LANES)), _row_spec(tm, LANES), _row_spec(tm, LANES)],
        out_specs=[pl.BlockSpec((1, heads, tm, NSA_HEAD_DIM), lambda b, s: (b, 0, s, 0)), _row_spec(tm, LANES)],
        out_shape=[jax.ShapeDtypeStruct((B, heads, S, NSA_HEAD_DIM), BF16),
                   jax.ShapeDtypeStruct((B, S, LANES), F32)],
        compiler_params=_cparams(("parallel", "parallel")),
        name="nsa_q_proj",
    )(x, sc, sh, wq, wg, cos, sin)


def _store_heads_as_rows(o_t, n_rep, o_ref):
    dh = o_t.shape[0]
    tq = o_t.shape[1] // n_rep
    for r in range(0, n_rep, 2):
        pair = jnp.concatenate([o_t[:, r * tq:(r + 1) * tq], o_t[:, (r + 1) * tq:(r + 2) * tq]], axis=0)
        o_ref[0, :, r * dh:(r + 2) * dh] = pair.T


def _cmp_select_kernel(q_ref, k_ref, v_ref, ovt_ref, o_ref, selt_ref):
    i = pl.program_id(2)
    n_rep, tq, dh = q_ref.shape[1:]
    n_cmp = k_ref.shape[2]
    n_slc = ovt_ref.shape[0]
    q = q_ref[0].reshape(n_rep * tq, dh)
    t = i * tq + lax.broadcasted_iota(jnp.int32, (1, tq), 1)
    blk_end = lax.broadcasted_iota(jnp.int32, (n_cmp, 1), 0) * CMP_STRIDE + (CMP_BLOCK - 1)
    mask = jnp.concatenate([blk_end <= t] * n_rep, axis=1)
    s = lax.dot_general(k_ref[0, 0], q, NT_DIMS, preferred_element_type=F32)
    s = jnp.where(mask, s, NEG_BIG)
    m = jnp.max(s, axis=0, keepdims=True)
    p = jnp.where(mask, jnp.exp(s - m), 0.0)
    p = p / jnp.maximum(jnp.sum(p, axis=0, keepdims=True), F32_TINY)
    pb = p.astype(BF16)
    _store_heads_as_rows(lax.dot_general(v_ref[0, 0], pb, TN_DIMS, preferred_element_type=F32), n_rep, o_ref)
    imp = _dot(ovt_ref[...], pb[:, 0:tq])
    for r in range(1, n_rep):
        imp = imp + _dot(ovt_ref[...], pb[:, r * tq:(r + 1) * tq])
    t_blk = t // SLC_BLOCK
    j = lax.broadcasted_iota(jnp.int32, (n_slc, 1), 0)
    forced = (j == 0) | (j == t_blk) | (j == t_blk - 1)
    valid = j <= t_blk
    score = jnp.where(valid, jnp.where(forced, FORCED_SCORE, imp), -jnp.inf)
    rank = jnp.zeros((n_slc, tq), jnp.int32)
    for jp in range(n_slc):
        c = score[jp:jp + 1, :]
        beats = (c > score) | ((c == score) & (jp < j))
        rank = rank + beats.astype(jnp.int32)
    n_sel = min(N_SELECT, n_slc)
    selt_ref[0, 0] = jnp.where((rank < n_sel) & valid, 1.0, 0.0).astype(selt_ref.dtype)


def _cmp_select(q, kvc, n_rep, tq=256):
    B, H, S, Dh = q.shape
    G = H // n_rep
    n_cmp = kvc.shape[2]
    n_slc = S // SLC_BLOCK
    cs = jnp.arange(n_cmp)[None, :] * CMP_STRIDE
    js = jnp.arange(n_slc)[:, None] * SLC_BLOCK
    ovt = jnp.maximum(jnp.minimum(cs + CMP_BLOCK, js + SLC_BLOCK) - jnp.maximum(cs, js), 0).astype(F32) / CMP_BLOCK
    return pl.pallas_call(
        _cmp_select_kernel,
        grid=(B, G, S // tq),
        in_specs=[pl.BlockSpec((1, n_rep, tq, Dh), lambda b, g, i: (b, g, i, 0)),
                  pl.BlockSpec((1, 1, n_cmp, Dh), lambda b, g, i: (b, g, 0, 0)),
                  pl.BlockSpec((1, 1, n_cmp, Dh), lambda b, g, i: (b, G + g, 0, 0)),
                  pl.BlockSpec((n_slc, n_cmp), lambda b, g, i: (0, 0))],
        out_specs=[pl.BlockSpec((1, tq, n_rep * Dh), lambda b, g, i: (b, i, g)),
                   pl.BlockSpec((1, 1, n_slc, tq), lambda b, g, i: (b, g, 0, i))],
        out_shape=[jax.ShapeDtypeStruct((B, S, H * Dh), F32), jax.ShapeDtypeStruct((B, G, n_slc, S), BF16)],
        compiler_params=_cparams(("parallel", "parallel", "parallel")),
        name="nsa_cmp_select",
    )(q, kvc, kvc, ovt.astype(BF16))


def _slc_attn_kernel(q_ref, selt_ref, e_ref, k_ref, v_ref, o_ref, m_s, l_s, acc_s, *, tk):
    i = pl.program_id(2)
    n_rep, tq, dh = q_ref.shape[1:]
    q = q_ref[0].reshape(n_rep * tq, dh)
    m_s[...] = jnp.full(m_s.shape, NEG_BIG, F32)
    l_s[...] = jnp.zeros(l_s.shape, F32)
    acc_s[...] = jnp.zeros(acc_s.shape, F32)
    t = i * tq + lax.broadcasted_iota(jnp.int32, (1, tq), 1)
    selt = selt_ref[0, 0]

    def body(c, carry):
        k0 = pl.multiple_of(c * tk, tk)
        kpos = k0 + lax.broadcasted_iota(jnp.int32, (tk, 1), 0)
        hit = _dot(e_ref[pl.ds(k0, tk), :], selt)
        bias = jnp.where((kpos <= t) & (hit > 0.5), 0.0, NEG_BIG)
        s = lax.dot_general(k_ref[0, 0, pl.ds(k0, tk), :], q, NT_DIMS, preferred_element_type=F32)
        s = s + jnp.concatenate([bias] * n_rep, axis=1)
        m_old = m_s[...]
        m_new = jnp.maximum(m_old, jnp.max(s, axis=0, keepdims=True))
        p = jnp.exp(s - m_new)
        alpha = jnp.exp(m_old - m_new)
        l_s[...] = alpha * l_s[...] + jnp.sum(p, axis=0, keepdims=True)
        pv = lax.dot_general(v_ref[0, 0, pl.ds(k0, tk), :], p.astype(BF16), TN_DIMS, preferred_element_type=F32)
        acc_s[...] = alpha * acc_s[...] + pv
        m_s[...] = m_new
        return carry

    lax.fori_loop(0, ((i + 1) * tq + tk - 1) // tk, body, 0)
    _store_heads_as_rows(acc_s[...] / jnp.maximum(l_s[...], F32_TINY), n_rep, o_ref)


def _win_attn_kernel(q_ref, k_ref, v_ref, o_ref, *, span):
    i = pl.program_id(2)
    n_rep, tq, dh = q_ref.shape[1:]
    q = q_ref[0].reshape(n_rep * tq, dh)
    t = i * tq + lax.broadcasted_iota(jnp.int32, (1, tq), 1)
    k0 = pl.multiple_of(jnp.maximum(i * tq - WINDOW, 0), tq)
    kpos = k0 + lax.broadcasted_iota(jnp.int32, (span, 1), 0)
    bias = jnp.where((kpos <= t) & (t - kpos < WINDOW), 0.0, NEG_BIG)
    s = lax.dot_general(k_ref[0, 0, pl.ds(k0, span), :], q, NT_DIMS, preferred_element_type=F32)
    s = s + jnp.concatenate([bias] * n_rep, axis=1)
    p = jnp.exp(s - jnp.max(s, axis=0, keepdims=True))
    l = jnp.sum(p, axis=0, keepdims=True)
    pv = lax.dot_general(v_ref[0, 0, pl.ds(k0, span), :], p.astype(BF16), TN_DIMS, preferred_element_type=F32)
    _store_heads_as_rows(pv / jnp.maximum(l, F32_TINY), n_rep, o_ref)


def _attention(q, kv, k_off, v_off, n_rep, selt=None, tq=128, tk=512):
    B, H, S, Dh = q.shape
    G = H // n_rep
    q_spec = pl.BlockSpec((1, n_rep, tq, Dh), lambda b, g, i: (b, g, i, 0))
    kv_specs = [pl.BlockSpec((1, 1, S, Dh), lambda b, g, i: (b, k_off + g, 0, 0)),
                pl.BlockSpec((1, 1, S, Dh), lambda b, g, i: (b, v_off + g, 0, 0))]
    common = dict(
        grid=(B, G, S // tq),
        out_specs=pl.BlockSpec((1, tq, n_rep * Dh), lambda b, g, i: (b, i, g)),
        out_shape=jax.ShapeDtypeStruct((B, S, H * Dh), F32),
        compiler_params=_cparams(("parallel", "parallel", "arbitrary")),
    )
    if selt is None:
        span = WINDOW + tq
        assert span <= S
        return pl.pallas_call(functools.partial(_win_attn_kernel, span=span), in_specs=[q_spec] + kv_specs,
                              name="nsa_attn_win", **common)(q, kv, kv)
    n_slc = selt.shape[2]
    tk = min(tk, S)
    e = (jnp.arange(S, dtype=jnp.int32)[:, None] // SLC_BLOCK == jnp.arange(n_slc, dtype=jnp.int32)[None, :]).astype(BF16)
    return pl.pallas_call(
        functools.partial(_slc_attn_kernel, tk=tk),
        in_specs=[q_spec, pl.BlockSpec((1, 1, n_slc, tq), lambda b, g, i: (b, g, 0, i)),
                  pl.BlockSpec((S, n_slc), lambda b, g, i: (0, 0))] + kv_specs,
        scratch_shapes=[pltpu.VMEM((1, n_rep * tq), F32), pltpu.VMEM((1, n_rep * tq), F32),
                        pltpu.VMEM((Dh, n_rep * tq), F32)],
        name="nsa_attn_slc", **common)(q, selt, e, kv, kv)


def _gate_oproj_norm_kernel(oc_ref, os_ref, ow_ref, gate_ref, ex_ref, w_ref, x_ref, gt_ref, g_ref, b_ref, out_ref):
    g_parts = _split_bf16(gate_ref[0], 2)
    o = None
    for br, o_ref in enumerate((oc_ref, os_ref, ow_ref)):
        ge = _dot(g_parts[0], ex_ref[br]) + _dot(g_parts[1], ex_ref[br])
        o = ge * o_ref[0] if o is None else o + ge * o_ref[0]
    y = _dot(o.astype(BF16), w_ref[...])
    v = ALPHA * x_ref[0] + (1.0 + gt_ref[0]) * y
    out_ref[0] = _layer_norm_rows(v, g_ref[...], b_ref[...])


def _gate_oproj_norm(o_cmp, o_slc, o_win, gates, o_w, x, gate_t, ln_g, ln_b, tm=256):
    B, S, D = x.shape
    QD = o_w.shape[0]
    lane = jnp.arange(LANES)[:, None]
    head = jnp.arange(QD)[None, :] // NSA_HEAD_DIM
    expand = jnp.stack([(lane == head * NSA_N_BRANCH + br) for br in range(NSA_N_BRANCH)]).astype(BF16)
    return pl.pallas_call(
        _gate_oproj_norm_kernel,
        grid=(B, S // tm),
        in_specs=[_row_spec(tm, QD), _row_spec(tm, QD), _row_spec(tm, QD), _row_spec(tm, LANES),
                  _const_spec((NSA_N_BRANCH, LANES, QD)), _const_spec((QD, D)), _row_spec(tm, D),
                  _batch_vec_spec(D), _const_spec((1, D)), _const_spec((1, D))],
        out_specs=_row_spec(tm, D),
        out_shape=jax.ShapeDtypeStruct((B, S, D), F32),
        compiler_params=_cparams(("parallel", "parallel")),
        name="nsa_gate_oproj_norm",
    )(o_cmp, o_slc, o_win, gates, expand, o_w.astype(BF16), x, gate_t, ln_g.reshape(1, D), ln_b.reshape(1, D))


def _nsa_shared_kv(x, sc, sh, kv_w, cmp_pos, phi_k_w1, phi_k_w2, phi_v_w1, phi_v_w2, cos, sin):
    kc, ka = _kvproj(x, sc, sh, kv_w, cos, sin)
    kvc = _compress(kc, cmp_pos, jnp.stack([phi_k_w1, phi_v_w1]), jnp.stack([phi_k_w2, phi_v_w2]))
    return kvc, ka


def _nsa_layer(x, sh, sc, gate_t, q_w, o_w, kvc, ka, cos, sin, ln_g, ln_b):
    G = NSA_KV_HEADS
    q_dim = o_w.shape[0]
    n_rep = q_dim // NSA_HEAD_DIM // G
    q, gates = _qproj(x, sc, sh, q_w, q_dim, cos, sin)
    o_cmp, selt = _cmp_select(q, kvc, n_rep)
    o_slc = _attention(q, ka, 0, G, n_rep, selt=selt)
    o_win = _attention(q, ka, 2 * G, 3 * G, n_rep)
    return _gate_oproj_norm(o_cmp, o_slc, o_win, gates, o_w, x, gate_t, ln_g, ln_b)


def kernel(x, c, pos, ada_w, ada_b, ln_g, ln_b, ssm_in_w, ssm_conv_w, ssm_conv_b, ssm_dt_bias, ssm_a_log, ssm_d, ssm_norm_w, ssm_out_w, kv_ada_w, kv_ada_b, kv_w, cmp_pos, phi_k_w1, phi_k_w2, phi_v_w1, phi_v_w2, nsa_q_w, nsa_o_w, router_w, router_b, moe_w_up, moe_b_up, moe_w_down, moe_b_down):
    c_act = jax.nn.silu(c)
    cos, sin = _rope_tables(pos)
    shared = None
    for i in range(DEPTH):
        mod = c_act @ ada_w[i] + ada_b[i]
        sh_t, sc_t, g_t, sh_c, sc_c, g_c = [m[:, None, :] for m in jnp.split(mod, 6, axis=-1)]
        if i < N_A:
            x = _mamba_layer(x, sh_t, sc_t, g_t, ssm_in_w[i], ssm_conv_w[i], ssm_conv_b[i], ssm_dt_bias[i],
                             ssm_a_log[i], ssm_d[i], ssm_norm_w[i], ssm_out_w[i], ln_g[i, 0], ln_b[i, 0])
        else:
            j = i - N_A
            x = _nsa_layer(x, sh_t, sc_t, g_t, nsa_q_w[j], nsa_o_w[j], *shared, cos, sin, ln_g[i, 0], ln_b[i, 0])
        x = _moe_layer(x, sh_c, sc_c, g_c, i, router_w[i], router_b[i],
                       moe_w_up, moe_b_up, moe_w_down, moe_b_down, ln_g[i, 1], ln_b[i, 1])
        if i == N_A - 1:
            kv_sh, kv_sc = [m[:, None, :] for m in jnp.split(c_act @ kv_ada_w + kv_ada_b, 2, axis=-1)]
            shared = _nsa_shared_kv(x, kv_sc, kv_sh, kv_w, cmp_pos, phi_k_w1, phi_k_w2, phi_v_w1, phi_v_w2, cos, sin)
    return x
```

```python
import functools

import jax
import jax.numpy as jnp
import numpy as np
from jax import lax
from jax.experimental import pallas as pl
from jax.experimental.pallas import tpu as pltpu

F32 = jnp.float32
BF16 = jnp.bfloat16

DEPTH = 4
N_A = DEPTH // 2
ALPHA = (2.0 * DEPTH) ** 0.25
LN_EPS = 1e-5

SSM_HEAD_DIM = 64
SSM_GROUPS = 4
SSM_STATE = 128
CONV_WIDTH = 4
SSM_CHUNK = 128

NSA_HEAD_DIM = 64
NSA_KV_HEADS = 4
NSA_N_BRANCH = 3
CMP_BLOCK = 32
CMP_STRIDE = 16
SLC_BLOCK = 64
N_SELECT = 16
WINDOW = 512
ROPE_THETA = 10000.0
ATTN_SCALE = NSA_HEAD_DIM ** -0.5
FORCED_SCORE = 1e9

TOP_K = 4
SWIGLU_LIMIT = 7.0
SWIGLU_ALPHA = 1.702

VMEM_LIMIT_BYTES_V7X = 56 * 1024 * 1024
LANES = 128
SUBLANES = 8
NEG_BIG = -1e30
F32_TINY = float(np.finfo(np.float32).tiny)
LOG2_E = float(np.log2(np.e))
NT_DIMS = (((1,), (1,)), ((), ()))
TN_DIMS = (((0,), (0,)), ((), ()))


def _cparams(sem):
    return pltpu.CompilerParams(dimension_semantics=sem, vmem_limit_bytes=VMEM_LIMIT_BYTES_V7X)


def _dot(a, b):
    return jnp.dot(a, b, preferred_element_type=F32)


def _split_bf16(a, parts):
    out = []
    r = a
    for _ in range(parts):
        p = r.astype(BF16)
        out.append(p)
        r = r - p.astype(F32)
    return out


def _layer_norm_rows(v, g, b):
    mu = jnp.mean(v, axis=-1, keepdims=True)
    d = v - mu
    var = jnp.mean(d * d, axis=-1, keepdims=True)
    return d * lax.rsqrt(var + LN_EPS) * g + b


def _modulated_bf16(x_ref, sc_ref, sh_ref):
    return (x_ref[0] * (1.0 + sc_ref[0]) + sh_ref[0]).astype(BF16)


def _row_spec(tm, width):
    return pl.BlockSpec((1, tm, width), lambda b, s: (b, s, 0))


def _batch_vec_spec(width):
    return pl.BlockSpec((1, 1, width), lambda b, s: (b, 0, 0))


def _const_spec(shape):
    nd = len(shape)
    return pl.BlockSpec(shape, lambda b, s: (0,) * nd)


def _router_kernel(x_ref, sc_ref, sh_ref, rw_ref, rb_ref, h_ref, idx_ref, gate_ref, cnt_ref, running):
    first_step = (pl.program_id(0) == 0) & (pl.program_id(1) == 0)

    @pl.when(first_step)
    def _():
        running[...] = jnp.zeros(running.shape, F32)

    hb = _modulated_bf16(x_ref, sc_ref, sh_ref)
    h_ref[0] = hb
    logits = _dot(hb, rw_ref[...]) + rb_ref[...]
    tm, n_exp = logits.shape
    lane_e = lax.broadcasted_iota(jnp.int32, logits.shape, 1)
    lane_o = lax.broadcasted_iota(jnp.int32, idx_ref.shape[1:], 1)
    vals, idxs = [], []
    l = logits
    for _ in range(TOP_K):
        m = jnp.max(l, axis=-1, keepdims=True)
        idx = jnp.min(jnp.where(l == m, lane_e, n_exp), axis=-1, keepdims=True)
        vals.append(m)
        idxs.append(idx)
        l = jnp.where(lane_e == idx, -jnp.inf, l)
    es = [jnp.exp(v - vals[0]) for v in vals]
    denom = es[0] + es[1] + es[2] + es[3]
    idx_out = jnp.zeros(idx_ref.shape[1:], jnp.int32)
    gate_out = jnp.zeros(gate_ref.shape[1:], F32)
    chosen = sum(jnp.where(lane_e == ix, 1.0, 0.0) for ix in idxs)
    row = lax.broadcasted_iota(jnp.int32, (tm, tm), 0)
    col = lax.broadcasted_iota(jnp.int32, (tm, tm), 1)
    earlier = jnp.where(col < row, 1.0, 0.0).astype(BF16)
    before = _dot(earlier, chosen.astype(BF16)) + running[...]
    for k in range(TOP_K):
        rank_k = jnp.sum(jnp.where(lane_e == idxs[k], before, 0.0), axis=-1, keepdims=True).astype(jnp.int32)
        idx_out = jnp.where(lane_o == k, idxs[k], idx_out)
        idx_out = jnp.where(lane_o == TOP_K + k, rank_k, idx_out)
        gate_out = jnp.where(lane_o == k, es[k] / denom, gate_out)
    idx_ref[0] = idx_out
    gate_ref[0] = gate_out
    running[...] = running[...] + jnp.sum(chosen, axis=0, keepdims=True)
    cnt_ref[...] = jnp.broadcast_to(running[...], cnt_ref.shape)


def _router(x, sc, sh, rw, rb, tm=512):
    B, S, D = x.shape
    E = rw.shape[-1]
    return pl.pallas_call(
        _router_kernel,
        grid=(B, S // tm),
        in_specs=[_row_spec(tm, D), _batch_vec_spec(D), _batch_vec_spec(D), _const_spec((D, E)), _const_spec((1, E))],
        out_specs=[_row_spec(tm, D), _row_spec(tm, LANES), _row_spec(tm, LANES), _const_spec((SUBLANES, E))],
        out_shape=[
            jax.ShapeDtypeStruct((B, S, D), BF16),
            jax.ShapeDtypeStruct((B, S, LANES), jnp.int32),
            jax.ShapeDtypeStruct((B, S, LANES), F32),
            jax.ShapeDtypeStruct((SUBLANES, E), F32),
        ],
        scratch_shapes=[pltpu.VMEM((1, E), F32)],
        compiler_params=_cparams(("arbitrary", "arbitrary")),
        name="moe_router",
    )(x, sc, sh, rw.astype(BF16), rb.reshape(1, E))


def _ffn_kernel(blk_e_ref, first_ref, nused_ref, x_ref, wu_ref, bu_ref, wd_ref, bd_ref, o_ref, wu_s, wd_s):
    i = pl.program_id(0)
    d_exp = wd_s.shape[0]

    @pl.when(i < nused_ref[0])
    def _():
        @pl.when(first_ref[i] == 1)
        def _():
            wu_s[...] = wu_ref[0].astype(BF16)
            wd_s[...] = wd_ref[0].astype(BF16)

        u = _dot(x_ref[...], wu_s[...]) + bu_ref[0]
        glu = jnp.minimum(u[:, :d_exp], SWIGLU_LIMIT)
        lin = jnp.clip(u[:, d_exp:], -SWIGLU_LIMIT, SWIGLU_LIMIT)
        act = glu * jax.nn.sigmoid(SWIGLU_ALPHA * glu) * (lin + 1.0)
        o_ref[...] = (_dot(act.astype(BF16), wd_s[...]) + bd_ref[0]).astype(o_ref.dtype)

    @pl.when(i >= nused_ref[0])
    def _():
        o_ref[...] = jnp.zeros_like(o_ref)


def _expert_ffn(xg, blk_e, first, nused, layer, w_up, b_up, w_down, b_down, tm):
    R, D = xg.shape
    n_layers, E, _, N2 = w_up.shape
    d_exp = w_down.shape[2]
    n_blk = R // tm
    grid_spec = pltpu.PrefetchScalarGridSpec(
        num_scalar_prefetch=3,
        grid=(n_blk,),
        in_specs=[
            pl.BlockSpec((tm, D), lambda i, be, fi, nu: (i, 0)),
            pl.BlockSpec((None, 1, D, N2), lambda i, be, fi, nu: (layer, be[i], 0, 0)),
            pl.BlockSpec((None, 1, 1, N2), lambda i, be, fi, nu: (layer, be[i], 0, 0)),
            pl.BlockSpec((None, 1, d_exp, D), lambda i, be, fi, nu: (layer, be[i], 0, 0)),
            pl.BlockSpec((None, 1, 1, D), lambda i, be, fi, nu: (layer, be[i], 0, 0)),
        ],
        out_specs=pl.BlockSpec((tm, D), lambda i, be, fi, nu: (i, 0)),
        scratch_shapes=[pltpu.VMEM((D, N2), BF16), pltpu.VMEM((d_exp, D), BF16)],
    )
    return pl.pallas_call(
        _ffn_kernel,
        grid_spec=grid_spec,
        out_shape=jax.ShapeDtypeStruct((R, D), BF16),
        compiler_params=_cparams(("arbitrary",)),
        name="moe_expert_ffn",
    )(blk_e, first, nused, xg, w_up, b_up.reshape(n_layers, E, 1, N2), w_down, b_down.reshape(n_layers, E, 1, D))


def _combine_norm_kernel(x_ref, yg_ref, gate_ref, gc_ref, g_ref, b_ref, o_ref):
    gate = gate_ref[0]
    y = yg_ref[0, 0] * gate[:, 0:1]
    for k in range(1, TOP_K):
        y = y + yg_ref[k, 0] * gate[:, k:k + 1]
    v = ALPHA * x_ref[0] + (1.0 + gc_ref[0]) * y
    o_ref[0] = _layer_norm_rows(v, g_ref[...], b_ref[...])


def _combine_norm(x, yg, gate, gate_c, ln_g, ln_b, tm=256):
    B, S, D = x.shape
    return pl.pallas_call(
        _combine_norm_kernel,
        grid=(B, S // tm),
        in_specs=[
            _row_spec(tm, D),
            pl.BlockSpec((TOP_K, 1, tm, D), lambda b, s: (0, b, s, 0)),
            _row_spec(tm, LANES),
            _batch_vec_spec(D),
            _const_spec((1, D)),
            _const_spec((1, D)),
        ],
        out_specs=_row_spec(tm, D),
        out_shape=jax.ShapeDtypeStruct((B, S, D), F32),
        compiler_params=_cparams(("parallel", "parallel")),
        name="moe_combine_norm",
    )(x, yg, gate, gate_c, ln_g.reshape(1, D), ln_b.reshape(1, D))


def _moe_layer(x, sh, sc, gate_c, layer, router_w, router_b, w_up, b_up, w_down, b_down, ln_g, ln_b, row_block=256):
    B, S, D = x.shape
    T = B * S
    E = router_w.shape[-1]
    h, route, gate, cnt = _router(x, sc, sh, router_w, router_b)
    A = T * TOP_K
    eid = route[:, :, :TOP_K].reshape(A)
    rank = route[:, :, TOP_K:2 * TOP_K].reshape(A)
    counts = cnt[0].astype(jnp.int32)
    padded = (counts + row_block - 1) // row_block * row_block
    off = jnp.cumsum(counts) - counts
    cum_padded = jnp.cumsum(padded)
    poff = cum_padded - padded
    pos = poff[eid] + rank
    R_rows = A + E * row_block
    n_blk = R_rows // row_block
    nused = (cum_padded[-1] // row_block).astype(jnp.int32)
    blk_start = jnp.arange(n_blk, dtype=jnp.int32) * row_block
    blk_e = jnp.minimum(jnp.sum(cum_padded[None, :] <= blk_start[:, None], axis=1), E - 1).astype(jnp.int32)
    last_e = blk_e[jnp.maximum(nused - 1, 0)]
    blk_e = jnp.where(jnp.arange(n_blk) < nused, blk_e, last_e)
    first = jnp.concatenate([jnp.ones((1,), jnp.int32), (blk_e[1:] != blk_e[:-1]).astype(jnp.int32)])
    order = jnp.argsort(eid).astype(jnp.int32)
    row = jnp.arange(R_rows, dtype=jnp.int32)
    row_e = jnp.repeat(blk_e, row_block)
    j = row - poff[row_e]
    row_tok = jnp.where(j < counts[row_e], order[jnp.minimum(off[row_e] + j, A - 1)] // TOP_K, 0)
    xg = h.reshape(T, D)[row_tok]
    y = _expert_ffn(xg, blk_e, first, nused.reshape(1), layer, w_up, b_up, w_down, b_down, row_block)
    yg = y[pos.reshape(T, TOP_K).T].reshape(TOP_K, B, S, D)
    return _combine_norm(x, yg, gate, gate_c, ln_g, ln_b)


def _outproj_norm_kernel(y_ref, w_ref, x_ref, gt_ref, g_ref, b_ref, o_ref):
    y = _dot(y_ref[0], w_ref[...])
    v = ALPHA * x_ref[0] + (1.0 + gt_ref[0]) * y
    o_ref[0] = _layer_norm_rows(v, g_ref[...], b_ref[...])


def _outproj_norm(y, w, x, gate_t, ln_g, ln_b, tm=512):
    B, S, D = x.shape
    K = y.shape[-1]
    return pl.pallas_call(
        _outproj_norm_kernel,
        grid=(B, S // tm),
        in_specs=[_row_spec(tm, K), _const_spec((K, D)), _row_spec(tm, D), _batch_vec_spec(D),
                  _const_spec((1, D)), _const_spec((1, D))],
        out_specs=_row_spec(tm, D),
        out_shape=jax.ShapeDtypeStruct((B, S, D), F32),
        compiler_params=_cparams(("parallel", "parallel")),
        name="outproj_norm",
    )(y, w.astype(BF16), x, gate_t, ln_g.reshape(1, D), ln_b.reshape(1, D))


def _ssm_inproj_kernel(x_ref, sc_ref, sh_ref, wz_ref, wx_ref, wdt_ref, cw_ref, cb_ref, dtb_ref,
                       z_ref, xs_ref, bm_ref, cm_ref, dt_ref, xbuf, *, col_chunk):
    tm = x_ref.shape[1]
    d_inner = xs_ref.shape[-1]
    bc = bm_ref.shape[-1]
    conv_dim = wx_ref.shape[-1]
    halo = SUBLANES

    @pl.when(pl.program_id(1) == 0)
    def _():
        xbuf[0:halo, :] = jnp.zeros((halo, conv_dim), F32)

    hb = _modulated_bf16(x_ref, sc_ref, sh_ref)
    z_ref[0] = _dot(hb, wz_ref[...])
    dt_ref[0] = jax.nn.softplus(_dot(hb, wdt_ref[...]) + dtb_ref[...])
    for c0 in range(0, conv_dim, col_chunk):
        cs = slice(c0, c0 + col_chunk)
        xbuf[halo:halo + tm, cs] = _dot(hb, wx_ref[:, cs])
        acc = cb_ref[:, cs] + cw_ref[CONV_WIDTH - 1:CONV_WIDTH, cs] * xbuf[halo:halo + tm, cs]
        for w in range(CONV_WIDTH - 1):
            lo = halo - (CONV_WIDTH - 1) + w
            acc = acc + cw_ref[w:w + 1, cs] * xbuf[lo:lo + tm, cs]
        y = acc * jax.nn.sigmoid(acc)
        if c0 + col_chunk <= d_inner:
            xs_ref[0, :, cs] = y
        elif c0 + col_chunk <= d_inner + bc:
            bm_ref[0, :, c0 - d_inner:c0 - d_inner + col_chunk] = y.astype(BF16)
        else:
            cm_ref[0, :, c0 - d_inner - bc:c0 - d_inner - bc + col_chunk] = y.astype(BF16)
        xbuf[0:halo, cs] = xbuf[tm:tm + halo, cs]


def _ssm_inproj(x, sc, sh, in_w, conv_w, conv_b, dt_bias, d_inner, bc_dim, tm=256, col_chunk=512):
    B, S, D = x.shape
    conv_dim = d_inner + 2 * bc_dim
    heads = dt_bias.shape[0]
    wz = in_w[:, :d_inner].astype(BF16)
    wx = in_w[:, d_inner:d_inner + conv_dim].astype(BF16)
    wdt = jnp.pad(in_w[:, d_inner + conv_dim:], ((0, 0), (0, LANES - heads))).astype(BF16)
    dtb = jnp.pad(dt_bias, (0, LANES - heads)).reshape(1, LANES)
    assert d_inner % col_chunk == 0 and bc_dim % col_chunk == 0
    return pl.pallas_call(
        functools.partial(_ssm_inproj_kernel, col_chunk=col_chunk),
        grid=(B, S // tm),
        in_specs=[_row_spec(tm, D), _batch_vec_spec(D), _batch_vec_spec(D),
                  _const_spec((D, d_inner)), _const_spec((D, conv_dim)), _const_spec((D, LANES)),
                  _const_spec((CONV_WIDTH, conv_dim)), _const_spec((1, conv_dim)), _const_spec((1, LANES))],
        out_specs=[_row_spec(tm, d_inner), _row_spec(tm, d_inner), _row_spec(tm, bc_dim), _row_spec(tm, bc_dim),
                   _row_spec(tm, LANES)],
        out_shape=[
            jax.ShapeDtypeStruct((B, S, d_inner), F32),
            jax.ShapeDtypeStruct((B, S, d_inner), F32),
            jax.ShapeDtypeStruct((B, S, bc_dim), BF16),
            jax.ShapeDtypeStruct((B, S, bc_dim), BF16),
            jax.ShapeDtypeStruct((B, S, LANES), F32),
        ],
        scratch_shapes=[pltpu.VMEM((SUBLANES + tm + SUBLANES, conv_dim), F32)],
        compiler_params=_cparams(("parallel", "arbitrary")),
        name="ssm_inproj_conv",
    )(x, sc, sh, wz, wx, wdt, conv_w, conv_b.reshape(1, conv_dim), dtb)


def _ssd_kernel(xs_ref, bm_ref, cm_ref, z_ref, dt_ref, dtt_ref, aneg_ref, anegt_ref, dskip_ref, nw_ref, ex_ref,
                y_ref, state, *, heads_per_group):
    L = xs_ref.shape[1]
    P = SSM_HEAD_DIM
    N = SSM_STATE
    n_groups = bm_ref.shape[-1] // N
    gw = heads_per_group * P

    @pl.when(pl.program_id(1) == 0)
    def _():
        state[...] = jnp.zeros(state.shape, F32)

    row = lax.broadcasted_iota(jnp.int32, (L, L), 0)
    col = lax.broadcasted_iota(jnp.int32, (L, L), 1)
    lower = row >= col
    tri = jnp.where(lower, 1.0, 0.0).astype(BF16)
    tri_t = jnp.where(row <= col, 1.0, 0.0).astype(BF16)
    dt = dt_ref[0]
    a_cs = sum(_dot(tri, p) for p in _split_bf16(dt * aneg_ref[...], 3))
    a_cs_t = sum(_dot(p, tri_t) for p in _split_bf16(dtt_ref[0] * anegt_ref[...], 3))
    dt_parts = _split_bf16(dt, 2)
    acs_parts = _split_bf16(a_cs, 2)
    lane = lax.broadcasted_iota(jnp.int32, (1, 2 * P), 1)
    first_head = lane < P
    for g in range(n_groups):
        gs = slice(g * gw, (g + 1) * gw)
        ex = ex_ref[g]
        dt_x = _dot(dt_parts[0], ex) + _dot(dt_parts[1], ex)
        acs_x = _dot(acs_parts[0], ex) + _dot(acs_parts[1], ex)
        end_x = acs_x[L - 1:L, :]
        bg = bm_ref[0, :, g * N:(g + 1) * N]
        cg = cm_ref[0, :, g * N:(g + 1) * N]
        xs_g = xs_ref[0, :, gs]
        xdt = xs_g * dt_x
        xdt_b = xdt.astype(BF16)
        s_g = state[g]
        y = _dot(cg, s_g.astype(BF16)) * jnp.exp(acs_x) + xs_g * dskip_ref[:, gs]
        to_end = (xdt * jnp.exp(end_x - acs_x)).astype(BF16)
        state[g] = s_g * jnp.exp(end_x) + lax.dot_general(bg, to_end, TN_DIMS, preferred_element_type=F32)
        cb = lax.dot_general(cg, bg, NT_DIMS, preferred_element_type=F32)
        diag = []
        for u in range(heads_per_group // 2):
            ms = []
            for h in (g * heads_per_group + 2 * u, g * heads_per_group + 2 * u + 1):
                seg = a_cs[:, h:h + 1] - a_cs_t[h:h + 1, :]
                ms.append((cb * jnp.exp(jnp.where(lower, seg, NEG_BIG))).astype(BF16))
            xp = xdt_b[:, 2 * u * P:(2 * u + 2) * P]
            zero = jnp.zeros_like(xp)
            rhs = jnp.concatenate([jnp.where(first_head, xp, zero), jnp.where(first_head, zero, xp)], axis=0)
            diag.append(_dot(jnp.concatenate(ms, axis=1), rhs))
        y = y + jnp.concatenate(diag, axis=1)
        zz = z_ref[0, :, gs]
        y = y * (zz * jax.nn.sigmoid(zz))
        scale = lax.rsqrt(jnp.sum(y * y, axis=-1, keepdims=True) / gw + LN_EPS)
        y_ref[0, :, gs] = (y * scale * nw_ref[:, gs]).astype(BF16)


def _ssd(xs, bm, cm, z, dt, a_log, d_skip, norm_w):
    B, S, d_inner = xs.shape
    heads = a_log.shape[0]
    bc_dim = bm.shape[-1]
    L = SSM_CHUNK
    a_neg = -jnp.exp(a_log)
    dt_t = jnp.swapaxes(dt[:, :, :heads], 1, 2)
    aneg = jnp.pad(a_neg, (0, LANES - heads)).reshape(1, LANES)
    anegt = jnp.broadcast_to(a_neg[:, None], (heads, L))
    dskip = jnp.repeat(d_skip, SSM_HEAD_DIM).reshape(1, d_inner)
    hpg = heads // SSM_GROUPS
    gw = hpg * SSM_HEAD_DIM
    head_of_lane = jnp.arange(SSM_GROUPS)[:, None, None] * hpg + jnp.arange(gw)[None, None, :] // SSM_HEAD_DIM
    expand = (jnp.arange(LANES)[None, :, None] == head_of_lane).astype(BF16)
    return pl.pallas_call(
        functools.partial(_ssd_kernel, heads_per_group=hpg),
        grid=(B, S // L),
        in_specs=[_row_spec(L, d_inner), _row_spec(L, bc_dim), _row_spec(L, bc_dim), _row_spec(L, d_inner),
                  _row_spec(L, LANES), pl.BlockSpec((1, heads, L), lambda b, c: (b, 0, c)),
                  _const_spec((1, LANES)), _const_spec((heads, L)), _const_spec((1, d_inner)), _const_spec((1, d_inner)),
                  _const_spec((SSM_GROUPS, LANES, gw))],
        out_specs=_row_spec(L, d_inner),
        out_shape=jax.ShapeDtypeStruct((B, S, d_inner), BF16),
        scratch_shapes=[pltpu.VMEM((SSM_GROUPS, SSM_STATE, gw), F32)],
        compiler_params=_cparams(("parallel", "arbitrary")),
        name="ssd_chunk_scan",
    )(xs, bm, cm, z, dt, dt_t, aneg, anegt, dskip, norm_w.reshape(1, d_inner), expand)


def _mamba_layer(x, sh, sc, gate_t, in_w, conv_w, conv_b, dt_bias, a_log, d_skip, norm_w, out_w, ln_g, ln_b):
    d_inner = out_w.shape[0]
    bc_dim = SSM_GROUPS * SSM_STATE
    z, xs, bm, cm, dt = _ssm_inproj(x, sc, sh, in_w, conv_w, conv_b, dt_bias, d_inner, bc_dim)
    y = _ssd(xs, bm, cm, z, dt, a_log, d_skip, norm_w)
    return _outproj_norm(y, out_w, x, gate_t, ln_g, ln_b)


def _rope_tables(pos):
    half = NSA_HEAD_DIM // 2
    inv = ROPE_THETA ** (-jnp.arange(half, dtype=F32) / half)
    ang = pos.astype(F32)[..., None] * inv
    cos, sin = jnp.cos(ang), jnp.sin(ang)
    return jnp.concatenate([cos, cos, cos, cos], axis=-1), jnp.concatenate([-sin, sin, -sin, sin], axis=-1)


def _rope_slab(x, cos, sin):
    half = NSA_HEAD_DIM // 2
    lane = lax.broadcasted_iota(jnp.int32, (1, LANES), 1)
    first_half = (lane % NSA_HEAD_DIM) < half
    partner = jnp.where(first_half, pltpu.roll(x, LANES - half, 1), pltpu.roll(x, half, 1))
    return x * cos + partner * sin


def _kvproj_kernel(x_ref, sc_ref, sh_ref, w_ref, cos_ref, sin_ref, kc_ref, ka_ref):
    hb = _modulated_bf16(x_ref, sc_ref, sh_ref)
    cos, sin = cos_ref[0], sin_ref[0]
    dh = NSA_HEAD_DIM
    heads_per_slab = LANES // dh
    n_slab = w_ref.shape[-1] // LANES
    slabs_per_branch = NSA_KV_HEADS // heads_per_slab
    for j in range(n_slab):
        slab = _dot(hb, w_ref[:, j * LANES:(j + 1) * LANES])
        branch = j // slabs_per_branch
        if branch % 2 == 0:
            slab = _rope_slab(slab, cos, sin)
        for u in range(heads_per_slab):
            head = (j % slabs_per_branch) * heads_per_slab + u
            piece = slab[:, u * dh:(u + 1) * dh]
            if branch < 2:
                kc_ref[0, branch * NSA_KV_HEADS + head] = piece
            else:
                ka_ref[0, (branch - 2) * NSA_KV_HEADS + head] = piece.astype(BF16)


def _kvproj(x, sc, sh, kv_w, cos, sin, tm=512):
    B, S, D = x.shape
    N = kv_w.shape[-1]
    n_cmp_arr = 2 * NSA_KV_HEADS
    n_att_arr = 4 * NSA_KV_HEADS
    return pl.pallas_call(
        _kvproj_kernel,
        grid=(B, S // tm),
        in_specs=[_row_spec(tm, D), _batch_vec_spec(D), _batch_vec_spec(D), _const_spec((D, N)),
                  _row_spec(tm, LANES), _row_spec(tm, LANES)],
        out_specs=[pl.BlockSpec((1, n_cmp_arr, tm, NSA_HEAD_DIM), lambda b, s: (b, 0, s, 0)),
                   pl.BlockSpec((1, n_att_arr, tm, NSA_HEAD_DIM), lambda b, s: (b, 0, s, 0))],
        out_shape=[jax.ShapeDtypeStruct((B, n_cmp_arr, S, NSA_HEAD_DIM), F32),
                   jax.ShapeDtypeStruct((B, n_att_arr, S, NSA_HEAD_DIM), BF16)],
        compiler_params=_cparams(("parallel", "parallel")),
        name="nsa_kv_proj",
    )(x, sc, sh, kv_w.astype(BF16), cos, sin)


def _compress_kernel(k_ref, pos_ref, w1_ref, w2_ref, o_ref):
    ks = k_ref[0, 0]
    n_sub, half_in = ks.shape
    top = _dot((ks + pos_ref[0:1, :]).astype(BF16), w1_ref[0, :half_in, :])
    bot = _dot((ks + pos_ref[1:2, :]).astype(BF16), w1_ref[0, half_in:, :])
    hid = jax.nn.gelu(top + pltpu.roll(bot, n_sub - 1, 0))
    o_ref[0, 0] = _dot(hid.astype(BF16), w2_ref[0]).astype(o_ref.dtype)


def _compress(kc, cmp_pos, w1s, w2s):
    B, A, S, Dh = kc.shape
    n_sub = S // CMP_STRIDE
    half_in = CMP_STRIDE * Dh
    hid = w1s.shape[-1]
    ksub = kc.reshape(B, A, n_sub, half_in)
    pos2 = cmp_pos.reshape(2, half_in)
    return pl.pallas_call(
        _compress_kernel,
        grid=(B, A),
        in_specs=[pl.BlockSpec((1, 1, n_sub, half_in), lambda b, a: (b, a, 0, 0)),
                  pl.BlockSpec((2, half_in), lambda b, a: (0, 0)),
                  pl.BlockSpec((1, 2 * half_in, hid), lambda b, a: (a // NSA_KV_HEADS, 0, 0)),
                  pl.BlockSpec((1, hid, Dh), lambda b, a: (a // NSA_KV_HEADS, 0, 0))],
        out_specs=pl.BlockSpec((1, 1, n_sub, Dh), lambda b, a: (b, a, 0, 0)),
        out_shape=jax.ShapeDtypeStruct((B, A, n_sub, Dh), BF16),
        compiler_params=_cparams(("parallel", "parallel")),
        name="nsa_compress",
    )(ksub, pos2, w1s.astype(BF16), w2s.astype(BF16))


def _qproj_kernel(x_ref, sc_ref, sh_ref, wq_ref, wg_ref, cos_ref, sin_ref, q_ref, g_ref):
    hb = _modulated_bf16(x_ref, sc_ref, sh_ref)
    cos, sin = cos_ref[0], sin_ref[0]
    dh = NSA_HEAD_DIM
    heads_per_slab = LANES // dh
    for j in range(wq_ref.shape[-1] // LANES):
        slab = _rope_slab(_dot(hb, wq_ref[:, j * LANES:(j + 1) * LANES]), cos, sin) * (ATTN_SCALE * LOG2_E)
        for u in range(heads_per_slab):
            q_ref[0, j * heads_per_slab + u] = slab[:, u * dh:(u + 1) * dh].astype(BF16)
    g_ref[0] = jax.nn.sigmoid(_dot(hb, wg_ref[...]))


def _qproj(x, sc, sh, q_w, q_dim, cos, sin, tm=512):
    B, S, D = x.shape
    heads = q_dim // NSA_HEAD_DIM
    n_gate = q_w.shape[-1] - q_dim
    wq = q_w[:, :q_dim].astype(BF16)
    wg = jnp.pad(q_w[:, q_dim:], ((0, 0), (0, LANES - n_gate))).astype(BF16)
    return pl.pallas_call(
        _qproj_kernel,
        grid=(B, S // tm),
        in_specs=[_row_spec(tm, D), _batch_vec_spec(D), _batch_vec_spec(D), _const_spec((D, q_dim)),
                  _const_spec((D, LANES)), _row_spec(tm, LANES), _row_spec(tm, LANES)],
        out_specs=[pl.BlockSpec((1, heads, tm, NSA_HEAD_DIM), lambda b, s: (b, 0, s, 0)), _row_spec(tm, LANES)],
        out_shape=[jax.ShapeDtypeStruct((B, heads, S, NSA_HEAD_DIM), BF16),
                   jax.ShapeDtypeStruct((B, S, LANES), F32)],
        compiler_params=_cparams(("parallel", "parallel")),
        name="nsa_q_proj",
    )(x, sc, sh, wq, wg, cos, sin)


def _store_heads_as_rows(o_t, n_rep, o_ref):
    dh = o_t.shape[0]
    tq = o_t.shape[1] // n_rep
    for r in range(0, n_rep, 2):
        pair = jnp.concatenate([o_t[:, r * tq:(r + 1) * tq], o_t[:, (r + 1) * tq:(r + 2) * tq]], axis=0)
        o_ref[0, :, r * dh:(r + 2) * dh] = pair.T


def _cmp_select_kernel(q_ref, k_ref, v_ref, ovt_ref, o_ref, selt_ref):
    i = pl.program_id(2)
    n_rep, tq, dh = q_ref.shape[1:]
    n_cmp = k_ref.shape[2]
    n_slc = ovt_ref.shape[0]
    q = q_ref[0].reshape(n_rep * tq, dh)
    t = i * tq + lax.broadcasted_iota(jnp.int32, (1, tq), 1)
    blk_end = lax.broadcasted_iota(jnp.int32, (n_cmp, 1), 0) * CMP_STRIDE + (CMP_BLOCK - 1)
    mask = jnp.concatenate([blk_end <= t] * n_rep, axis=1)
    s = lax.dot_general(k_ref[0, 0], q, NT_DIMS, preferred_element_type=F32)
    s = jnp.where(mask, s, NEG_BIG)
    m = jnp.max(s, axis=0, keepdims=True)
    p = jnp.where(mask, jnp.exp2(s - m), 0.0)
    p = p / jnp.maximum(jnp.sum(p, axis=0, keepdims=True), F32_TINY)
    pb = p.astype(BF16)
    _store_heads_as_rows(lax.dot_general(v_ref[0, 0], pb, TN_DIMS, preferred_element_type=F32), n_rep, o_ref)
    imp = _dot(ovt_ref[...], pb[:, 0:tq])
    for r in range(1, n_rep):
        imp = imp + _dot(ovt_ref[...], pb[:, r * tq:(r + 1) * tq])
    t_blk = t // SLC_BLOCK
    j = lax.broadcasted_iota(jnp.int32, (n_slc, 1), 0)
    forced = (j == 0) | (j == t_blk) | (j == t_blk - 1)
    valid = j <= t_blk
    score = jnp.where(valid, jnp.where(forced, FORCED_SCORE, imp), -jnp.inf)
    rank = jnp.zeros((n_slc, tq), jnp.int32)
    for jp in range(n_slc):
        c = score[jp:jp + 1, :]
        beats = (c > score) | ((c == score) & (jp < j))
        rank = rank + beats.astype(jnp.int32)
    n_sel = min(N_SELECT, n_slc)
    selt_ref[0, 0] = jnp.where((rank < n_sel) & valid, 1.0, 0.0).astype(selt_ref.dtype)


def _cmp_select(q, kvc, n_rep, tq=256):
    B, H, S, Dh = q.shape
    G = H // n_rep
    n_cmp = kvc.shape[2]
    n_slc = S // SLC_BLOCK
    cs = jnp.arange(n_cmp)[None, :] * CMP_STRIDE
    js = jnp.arange(n_slc)[:, None] * SLC_BLOCK
    ovt = jnp.maximum(jnp.minimum(cs + CMP_BLOCK, js + SLC_BLOCK) - jnp.maximum(cs, js), 0).astype(F32) / CMP_BLOCK
    return pl.pallas_call(
        _cmp_select_kernel,
        grid=(B, G, S // tq),
        in_specs=[pl.BlockSpec((1, n_rep, tq, Dh), lambda b, g, i: (b, g, i, 0)),
                  pl.BlockSpec((1, 1, n_cmp, Dh), lambda b, g, i: (b, g, 0, 0)),
                  pl.BlockSpec((1, 1, n_cmp, Dh), lambda b, g, i: (b, G + g, 0, 0)),
                  pl.BlockSpec((n_slc, n_cmp), lambda b, g, i: (0, 0))],
        out_specs=[pl.BlockSpec((1, tq, n_rep * Dh), lambda b, g, i: (b, i, g)),
                   pl.BlockSpec((1, 1, n_slc, tq), lambda b, g, i: (b, g, 0, i))],
        out_shape=[jax.ShapeDtypeStruct((B, S, H * Dh), F32), jax.ShapeDtypeStruct((B, G, n_slc, S), BF16)],
        compiler_params=_cparams(("parallel", "parallel", "parallel")),
        name="nsa_cmp_select",
    )(q, kvc, kvc, ovt.astype(BF16))


def _slc_attn_kernel(q_ref, selt_ref, e_ref, k_ref, v_ref, o_ref, m_s, l_s, acc_s, *, tk):
    i = pl.program_id(2)
    n_rep, tq, dh = q_ref.shape[1:]
    q = q_ref[0].reshape(n_rep * tq, dh)
    m_s[...] = jnp.full(m_s.shape, NEG_BIG, F32)
    l_s[...] = jnp.zeros(l_s.shape, F32)
    acc_s[...] = jnp.zeros(acc_s.shape, F32)
    t = i * tq + lax.broadcasted_iota(jnp.int32, (1, tq), 1)
    selt = selt_ref[0, 0]

    def chunk(k0, width):
        kpos = k0 + lax.broadcasted_iota(jnp.int32, (width, 1), 0)
        hit = _dot(e_ref[pl.ds(k0, width), :], selt)
        bias = jnp.where((kpos <= t) & (hit > 0.5), 0.0, NEG_BIG)
        s = lax.dot_general(k_ref[0, 0, pl.ds(k0, width), :], q, NT_DIMS, preferred_element_type=F32)
        s = s + jnp.concatenate([bias] * n_rep, axis=1)
        m_old = m_s[...]
        m_new = jnp.maximum(m_old, jnp.max(s, axis=0, keepdims=True))
        p = jnp.exp2(s - m_new)
        alpha = jnp.exp2(m_old - m_new)
        l_s[...] = alpha * l_s[...] + jnp.sum(p, axis=0, keepdims=True)
        pv = lax.dot_general(v_ref[0, 0, pl.ds(k0, width), :], p.astype(BF16), TN_DIMS, preferred_element_type=F32)
        acc_s[...] = alpha * acc_s[...] + pv
        m_s[...] = m_new

    def body(c, carry):
        chunk(pl.multiple_of(c * tk, tk), tk)
        return carry

    n_keys = (i + 1) * tq
    n_full = n_keys // tk
    rem = n_keys - n_full * tk
    lax.fori_loop(0, n_full, body, 0)
    tail0 = pl.multiple_of(n_full * tk, tk)

    @pl.when(rem > tk // 2)
    def _():
        chunk(tail0, tk)

    @pl.when((rem > 0) & (rem <= tk // 2))
    def _():
        chunk(tail0, tk // 2)

    _store_heads_as_rows(acc_s[...] / jnp.maximum(l_s[...], F32_TINY), n_rep, o_ref)


def _win_attn_kernel(q_ref, k_ref, v_ref, o_ref, *, span):
    i = pl.program_id(2)
    n_rep, tq, dh = q_ref.shape[1:]
    q = q_ref[0].reshape(n_rep * tq, dh)
    t = i * tq + lax.broadcasted_iota(jnp.int32, (1, tq), 1)
    k0 = pl.multiple_of(jnp.maximum(i * tq - WINDOW, 0), tq)
    kpos = k0 + lax.broadcasted_iota(jnp.int32, (span, 1), 0)
    bias = jnp.where((kpos <= t) & (t - kpos < WINDOW), 0.0, NEG_BIG)
    s = lax.dot_general(k_ref[0, 0, pl.ds(k0, span), :], q, NT_DIMS, preferred_element_type=F32)
    s = s + jnp.concatenate([bias] * n_rep, axis=1)
    p = jnp.exp2(s - jnp.max(s, axis=0, keepdims=True))
    l = jnp.sum(p, axis=0, keepdims=True)
    pv = lax.dot_general(v_ref[0, 0, pl.ds(k0, span), :], p.astype(BF16), TN_DIMS, preferred_element_type=F32)
    _store_heads_as_rows(pv / jnp.maximum(l, F32_TINY), n_rep, o_ref)


def _attention(q, kv, k_off, v_off, n_rep, selt=None, tq=128, tk=1024):
    B, H, S, Dh = q.shape
    G = H // n_rep
    q_spec = pl.BlockSpec((1, n_rep, tq, Dh), lambda b, g, i: (b, g, i, 0))
    kv_specs = [pl.BlockSpec((1, 1, S, Dh), lambda b, g, i: (b, k_off + g, 0, 0)),
                pl.BlockSpec((1, 1, S, Dh), lambda b, g, i: (b, v_off + g, 0, 0))]
    common = dict(
        grid=(B, G, S // tq),
        out_specs=pl.BlockSpec((1, tq, n_rep * Dh), lambda b, g, i: (b, i, g)),
        out_shape=jax.ShapeDtypeStruct((B, S, H * Dh), F32),
        compiler_params=_cparams(("parallel", "parallel", "arbitrary")),
    )
    if selt is None:
        span = WINDOW + tq
        assert span <= S
        return pl.pallas_call(functools.partial(_win_attn_kernel, span=span), in_specs=[q_spec] + kv_specs,
                              name="nsa_attn_win", **common)(q, kv, kv)
    n_slc = selt.shape[2]
    tk = min(tk, S)
    e = (jnp.arange(S, dtype=jnp.int32)[:, None] // SLC_BLOCK == jnp.arange(n_slc, dtype=jnp.int32)[None, :]).astype(BF16)
    return pl.pallas_call(
        functools.partial(_slc_attn_kernel, tk=tk),
        in_specs=[q_spec, pl.BlockSpec((1, 1, n_slc, tq), lambda b, g, i: (b, g, 0, i)),
                  pl.BlockSpec((S, n_slc), lambda b, g, i: (0, 0))] + kv_specs,
        scratch_shapes=[pltpu.VMEM((1, n_rep * tq), F32), pltpu.VMEM((1, n_rep * tq), F32),
                        pltpu.VMEM((Dh, n_rep * tq), F32)],
        name="nsa_attn_slc", **common)(q, selt, e, kv, kv)


def _gate_oproj_norm_kernel(oc_ref, os_ref, ow_ref, gate_ref, ex_ref, w_ref, x_ref, gt_ref, g_ref, b_ref, out_ref):
    g_parts = _split_bf16(gate_ref[0], 2)
    o = None
    for br, o_ref in enumerate((oc_ref, os_ref, ow_ref)):
        ge = _dot(g_parts[0], ex_ref[br]) + _dot(g_parts[1], ex_ref[br])
        o = ge * o_ref[0] if o is None else o + ge * o_ref[0]
    y = _dot(o.astype(BF16), w_ref[...])
    v = ALPHA * x_ref[0] + (1.0 + gt_ref[0]) * y
    out_ref[0] = _layer_norm_rows(v, g_ref[...], b_ref[...])


def _gate_oproj_norm(o_cmp, o_slc, o_win, gates, o_w, x, gate_t, ln_g, ln_b, tm=256):
    B, S, D = x.shape
    QD = o_w.shape[0]
    lane = jnp.arange(LANES)[:, None]
    head = jnp.arange(QD)[None, :] // NSA_HEAD_DIM
    expand = jnp.stack([(lane == head * NSA_N_BRANCH + br) for br in range(NSA_N_BRANCH)]).astype(BF16)
    return pl.pallas_call(
        _gate_oproj_norm_kernel,
        grid=(B, S // tm),
        in_specs=[_row_spec(tm, QD), _row_spec(tm, QD), _row_spec(tm, QD), _row_spec(tm, LANES),
                  _const_spec((NSA_N_BRANCH, LANES, QD)), _const_spec((QD, D)), _row_spec(tm, D),
                  _batch_vec_spec(D), _const_spec((1, D)), _const_spec((1, D))],
        out_specs=_row_spec(tm, D),
        out_shape=jax.ShapeDtypeStruct((B, S, D), F32),
        compiler_params=_cparams(("parallel", "parallel")),
        name="nsa_gate_oproj_norm",
    )(o_cmp, o_slc, o_win, gates, expand, o_w.astype(BF16), x, gate_t, ln_g.reshape(1, D), ln_b.reshape(1, D))


def _nsa_shared_kv(x, sc, sh, kv_w, cmp_pos, phi_k_w1, phi_k_w2, phi_v_w1, phi_v_w2, cos, sin):
    kc, ka = _kvproj(x, sc, sh, kv_w, cos, sin)
    kvc = _compress(kc, cmp_pos, jnp.stack([phi_k_w1, phi_v_w1]), jnp.stack([phi_k_w2, phi_v_w2]))
    return kvc, ka


def _nsa_layer(x, sh, sc, gate_t, q_w, o_w, kvc, ka, cos, sin, ln_g, ln_b):
    G = NSA_KV_HEADS
    q_dim = o_w.shape[0]
    n_rep = q_dim // NSA_HEAD_DIM // G
    q, gates = _qproj(x, sc, sh, q_w, q_dim, cos, sin)
    o_cmp, selt = _cmp_select(q, kvc, n_rep)
    o_slc = _attention(q, ka, 0, G, n_rep, selt=selt)
    o_win = _attention(q, ka, 2 * G, 3 * G, n_rep)
    return _gate_oproj_norm(o_cmp, o_slc, o_win, gates, o_w, x, gate_t, ln_g, ln_b)


def kernel(x, c, pos, ada_w, ada_b, ln_g, ln_b, ssm_in_w, ssm_conv_w, ssm_conv_b, ssm_dt_bias, ssm_a_log, ssm_d, ssm_norm_w, ssm_out_w, kv_ada_w, kv_ada_b, kv_w, cmp_pos, phi_k_w1, phi_k_w2, phi_v_w1, phi_v_w2, nsa_q_w, nsa_o_w, router_w, router_b, moe_w_up, moe_b_up, moe_w_down, moe_b_down):
    c_act = jax.nn.silu(c)
    cos, sin = _rope_tables(pos)
    shared = None
    for i in range(DEPTH):
        mod = c_act @ ada_w[i] + ada_b[i]
        sh_t, sc_t, g_t, sh_c, sc_c, g_c = [m[:, None, :] for m in jnp.split(mod, 6, axis=-1)]
        if i < N_A:
            x = _mamba_layer(x, sh_t, sc_t, g_t, ssm_in_w[i], ssm_conv_w[i], ssm_conv_b[i], ssm_dt_bias[i],
                             ssm_a_log[i], ssm_d[i], ssm_norm_w[i], ssm_out_w[i], ln_g[i, 0], ln_b[i, 0])
        else:
            j = i - N_A
            x = _nsa_layer(x, sh_t, sc_t, g_t, nsa_q_w[j], nsa_o_w[j], *shared, cos, sin, ln_g[i, 0], ln_b[i, 0])
        x = _moe_layer(x, sh_c, sc_c, g_c, i, router_w[i], router_b[i],
                       moe_w_up, moe_b_up, moe_w_down, moe_b_down, ln_g[i, 1], ln_b[i, 1])
        if i == N_A - 1:
            kv_sh, kv_sc = [m[:, None, :] for m in jnp.split(c_act @ kv_ada_w + kv_ada_b, 2, axis=-1)]
            shared = _nsa_shared_kv(x, kv_sc, kv_sh, kv_w, cmp_pos, phi_k_w1, phi_k_w2, phi_v_w1, phi_v_w2, cos, sin)
    return x
```

```python
import functools

import jax
import jax.numpy as jnp
import numpy as np
from jax import lax
from jax.experimental import pallas as pl
from jax.experimental.pallas import tpu as pltpu

F32 = jnp.float32
BF16 = jnp.bfloat16

DEPTH = 4
N_A = DEPTH // 2
ALPHA = (2.0 * DEPTH) ** 0.25
LN_EPS = 1e-5

SSM_HEAD_DIM = 64
SSM_GROUPS = 4
SSM_STATE = 128
CONV_WIDTH = 4
SSM_CHUNK = 128

NSA_HEAD_DIM = 64
NSA_KV_HEADS = 4
NSA_N_BRANCH = 3
CMP_BLOCK = 32
CMP_STRIDE = 16
SLC_BLOCK = 64
N_SELECT = 16
WINDOW = 512
ROPE_THETA = 10000.0
ATTN_SCALE = NSA_HEAD_DIM ** -0.5
FORCED_SCORE = 1e9

TOP_K = 4
SWIGLU_LIMIT = 7.0
SWIGLU_ALPHA = 1.702

VMEM_LIMIT_BYTES_V7X = 56 * 1024 * 1024
LANES = 128
SUBLANES = 8
NEG_BIG = -1e30
F32_TINY = float(np.finfo(np.float32).tiny)
LOG2_E = float(np.log2(np.e))
NT_DIMS = (((1,), (1,)), ((), ()))
TN_DIMS = (((0,), (0,)), ((), ()))


def _cparams(sem):
    return pltpu.CompilerParams(dimension_semantics=sem, vmem_limit_bytes=VMEM_LIMIT_BYTES_V7X)


def _dot(a, b):
    return jnp.dot(a, b, preferred_element_type=F32)


def _split_bf16(a, parts):
    out = []
    r = a
    for _ in range(parts):
        p = r.astype(BF16)
        out.append(p)
        r = r - p.astype(F32)
    return out


def _layer_norm_rows(v, g, b):
    mu = jnp.mean(v, axis=-1, keepdims=True)
    d = v - mu
    var = jnp.mean(d * d, axis=-1, keepdims=True)
    return d * lax.rsqrt(var + LN_EPS) * g + b


def _modulated_bf16(x_ref, sc_ref, sh_ref):
    return (x_ref[0] * (1.0 + sc_ref[0]) + sh_ref[0]).astype(BF16)


def _row_spec(tm, width):
    return pl.BlockSpec((1, tm, width), lambda b, s: (b, s, 0))


def _batch_vec_spec(width):
    return pl.BlockSpec((1, 1, width), lambda b, s: (b, 0, 0))


def _const_spec(shape):
    nd = len(shape)
    return pl.BlockSpec(shape, lambda b, s: (0,) * nd)


def _router_kernel(x_ref, sc_ref, sh_ref, rw_ref, rb_ref, h_ref, idx_ref, gate_ref, cnt_ref, running):
    first_step = (pl.program_id(0) == 0) & (pl.program_id(1) == 0)

    @pl.when(first_step)
    def _():
        running[...] = jnp.zeros(running.shape, F32)

    hb = _modulated_bf16(x_ref, sc_ref, sh_ref)
    h_ref[0] = hb
    logits = _dot(hb, rw_ref[...]) + rb_ref[...]
    tm, n_exp = logits.shape
    lane_e = lax.broadcasted_iota(jnp.int32, logits.shape, 1)
    lane_o = lax.broadcasted_iota(jnp.int32, idx_ref.shape[1:], 1)
    vals, idxs = [], []
    l = logits
    for _ in range(TOP_K):
        m = jnp.max(l, axis=-1, keepdims=True)
        idx = jnp.min(jnp.where(l == m, lane_e, n_exp), axis=-1, keepdims=True)
        vals.append(m)
        idxs.append(idx)
        l = jnp.where(lane_e == idx, -jnp.inf, l)
    es = [jnp.exp(v - vals[0]) for v in vals]
    denom = es[0] + es[1] + es[2] + es[3]
    idx_out = jnp.zeros(idx_ref.shape[1:], jnp.int32)
    gate_out = jnp.zeros(gate_ref.shape[1:], F32)
    chosen = sum(jnp.where(lane_e == ix, 1.0, 0.0) for ix in idxs)
    row = lax.broadcasted_iota(jnp.int32, (tm, tm), 0)
    col = lax.broadcasted_iota(jnp.int32, (tm, tm), 1)
    earlier = jnp.where(col < row, 1.0, 0.0).astype(BF16)
    before = _dot(earlier, chosen.astype(BF16)) + running[...]
    for k in range(TOP_K):
        rank_k = jnp.sum(jnp.where(lane_e == idxs[k], before, 0.0), axis=-1, keepdims=True).astype(jnp.int32)
        idx_out = jnp.where(lane_o == k, idxs[k], idx_out)
        idx_out = jnp.where(lane_o == TOP_K + k, rank_k, idx_out)
        gate_out = jnp.where(lane_o == k, es[k] / denom, gate_out)
    idx_ref[0] = idx_out
    gate_ref[0] = gate_out
    running[...] = running[...] + jnp.sum(chosen, axis=0, keepdims=True)
    cnt_ref[...] = jnp.broadcast_to(running[...], cnt_ref.shape)


def _router(x, sc, sh, rw, rb, tm=512):
    B, S, D = x.shape
    E = rw.shape[-1]
    return pl.pallas_call(
        _router_kernel,
        grid=(B, S // tm),
        in_specs=[_row_spec(tm, D), _batch_vec_spec(D), _batch_vec_spec(D), _const_spec((D, E)), _const_spec((1, E))],
        out_specs=[_row_spec(tm, D), _row_spec(tm, LANES), _row_spec(tm, LANES), _const_spec((SUBLANES, E))],
        out_shape=[
            jax.ShapeDtypeStruct((B, S, D), BF16),
            jax.ShapeDtypeStruct((B, S, LANES), jnp.int32),
            jax.ShapeDtypeStruct((B, S, LANES), F32),
            jax.ShapeDtypeStruct((SUBLANES, E), F32),
        ],
        scratch_shapes=[pltpu.VMEM((1, E), F32)],
        compiler_params=_cparams(("arbitrary", "arbitrary")),
        name="moe_router",
    )(x, sc, sh, rw.astype(BF16), rb.reshape(1, E))


def _ffn_kernel(blk_e_ref, first_ref, nused_ref, x_ref, wu_ref, bu_ref, wd_ref, bd_ref, o_ref, wu_s, wd_s):
    i = pl.program_id(0)
    d_exp = wd_s.shape[0]

    @pl.when(i < nused_ref[0])
    def _():
        @pl.when(first_ref[i] == 1)
        def _():
            wu_s[...] = wu_ref[0].astype(BF16)
            wd_s[...] = wd_ref[0].astype(BF16)

        u = _dot(x_ref[...], wu_s[...]) + bu_ref[0]
        glu = jnp.minimum(u[:, :d_exp], SWIGLU_LIMIT)
        lin = jnp.clip(u[:, d_exp:], -SWIGLU_LIMIT, SWIGLU_LIMIT)
        act = glu * jax.nn.sigmoid(SWIGLU_ALPHA * glu) * (lin + 1.0)
        o_ref[...] = (_dot(act.astype(BF16), wd_s[...]) + bd_ref[0]).astype(o_ref.dtype)

    @pl.when(i >= nused_ref[0])
    def _():
        o_ref[...] = jnp.zeros_like(o_ref)


def _expert_ffn(xg, blk_e, first, nused, layer, w_up, b_up, w_down, b_down, tm):
    R, D = xg.shape
    n_layers, E, _, N2 = w_up.shape
    d_exp = w_down.shape[2]
    n_blk = R // tm
    grid_spec = pltpu.PrefetchScalarGridSpec(
        num_scalar_prefetch=3,
        grid=(n_blk,),
        in_specs=[
            pl.BlockSpec((tm, D), lambda i, be, fi, nu: (i, 0)),
            pl.BlockSpec((None, 1, D, N2), lambda i, be, fi, nu: (layer, be[i], 0, 0)),
            pl.BlockSpec((None, 1, 1, N2), lambda i, be, fi, nu: (layer, be[i], 0, 0)),
            pl.BlockSpec((None, 1, d_exp, D), lambda i, be, fi, nu: (layer, be[i], 0, 0)),
            pl.BlockSpec((None, 1, 1, D), lambda i, be, fi, nu: (layer, be[i], 0, 0)),
        ],
        out_specs=pl.BlockSpec((tm, D), lambda i, be, fi, nu: (i, 0)),
        scratch_shapes=[pltpu.VMEM((D, N2), BF16), pltpu.VMEM((d_exp, D), BF16)],
    )
    return pl.pallas_call(
        _ffn_kernel,
        grid_spec=grid_spec,
        out_shape=jax.ShapeDtypeStruct((R, D), BF16),
        compiler_params=_cparams(("arbitrary",)),
        name="moe_expert_ffn",
    )(blk_e, first, nused, xg, w_up, b_up.reshape(n_layers, E, 1, N2), w_down, b_down.reshape(n_layers, E, 1, D))


def _combine_norm_kernel(x_ref, yg_ref, gate_ref, gc_ref, g_ref, b_ref, o_ref):
    gate = gate_ref[0]
    y = yg_ref[0, 0] * gate[:, 0:1]
    for k in range(1, TOP_K):
        y = y + yg_ref[k, 0] * gate[:, k:k + 1]
    v = ALPHA * x_ref[0] + (1.0 + gc_ref[0]) * y
    o_ref[0] = _layer_norm_rows(v, g_ref[...], b_ref[...])


def _combine_norm(x, yg, gate, gate_c, ln_g, ln_b, tm=256):
    B, S, D = x.shape
    return pl.pallas_call(
        _combine_norm_kernel,
        grid=(B, S // tm),
        in_specs=[
            _row_spec(tm, D),
            pl.BlockSpec((TOP_K, 1, tm, D), lambda b, s: (0, b, s, 0)),
            _row_spec(tm, LANES),
            _batch_vec_spec(D),
            _const_spec((1, D)),
            _const_spec((1, D)),
        ],
        out_specs=_row_spec(tm, D),
        out_shape=jax.ShapeDtypeStruct((B, S, D), F32),
        compiler_params=_cparams(("parallel", "parallel")),
        name="moe_combine_norm",
    )(x, yg, gate, gate_c, ln_g.reshape(1, D), ln_b.reshape(1, D))


def _moe_layer(x, sh, sc, gate_c, layer, router_w, router_b, w_up, b_up, w_down, b_down, ln_g, ln_b, row_block=256):
    B, S, D = x.shape
    T = B * S
    E = router_w.shape[-1]
    h, route, gate, cnt = _router(x, sc, sh, router_w, router_b)
    A = T * TOP_K
    eid = route[:, :, :TOP_K].reshape(A)
    rank = route[:, :, TOP_K:2 * TOP_K].reshape(A)
    counts = cnt[0].astype(jnp.int32)
    padded = (counts + row_block - 1) // row_block * row_block
    off = jnp.cumsum(counts) - counts
    cum_padded = jnp.cumsum(padded)
    poff = cum_padded - padded
    pos = poff[eid] + rank
    R_rows = A + E * row_block
    n_blk = R_rows // row_block
    nused = (cum_padded[-1] // row_block).astype(jnp.int32)
    blk_start = jnp.arange(n_blk, dtype=jnp.int32) * row_block
    blk_e = jnp.minimum(jnp.sum(cum_padded[None, :] <= blk_start[:, None], axis=1), E - 1).astype(jnp.int32)
    last_e = blk_e[jnp.maximum(nused - 1, 0)]
    blk_e = jnp.where(jnp.arange(n_blk) < nused, blk_e, last_e)
    first = jnp.concatenate([jnp.ones((1,), jnp.int32), (blk_e[1:] != blk_e[:-1]).astype(jnp.int32)])
    order = jnp.argsort(eid).astype(jnp.int32)
    row = jnp.arange(R_rows, dtype=jnp.int32)
    per_row = lambda table: jnp.repeat(table[blk_e], row_block)
    j = row - per_row(poff)
    row_tok = jnp.where(j < per_row(counts), order[jnp.minimum(per_row(off) + j, A - 1)] // TOP_K, row % T)
    xg = h.reshape(T, D)[row_tok]
    y = _expert_ffn(xg, blk_e, first, nused.reshape(1), layer, w_up, b_up, w_down, b_down, row_block)
    yg = y[pos.reshape(T, TOP_K).T].reshape(TOP_K, B, S, D)
    return _combine_norm(x, yg, gate, gate_c, ln_g, ln_b)


def _outproj_norm_kernel(y_ref, w_ref, x_ref, gt_ref, g_ref, b_ref, o_ref):
    y = _dot(y_ref[0], w_ref[...])
    v = ALPHA * x_ref[0] + (1.0 + gt_ref[0]) * y
    o_ref[0] = _layer_norm_rows(v, g_ref[...], b_ref[...])


def _outproj_norm(y, w, x, gate_t, ln_g, ln_b, tm=512):
    B, S, D = x.shape
    K = y.shape[-1]
    return pl.pallas_call(
        _outproj_norm_kernel,
        grid=(B, S // tm),
        in_specs=[_row_spec(tm, K), _const_spec((K, D)), _row_spec(tm, D), _batch_vec_spec(D),
                  _const_spec((1, D)), _const_spec((1, D))],
        out_specs=_row_spec(tm, D),
        out_shape=jax.ShapeDtypeStruct((B, S, D), F32),
        compiler_params=_cparams(("parallel", "parallel")),
        name="outproj_norm",
    )(y, w.astype(BF16), x, gate_t, ln_g.reshape(1, D), ln_b.reshape(1, D))


def _ssm_inproj_kernel(x_ref, sc_ref, sh_ref, wz_ref, wx_ref, wdt_ref, cw_ref, cb_ref, dtb_ref,
                       z_ref, xs_ref, bm_ref, cm_ref, dt_ref, xbuf, *, col_chunk):
    tm = x_ref.shape[1]
    d_inner = xs_ref.shape[-1]
    bc = bm_ref.shape[-1]
    conv_dim = wx_ref.shape[-1]
    halo = SUBLANES

    @pl.when(pl.program_id(1) == 0)
    def _():
        xbuf[0:halo, :] = jnp.zeros((halo, conv_dim), F32)

    hb = _modulated_bf16(x_ref, sc_ref, sh_ref)
    z_ref[0] = _dot(hb, wz_ref[...])
    dt_ref[0] = jax.nn.softplus(_dot(hb, wdt_ref[...]) + dtb_ref[...])
    for c0 in range(0, conv_dim, col_chunk):
        cs = slice(c0, c0 + col_chunk)
        xbuf[halo:halo + tm, cs] = _dot(hb, wx_ref[:, cs])
        acc = cb_ref[:, cs] + cw_ref[CONV_WIDTH - 1:CONV_WIDTH, cs] * xbuf[halo:halo + tm, cs]
        for w in range(CONV_WIDTH - 1):
            lo = halo - (CONV_WIDTH - 1) + w
            acc = acc + cw_ref[w:w + 1, cs] * xbuf[lo:lo + tm, cs]
        y = acc * jax.nn.sigmoid(acc)
        if c0 + col_chunk <= d_inner:
            xs_ref[0, :, cs] = y
        elif c0 + col_chunk <= d_inner + bc:
            bm_ref[0, :, c0 - d_inner:c0 - d_inner + col_chunk] = y.astype(BF16)
        else:
            cm_ref[0, :, c0 - d_inner - bc:c0 - d_inner - bc + col_chunk] = y.astype(BF16)
        xbuf[0:halo, cs] = xbuf[tm:tm + halo, cs]


def _ssm_inproj(x, sc, sh, in_w, conv_w, conv_b, dt_bias, d_inner, bc_dim, tm=256, col_chunk=512):
    B, S, D = x.shape
    conv_dim = d_inner + 2 * bc_dim
    heads = dt_bias.shape[0]
    wz = in_w[:, :d_inner].astype(BF16)
    wx = in_w[:, d_inner:d_inner + conv_dim].astype(BF16)
    wdt = jnp.pad(in_w[:, d_inner + conv_dim:], ((0, 0), (0, LANES - heads))).astype(BF16)
    dtb = jnp.pad(dt_bias, (0, LANES - heads)).reshape(1, LANES)
    assert d_inner % col_chunk == 0 and bc_dim % col_chunk == 0
    return pl.pallas_call(
        functools.partial(_ssm_inproj_kernel, col_chunk=col_chunk),
        grid=(B, S // tm),
        in_specs=[_row_spec(tm, D), _batch_vec_spec(D), _batch_vec_spec(D),
                  _const_spec((D, d_inner)), _const_spec((D, conv_dim)), _const_spec((D, LANES)),
                  _const_spec((CONV_WIDTH, conv_dim)), _const_spec((1, conv_dim)), _const_spec((1, LANES))],
        out_specs=[_row_spec(tm, d_inner), _row_spec(tm, d_inner), _row_spec(tm, bc_dim), _row_spec(tm, bc_dim),
                   _row_spec(tm, LANES)],
        out_shape=[
            jax.ShapeDtypeStruct((B, S, d_inner), F32),
            jax.ShapeDtypeStruct((B, S, d_inner), F32),
            jax.ShapeDtypeStruct((B, S, bc_dim), BF16),
            jax.ShapeDtypeStruct((B, S, bc_dim), BF16),
            jax.ShapeDtypeStruct((B, S, LANES), F32),
        ],
        scratch_shapes=[pltpu.VMEM((SUBLANES + tm + SUBLANES, conv_dim), F32)],
        compiler_params=_cparams(("parallel", "arbitrary")),
        name="ssm_inproj_conv",
    )(x, sc, sh, wz, wx, wdt, conv_w, conv_b.reshape(1, conv_dim), dtb)


def _ssd_kernel(xs_ref, bm_ref, cm_ref, z_ref, dt_ref, dtt_ref, aneg_ref, anegt_ref, dskip_ref, nw_ref, ex_ref,
                y_ref, state, *, heads_per_group):
    L = xs_ref.shape[1]
    P = SSM_HEAD_DIM
    N = SSM_STATE
    n_groups = bm_ref.shape[-1] // N
    gw = heads_per_group * P

    @pl.when(pl.program_id(1) == 0)
    def _():
        state[...] = jnp.zeros(state.shape, F32)

    row = lax.broadcasted_iota(jnp.int32, (L, L), 0)
    col = lax.broadcasted_iota(jnp.int32, (L, L), 1)
    lower = row >= col
    tri = jnp.where(lower, 1.0, 0.0).astype(BF16)
    tri_t = jnp.where(row <= col, 1.0, 0.0).astype(BF16)
    dt = dt_ref[0]
    a_cs = sum(_dot(tri, p) for p in _split_bf16(dt * aneg_ref[...], 3))
    a_cs_t = sum(_dot(p, tri_t) for p in _split_bf16(dtt_ref[0] * anegt_ref[...], 3))
    dt_parts = _split_bf16(dt, 2)
    acs_parts = _split_bf16(a_cs, 2)
    lane = lax.broadcasted_iota(jnp.int32, (1, 2 * P), 1)
    first_head = lane < P
    for g in range(n_groups):
        gs = slice(g * gw, (g + 1) * gw)
        ex = ex_ref[g]
        dt_x = _dot(dt_parts[0], ex) + _dot(dt_parts[1], ex)
        acs_x = _dot(acs_parts[0], ex) + _dot(acs_parts[1], ex)
        end_x = acs_x[L - 1:L, :]
        bg = bm_ref[0, :, g * N:(g + 1) * N]
        cg = cm_ref[0, :, g * N:(g + 1) * N]
        xs_g = xs_ref[0, :, gs]
        xdt = xs_g * dt_x
        xdt_b = xdt.astype(BF16)
        s_g = state[g]
        y = _dot(cg, s_g.astype(BF16)) * jnp.exp(acs_x) + xs_g * dskip_ref[:, gs]
        to_end = (xdt * jnp.exp(end_x - acs_x)).astype(BF16)
        state[g] = s_g * jnp.exp(end_x) + lax.dot_general(bg, to_end, TN_DIMS, preferred_element_type=F32)
        cb = lax.dot_general(cg, bg, NT_DIMS, preferred_element_type=F32)
        diag = []
        for u in range(heads_per_group // 2):
            ms = []
            for h in (g * heads_per_group + 2 * u, g * heads_per_group + 2 * u + 1):
                seg = a_cs[:, h:h + 1] - a_cs_t[h:h + 1, :]
                ms.append((cb * jnp.exp(jnp.where(lower, seg, NEG_BIG))).astype(BF16))
            xp = xdt_b[:, 2 * u * P:(2 * u + 2) * P]
            zero = jnp.zeros_like(xp)
            rhs = jnp.concatenate([jnp.where(first_head, xp, zero), jnp.where(first_head, zero, xp)], axis=0)
            diag.append(_dot(jnp.concatenate(ms, axis=1), rhs))
        y = y + jnp.concatenate(diag, axis=1)
        zz = z_ref[0, :, gs]
        y = y * (zz * jax.nn.sigmoid(zz))
        scale = lax.rsqrt(jnp.sum(y * y, axis=-1, keepdims=True) / gw + LN_EPS)
        y_ref[0, :, gs] = (y * scale * nw_ref[:, gs]).astype(BF16)


def _ssd(xs, bm, cm, z, dt, a_log, d_skip, norm_w):
    B, S, d_inner = xs.shape
    heads = a_log.shape[0]
    bc_dim = bm.shape[-1]
    L = SSM_CHUNK
    a_neg = -jnp.exp(a_log)
    dt_t = jnp.swapaxes(dt[:, :, :heads], 1, 2)
    aneg = jnp.pad(a_neg, (0, LANES - heads)).reshape(1, LANES)
    anegt = jnp.broadcast_to(a_neg[:, None], (heads, L))
    dskip = jnp.repeat(d_skip, SSM_HEAD_DIM).reshape(1, d_inner)
    hpg = heads // SSM_GROUPS
    gw = hpg * SSM_HEAD_DIM
    head_of_lane = jnp.arange(SSM_GROUPS)[:, None, None] * hpg + jnp.arange(gw)[None, None, :] // SSM_HEAD_DIM
    expand = (jnp.arange(LANES)[None, :, None] == head_of_lane).astype(BF16)
    return pl.pallas_call(
        functools.partial(_ssd_kernel, heads_per_group=hpg),
        grid=(B, S // L),
        in_specs=[_row_spec(L, d_inner), _row_spec(L, bc_dim), _row_spec(L, bc_dim), _row_spec(L, d_inner),
                  _row_spec(L, LANES), pl.BlockSpec((1, heads, L), lambda b, c: (b, 0, c)),
                  _const_spec((1, LANES)), _const_spec((heads, L)), _const_spec((1, d_inner)), _const_spec((1, d_inner)),
                  _const_spec((SSM_GROUPS, LANES, gw))],
        out_specs=_row_spec(L, d_inner),
        out_shape=jax.ShapeDtypeStruct((B, S, d_inner), BF16),
        scratch_shapes=[pltpu.VMEM((SSM_GROUPS, SSM_STATE, gw), F32)],
        compiler_params=_cparams(("parallel", "arbitrary")),
        name="ssd_chunk_scan",
    )(xs, bm, cm, z, dt, dt_t, aneg, anegt, dskip, norm_w.reshape(1, d_inner), expand)


def _mamba_layer(x, sh, sc, gate_t, in_w, conv_w, conv_b, dt_bias, a_log, d_skip, norm_w, out_w, ln_g, ln_b):
    d_inner = out_w.shape[0]
    bc_dim = SSM_GROUPS * SSM_STATE
    z, xs, bm, cm, dt = _ssm_inproj(x, sc, sh, in_w, conv_w, conv_b, dt_bias, d_inner, bc_dim)
    y = _ssd(xs, bm, cm, z, dt, a_log, d_skip, norm_w)
    return _outproj_norm(y, out_w, x, gate_t, ln_g, ln_b)


def _rope_tables(pos):
    half = NSA_HEAD_DIM // 2
    inv = ROPE_THETA ** (-jnp.arange(half, dtype=F32) / half)
    ang = pos.astype(F32)[..., None] * inv
    cos, sin = jnp.cos(ang), jnp.sin(ang)
    return jnp.concatenate([cos, cos, cos, cos], axis=-1), jnp.concatenate([-sin, sin, -sin, sin], axis=-1)


def _rope_slab(x, cos, sin):
    half = NSA_HEAD_DIM // 2
    lane = lax.broadcasted_iota(jnp.int32, (1, LANES), 1)
    first_half = (lane % NSA_HEAD_DIM) < half
    partner = jnp.where(first_half, pltpu.roll(x, LANES - half, 1), pltpu.roll(x, half, 1))
    return x * cos + partner * sin


def _kvproj_kernel(x_ref, sc_ref, sh_ref, w_ref, cos_ref, sin_ref, kc_ref, ka_ref):
    hb = _modulated_bf16(x_ref, sc_ref, sh_ref)
    cos, sin = cos_ref[0], sin_ref[0]
    dh = NSA_HEAD_DIM
    heads_per_slab = LANES // dh
    n_slab = w_ref.shape[-1] // LANES
    slabs_per_branch = NSA_KV_HEADS // heads_per_slab
    for j in range(n_slab):
        slab = _dot(hb, w_ref[:, j * LANES:(j + 1) * LANES])
        branch = j // slabs_per_branch
        if branch % 2 == 0:
            slab = _rope_slab(slab, cos, sin)
        for u in range(heads_per_slab):
            head = (j % slabs_per_branch) * heads_per_slab + u
            piece = slab[:, u * dh:(u + 1) * dh]
            if branch < 2:
                kc_ref[0, branch * NSA_KV_HEADS + head] = piece
            else:
                ka_ref[0, (branch - 2) * NSA_KV_HEADS + head] = piece.astype(BF16)


def _kvproj(x, sc, sh, kv_w, cos, sin, tm=512):
    B, S, D = x.shape
    N = kv_w.shape[-1]
    n_cmp_arr = 2 * NSA_KV_HEADS
    n_att_arr = 4 * NSA_KV_HEADS
    return pl.pallas_call(
        _kvproj_kernel,
        grid=(B, S // tm),
        in_specs=[_row_spec(tm, D), _batch_vec_spec(D), _batch_vec_spec(D), _const_spec((D, N)),
                  _row_spec(tm, LANES), _row_spec(tm, LANES)],
        out_specs=[pl.BlockSpec((1, n_cmp_arr, tm, NSA_HEAD_DIM), lambda b, s: (b, 0, s, 0)),
                   pl.BlockSpec((1, n_att_arr, tm, NSA_HEAD_DIM), lambda b, s: (b, 0, s, 0))],
        out_shape=[jax.ShapeDtypeStruct((B, n_cmp_arr, S, NSA_HEAD_DIM), F32),
                   jax.ShapeDtypeStruct((B, n_att_arr, S, NSA_HEAD_DIM), BF16)],
        compiler_params=_cparams(("parallel", "parallel")),
        name="nsa_kv_proj",
    )(x, sc, sh, kv_w.astype(BF16), cos, sin)


def _compress_kernel(k_ref, pos_ref, w1_ref, w2_ref, o_ref):
    ks = k_ref[0, 0]
    n_sub, half_in = ks.shape
    top = _dot((ks + pos_ref[0:1, :]).astype(BF16), w1_ref[0, :half_in, :])
    bot = _dot((ks + pos_ref[1:2, :]).astype(BF16), w1_ref[0, half_in:, :])
    hid = jax.nn.gelu(top + pltpu.roll(bot, n_sub - 1, 0))
    o_ref[0, 0] = _dot(hid.astype(BF16), w2_ref[0]).astype(o_ref.dtype)


def _compress(kc, cmp_pos, w1s, w2s):
    B, A, S, Dh = kc.shape
    n_sub = S // CMP_STRIDE
    half_in = CMP_STRIDE * Dh
    hid = w1s.shape[-1]
    ksub = kc.reshape(B, A, n_sub, half_in)
    pos2 = cmp_pos.reshape(2, half_in)
    return pl.pallas_call(
        _compress_kernel,
        grid=(B, A),
        in_specs=[pl.BlockSpec((1, 1, n_sub, half_in), lambda b, a: (b, a, 0, 0)),
                  pl.BlockSpec((2, half_in), lambda b, a: (0, 0)),
                  pl.BlockSpec((1, 2 * half_in, hid), lambda b, a: (a // NSA_KV_HEADS, 0, 0)),
                  pl.BlockSpec((1, hid, Dh), lambda b, a: (a // NSA_KV_HEADS, 0, 0))],
        out_specs=pl.BlockSpec((1, 1, n_sub, Dh), lambda b, a: (b, a, 0, 0)),
        out_shape=jax.ShapeDtypeStruct((B, A, n_sub, Dh), BF16),
        compiler_params=_cparams(("parallel", "parallel")),
        name="nsa_compress",
    )(ksub, pos2, w1s.astype(BF16), w2s.astype(BF16))


def _qproj_kernel(x_ref, sc_ref, sh_ref, wq_ref, wg_ref, cos_ref, sin_ref, q_ref, g_ref):
    hb = _modulated_bf16(x_ref, sc_ref, sh_ref)
    cos, sin = cos_ref[0], sin_ref[0]
    dh = NSA_HEAD_DIM
    heads_per_slab = LANES // dh
    for j in range(wq_ref.shape[-1] // LANES):
        slab = _rope_slab(_dot(hb, wq_ref[:, j * LANES:(j + 1) * LANES]), cos, sin) * (ATTN_SCALE * LOG2_E)
        for u in range(heads_per_slab):
            q_ref[0, j * heads_per_slab + u] = slab[:, u * dh:(u + 1) * dh].astype(BF16)
    g_ref[0] = jax.nn.sigmoid(_dot(hb, wg_ref[...]))


def _qproj(x, sc, sh, q_w, q_dim, cos, sin, tm=512):
    B, S, D = x.shape
    heads = q_dim // NSA_HEAD_DIM
    n_gate = q_w.shape[-1] - q_dim
    wq = q_w[:, :q_dim].astype(BF16)
    wg = jnp.pad(q_w[:, q_dim:], ((0, 0), (0, LANES - n_gate))).astype(BF16)
    return pl.pallas_call(
        _qproj_kernel,
        grid=(B, S // tm),
        in_specs=[_row_spec(tm, D), _batch_vec_spec(D), _batch_vec_spec(D), _const_spec((D, q_dim)),
                  _const_spec((D, LANES)), _row_spec(tm, LANES), _row_spec(tm, LANES)],
        out_specs=[pl.BlockSpec((1, heads, tm, NSA_HEAD_DIM), lambda b, s: (b, 0, s, 0)), _row_spec(tm, LANES)],
        out_shape=[jax.ShapeDtypeStruct((B, heads, S, NSA_HEAD_DIM), BF16),
                   jax.ShapeDtypeStruct((B, S, LANES), F32)],
        compiler_params=_cparams(("parallel", "parallel")),
        name="nsa_q_proj",
    )(x, sc, sh, wq, wg, cos, sin)


def _store_heads_as_rows(o_t, n_rep, o_ref, row0=0):
    dh = o_t.shape[0]
    tq = o_t.shape[1] // n_rep
    for r in range(0, n_rep, 2):
        pair = jnp.concatenate([o_t[:, r * tq:(r + 1) * tq], o_t[:, (r + 1) * tq:(r + 2) * tq]], axis=0)
        o_ref[0, row0:row0 + tq, r * dh:(r + 2) * dh] = pair.T


def _cmp_select_kernel(q_ref, k_ref, v_ref, ovt_ref, o_ref, selt_ref):
    i = pl.program_id(2)
    n_rep, tq, dh = q_ref.shape[1:]
    n_cmp = k_ref.shape[2]
    n_slc = ovt_ref.shape[0]
    q = q_ref[0].reshape(n_rep * tq, dh)
    t = i * tq + lax.broadcasted_iota(jnp.int32, (1, tq), 1)
    blk_end = lax.broadcasted_iota(jnp.int32, (n_cmp, 1), 0) * CMP_STRIDE + (CMP_BLOCK - 1)
    mask = jnp.concatenate([blk_end <= t] * n_rep, axis=1)
    s = lax.dot_general(k_ref[0, 0], q, NT_DIMS, preferred_element_type=F32)
    s = jnp.where(mask, s, NEG_BIG)
    m = jnp.max(s, axis=0, keepdims=True)
    p = jnp.where(mask, jnp.exp2(s - m), 0.0)
    p = p / jnp.maximum(jnp.sum(p, axis=0, keepdims=True), F32_TINY)
    pb = p.astype(BF16)
    _store_heads_as_rows(lax.dot_general(v_ref[0, 0], pb, TN_DIMS, preferred_element_type=F32), n_rep, o_ref)
    imp = _dot(ovt_ref[...], pb[:, 0:tq])
    for r in range(1, n_rep):
        imp = imp + _dot(ovt_ref[...], pb[:, r * tq:(r + 1) * tq])
    t_blk = t // SLC_BLOCK
    j = lax.broadcasted_iota(jnp.int32, (n_slc, 1), 0)
    forced = (j == 0) | (j == t_blk) | (j == t_blk - 1)
    valid = j <= t_blk
    score = jnp.where(valid, jnp.where(forced, FORCED_SCORE, imp), -jnp.inf)
    rank = jnp.zeros((n_slc, tq), jnp.int32)
    for jp in range(n_slc):
        c = score[jp:jp + 1, :]
        beats = (c > score) | ((c == score) & (jp < j))
        rank = rank + beats.astype(jnp.int32)
    n_sel = min(N_SELECT, n_slc)
    selt_ref[0, 0] = jnp.where((rank < n_sel) & valid, 1.0, 0.0).astype(selt_ref.dtype)


def _cmp_select(q, kvc, n_rep, tq=256):
    B, H, S, Dh = q.shape
    G = H // n_rep
    n_cmp = kvc.shape[2]
    n_slc = S // SLC_BLOCK
    cs = jnp.arange(n_cmp)[None, :] * CMP_STRIDE
    js = jnp.arange(n_slc)[:, None] * SLC_BLOCK
    ovt = jnp.maximum(jnp.minimum(cs + CMP_BLOCK, js + SLC_BLOCK) - jnp.maximum(cs, js), 0).astype(F32) / CMP_BLOCK
    return pl.pallas_call(
        _cmp_select_kernel,
        grid=(B, G, S // tq),
        in_specs=[pl.BlockSpec((1, n_rep, tq, Dh), lambda b, g, i: (b, g, i, 0)),
                  pl.BlockSpec((1, 1, n_cmp, Dh), lambda b, g, i: (b, g, 0, 0)),
                  pl.BlockSpec((1, 1, n_cmp, Dh), lambda b, g, i: (b, G + g, 0, 0)),
                  pl.BlockSpec((n_slc, n_cmp), lambda b, g, i: (0, 0))],
        out_specs=[pl.BlockSpec((1, tq, n_rep * Dh), lambda b, g, i: (b, i, g)),
                   pl.BlockSpec((1, 1, n_slc, tq), lambda b, g, i: (b, g, 0, i))],
        out_shape=[jax.ShapeDtypeStruct((B, S, H * Dh), F32), jax.ShapeDtypeStruct((B, G, n_slc, S), BF16)],
        compiler_params=_cparams(("parallel", "parallel", "parallel")),
        name="nsa_cmp_select",
    )(q, kvc, kvc, ovt.astype(BF16))


def _slc_attn_kernel(q_ref, selt_ref, e_ref, k_ref, v_ref, o_ref, m_s, l_s, acc_s, *, tk):
    i = pl.program_id(2)
    n_rep, tq, dh = q_ref.shape[1:]
    q = q_ref[0].reshape(n_rep * tq, dh)
    m_s[...] = jnp.full(m_s.shape, NEG_BIG, F32)
    l_s[...] = jnp.zeros(l_s.shape, F32)
    acc_s[...] = jnp.zeros(acc_s.shape, F32)
    t = i * tq + lax.broadcasted_iota(jnp.int32, (1, tq), 1)
    selt = selt_ref[0, 0]

    def chunk(k0, width):
        kpos = k0 + lax.broadcasted_iota(jnp.int32, (width, 1), 0)
        hit = _dot(e_ref[pl.ds(k0, width), :], selt)
        bias = jnp.where((kpos <= t) & (hit > 0.5), 0.0, NEG_BIG)
        s = lax.dot_general(k_ref[0, 0, pl.ds(k0, width), :], q, NT_DIMS, preferred_element_type=F32)
        s = s + jnp.concatenate([bias] * n_rep, axis=1)
        m_old = m_s[...]
        m_new = jnp.maximum(m_old, jnp.max(s, axis=0, keepdims=True))
        p = jnp.exp2(s - m_new)
        alpha = jnp.exp2(m_old - m_new)
        l_s[...] = alpha * l_s[...] + jnp.sum(p, axis=0, keepdims=True)
        pv = lax.dot_general(v_ref[0, 0, pl.ds(k0, width), :], p.astype(BF16), TN_DIMS, preferred_element_type=F32)
        acc_s[...] = alpha * acc_s[...] + pv
        m_s[...] = m_new

    def body(c, carry):
        chunk(pl.multiple_of(c * tk, tk), tk)
        return carry

    n_keys = (i + 1) * tq
    n_full = n_keys // tk
    rem = n_keys - n_full * tk
    lax.fori_loop(0, n_full, body, 0)
    tail0 = pl.multiple_of(n_full * tk, tk)

    @pl.when(rem > tk // 2)
    def _():
        chunk(tail0, tk)

    @pl.when((rem > 0) & (rem <= tk // 2))
    def _():
        chunk(tail0, tk // 2)

    _store_heads_as_rows(acc_s[...] / jnp.maximum(l_s[...], F32_TINY), n_rep, o_ref)


def _win_attn_kernel(q_ref, k_ref, v_ref, o_ref, *, tq):
    n_rep, rows, dh = q_ref.shape[1:]
    span = WINDOW + tq
    for u in range(rows // tq):
        i = pl.program_id(2) * (rows // tq) + u
        q = q_ref[0, :, u * tq:(u + 1) * tq, :].reshape(n_rep * tq, dh)
        t = i * tq + lax.broadcasted_iota(jnp.int32, (1, tq), 1)
        k0 = pl.multiple_of(jnp.maximum(i * tq - WINDOW, 0), tq)
        kpos = k0 + lax.broadcasted_iota(jnp.int32, (span, 1), 0)
        bias = jnp.where((kpos <= t) & (t - kpos < WINDOW), 0.0, NEG_BIG)
        s = lax.dot_general(k_ref[0, 0, pl.ds(k0, span), :], q, NT_DIMS, preferred_element_type=F32)
        s = s + jnp.concatenate([bias] * n_rep, axis=1)
        p = jnp.exp2(s - jnp.max(s, axis=0, keepdims=True))
        l = jnp.sum(p, axis=0, keepdims=True)
        pv = lax.dot_general(v_ref[0, 0, pl.ds(k0, span), :], p.astype(BF16), TN_DIMS, preferred_element_type=F32)
        _store_heads_as_rows(pv / jnp.maximum(l, F32_TINY), n_rep, o_ref, row0=u * tq)


def _attention(q, kv, k_off, v_off, n_rep, selt=None, tq=128, tk=1024, win_tiles=4):
    B, H, S, Dh = q.shape
    G = H // n_rep
    rows = tq * win_tiles if selt is None else tq
    q_spec = pl.BlockSpec((1, n_rep, rows, Dh), lambda b, g, i: (b, g, i, 0))
    kv_specs = [pl.BlockSpec((1, 1, S, Dh), lambda b, g, i: (b, k_off + g, 0, 0)),
                pl.BlockSpec((1, 1, S, Dh), lambda b, g, i: (b, v_off + g, 0, 0))]
    common = dict(
        grid=(B, G, S // rows),
        out_specs=pl.BlockSpec((1, rows, n_rep * Dh), lambda b, g, i: (b, i, g)),
        out_shape=jax.ShapeDtypeStruct((B, S, H * Dh), F32),
        compiler_params=_cparams(("parallel", "parallel", "arbitrary")),
    )
    if selt is None:
        assert WINDOW + tq <= S
        return pl.pallas_call(functools.partial(_win_attn_kernel, tq=tq), in_specs=[q_spec] + kv_specs,
                              name="nsa_attn_win", **common)(q, kv, kv)
    n_slc = selt.shape[2]
    tk = min(tk, S)
    e = (jnp.arange(S, dtype=jnp.int32)[:, None] // SLC_BLOCK == jnp.arange(n_slc, dtype=jnp.int32)[None, :]).astype(BF16)
    return pl.pallas_call(
        functools.partial(_slc_attn_kernel, tk=tk),
        in_specs=[q_spec, pl.BlockSpec((1, 1, n_slc, tq), lambda b, g, i: (b, g, 0, i)),
                  pl.BlockSpec((S, n_slc), lambda b, g, i: (0, 0))] + kv_specs,
        scratch_shapes=[pltpu.VMEM((1, n_rep * tq), F32), pltpu.VMEM((1, n_rep * tq), F32),
                        pltpu.VMEM((Dh, n_rep * tq), F32)],
        name="nsa_attn_slc", **common)(q, selt, e, kv, kv)


def _gate_oproj_norm_kernel(oc_ref, os_ref, ow_ref, gate_ref, ex_ref, w_ref, x_ref, gt_ref, g_ref, b_ref, out_ref):
    g_parts = _split_bf16(gate_ref[0], 2)
    o = None
    for br, o_ref in enumerate((oc_ref, os_ref, ow_ref)):
        ge = _dot(g_parts[0], ex_ref[br]) + _dot(g_parts[1], ex_ref[br])
        o = ge * o_ref[0] if o is None else o + ge * o_ref[0]
    y = _dot(o.astype(BF16), w_ref[...])
    v = ALPHA * x_ref[0] + (1.0 + gt_ref[0]) * y
    out_ref[0] = _layer_norm_rows(v, g_ref[...], b_ref[...])


def _gate_oproj_norm(o_cmp, o_slc, o_win, gates, o_w, x, gate_t, ln_g, ln_b, tm=256):
    B, S, D = x.shape
    QD = o_w.shape[0]
    lane = jnp.arange(LANES)[:, None]
    head = jnp.arange(QD)[None, :] // NSA_HEAD_DIM
    expand = jnp.stack([(lane == head * NSA_N_BRANCH + br) for br in range(NSA_N_BRANCH)]).astype(BF16)
    return pl.pallas_call(
        _gate_oproj_norm_kernel,
        grid=(B, S // tm),
        in_specs=[_row_spec(tm, QD), _row_spec(tm, QD), _row_spec(tm, QD), _row_spec(tm, LANES),
                  _const_spec((NSA_N_BRANCH, LANES, QD)), _const_spec((QD, D)), _row_spec(tm, D),
                  _batch_vec_spec(D), _const_spec((1, D)), _const_spec((1, D))],
        out_specs=_row_spec(tm, D),
        out_shape=jax.ShapeDtypeStruct((B, S, D), F32),
        compiler_params=_cparams(("parallel", "parallel")),
        name="nsa_gate_oproj_norm",
    )(o_cmp, o_slc, o_win, gates, expand, o_w.astype(BF16), x, gate_t, ln_g.reshape(1, D), ln_b.reshape(1, D))


def _nsa_shared_kv(x, sc, sh, kv_w, cmp_pos, phi_k_w1, phi_k_w2, phi_v_w1, phi_v_w2, cos, sin):
    kc, ka = _kvproj(x, sc, sh, kv_w, cos, sin)
    kvc = _compress(kc, cmp_pos, jnp.stack([phi_k_w1, phi_v_w1]), jnp.stack([phi_k_w2, phi_v_w2]))
    return kvc, ka


def _nsa_layer(x, sh, sc, gate_t, q_w, o_w, kvc, ka, cos, sin, ln_g, ln_b):
    G = NSA_KV_HEADS
    q_dim = o_w.shape[0]
    n_rep = q_dim // NSA_HEAD_DIM // G
    q, gates = _qproj(x, sc, sh, q_w, q_dim, cos, sin)
    o_cmp, selt = _cmp_select(q, kvc, n_rep)
    o_slc = _attention(q, ka, 0, G, n_rep, selt=selt)
    o_win = _attention(q, ka, 2 * G, 3 * G, n_rep)
    return _gate_oproj_norm(o_cmp, o_slc, o_win, gates, o_w, x, gate_t, ln_g, ln_b)


def kernel(x, c, pos, ada_w, ada_b, ln_g, ln_b, ssm_in_w, ssm_conv_w, ssm_conv_b, ssm_dt_bias, ssm_a_log, ssm_d, ssm_norm_w, ssm_out_w, kv_ada_w, kv_ada_b, kv_w, cmp_pos, phi_k_w1, phi_k_w2, phi_v_w1, phi_v_w2, nsa_q_w, nsa_o_w, router_w, router_b, moe_w_up, moe_b_up, moe_w_down, moe_b_down):
    c_act = jax.nn.silu(c)
    cos, sin = _rope_tables(pos)
    shared = None
    for i in range(DEPTH):
        mod = c_act @ ada_w[i] + ada_b[i]
        sh_t, sc_t, g_t, sh_c, sc_c, g_c = [m[:, None, :] for m in jnp.split(mod, 6, axis=-1)]
        if i < N_A:
            x = _mamba_layer(x, sh_t, sc_t, g_t, ssm_in_w[i], ssm_conv_w[i], ssm_conv_b[i], ssm_dt_bias[i],
                             ssm_a_log[i], ssm_d[i], ssm_norm_w[i], ssm_out_w[i], ln_g[i, 0], ln_b[i, 0])
        else:
            j = i - N_A
            x = _nsa_layer(x, sh_t, sc_t, g_t, nsa_q_w[j], nsa_o_w[j], *shared, cos, sin, ln_g[i, 0], ln_b[i, 0])
        x = _moe_layer(x, sh_c, sc_c, g_c, i, router_w[i], router_b[i],
                       moe_w_up, moe_b_up, moe_w_down, moe_b_down, ln_g[i, 1], ln_b[i, 1])
        if i == N_A - 1:
            kv_sh, kv_sc = [m[:, None, :] for m in jnp.split(c_act @ kv_ada_w + kv_ada_b, 2, axis=-1)]
            shared = _nsa_shared_kv(x, kv_sc, kv_sh, kv_w, cmp_pos, phi_k_w1, phi_k_w2, phi_v_w1, phi_v_w2, cos, sin)
    return x
```

```python
import functools

import jax
import jax.numpy as jnp
import numpy as np
from jax import lax
from jax.experimental import pallas as pl
from jax.experimental.pallas import tpu as pltpu

F32 = jnp.float32
BF16 = jnp.bfloat16

DEPTH = 4
N_A = DEPTH // 2
ALPHA = (2.0 * DEPTH) ** 0.25
LN_EPS = 1e-5

SSM_HEAD_DIM = 64
SSM_GROUPS = 4
SSM_STATE = 128
CONV_WIDTH = 4
SSM_CHUNK = 128

NSA_HEAD_DIM = 64
NSA_KV_HEADS = 4
NSA_N_BRANCH = 3
CMP_BLOCK = 32
CMP_STRIDE = 16
SLC_BLOCK = 64
N_SELECT = 16
WINDOW = 512
ROPE_THETA = 10000.0
ATTN_SCALE = NSA_HEAD_DIM ** -0.5
FORCED_SCORE = 1e9

TOP_K = 4
SWIGLU_LIMIT = 7.0
SWIGLU_ALPHA = 1.702

VMEM_LIMIT_BYTES_V7X = 56 * 1024 * 1024
LANES = 128
SUBLANES = 8
NEG_BIG = -1e30
F32_TINY = float(np.finfo(np.float32).tiny)
LOG2_E = float(np.log2(np.e))
NT_DIMS = (((1,), (1,)), ((), ()))
TN_DIMS = (((0,), (0,)), ((), ()))


def _cparams(sem):
    return pltpu.CompilerParams(dimension_semantics=sem, vmem_limit_bytes=VMEM_LIMIT_BYTES_V7X)


def _dot(a, b):
    return jnp.dot(a, b, preferred_element_type=F32)


def _split_bf16(a, parts):
    out = []
    r = a
    for _ in range(parts):
        p = r.astype(BF16)
        out.append(p)
        r = r - p.astype(F32)
    return out


def _layer_norm_rows(v, g, b):
    mu = jnp.mean(v, axis=-1, keepdims=True)
    d = v - mu
    var = jnp.mean(d * d, axis=-1, keepdims=True)
    return d * lax.rsqrt(var + LN_EPS) * g + b


def _modulated_bf16(x_ref, sc_ref, sh_ref):
    return (x_ref[0] * (1.0 + sc_ref[0]) + sh_ref[0]).astype(BF16)


def _row_spec(tm, width):
    return pl.BlockSpec((1, tm, width), lambda b, s: (b, s, 0))


def _batch_vec_spec(width):
    return pl.BlockSpec((1, 1, width), lambda b, s: (b, 0, 0))


def _const_spec(shape):
    nd = len(shape)
    return pl.BlockSpec(shape, lambda b, s: (0,) * nd)


def _router_kernel(x_ref, sc_ref, sh_ref, rw_ref, rb_ref, h_ref, idx_ref, gate_ref, cnt_ref, running):
    first_step = (pl.program_id(0) == 0) & (pl.program_id(1) == 0)

    @pl.when(first_step)
    def _():
        running[...] = jnp.zeros(running.shape, F32)

    hb = _modulated_bf16(x_ref, sc_ref, sh_ref)
    h_ref[0] = hb
    logits = _dot(hb, rw_ref[...]) + rb_ref[...]
    tm, n_exp = logits.shape
    lane_e = lax.broadcasted_iota(jnp.int32, logits.shape, 1)
    lane_o = lax.broadcasted_iota(jnp.int32, idx_ref.shape[1:], 1)
    vals, idxs = [], []
    l = logits
    for _ in range(TOP_K):
        m = jnp.max(l, axis=-1, keepdims=True)
        idx = jnp.min(jnp.where(l == m, lane_e, n_exp), axis=-1, keepdims=True)
        vals.append(m)
        idxs.append(idx)
        l = jnp.where(lane_e == idx, -jnp.inf, l)
    es = [jnp.exp(v - vals[0]) for v in vals]
    denom = es[0] + es[1] + es[2] + es[3]
    idx_out = jnp.zeros(idx_ref.shape[1:], jnp.int32)
    gate_out = jnp.zeros(gate_ref.shape[1:], F32)
    chosen = sum(jnp.where(lane_e == ix, 1.0, 0.0) for ix in idxs)
    row = lax.broadcasted_iota(jnp.int32, (tm, tm), 0)
    col = lax.broadcasted_iota(jnp.int32, (tm, tm), 1)
    earlier = jnp.where(col < row, 1.0, 0.0).astype(BF16)
    before = _dot(earlier, chosen.astype(BF16)) + running[...]
    for k in range(TOP_K):
        rank_k = jnp.sum(jnp.where(lane_e == idxs[k], before, 0.0), axis=-1, keepdims=True).astype(jnp.int32)
        idx_out = jnp.where(lane_o == k, idxs[k], idx_out)
        idx_out = jnp.where(lane_o == TOP_K + k, rank_k, idx_out)
        gate_out = jnp.where(lane_o == k, es[k] / denom, gate_out)
    idx_ref[0] = idx_out
    gate_ref[0] = gate_out
    running[...] = running[...] + jnp.sum(chosen, axis=0, keepdims=True)
    cnt_ref[...] = jnp.broadcast_to(running[...], cnt_ref.shape)


def _router(x, sc, sh, rw, rb, tm=512):
    B, S, D = x.shape
    E = rw.shape[-1]
    return pl.pallas_call(
        _router_kernel,
        grid=(B, S // tm),
        in_specs=[_row_spec(tm, D), _batch_vec_spec(D), _batch_vec_spec(D), _const_spec((D, E)), _const_spec((1, E))],
        out_specs=[_row_spec(tm, D), _row_spec(tm, LANES), _row_spec(tm, LANES), _const_spec((SUBLANES, E))],
        out_shape=[
            jax.ShapeDtypeStruct((B, S, D), BF16),
            jax.ShapeDtypeStruct((B, S, LANES), jnp.int32),
            jax.ShapeDtypeStruct((B, S, LANES), F32),
            jax.ShapeDtypeStruct((SUBLANES, E), F32),
        ],
        scratch_shapes=[pltpu.VMEM((1, E), F32)],
        compiler_params=_cparams(("arbitrary", "arbitrary")),
        name="moe_router",
    )(x, sc, sh, rw.astype(BF16), rb.reshape(1, E))


def _ffn_kernel(blk_e_ref, first_ref, nused_ref, x_ref, wu_ref, bu_ref, wd_ref, bd_ref, o_ref, wu_s, wd_s):
    i = pl.program_id(0)
    d_exp = wd_s.shape[0]

    @pl.when(i < nused_ref[0])
    def _():
        @pl.when(first_ref[i] == 1)
        def _():
            wu_s[...] = wu_ref[0].astype(BF16)
            wd_s[...] = wd_ref[0].astype(BF16)

        u = _dot(x_ref[...], wu_s[...]) + bu_ref[0]
        glu = jnp.minimum(u[:, :d_exp], SWIGLU_LIMIT)
        lin = jnp.clip(u[:, d_exp:], -SWIGLU_LIMIT, SWIGLU_LIMIT)
        act = glu * jax.nn.sigmoid(SWIGLU_ALPHA * glu) * (lin + 1.0)
        o_ref[...] = (_dot(act.astype(BF16), wd_s[...]) + bd_ref[0]).astype(o_ref.dtype)

    @pl.when(i >= nused_ref[0])
    def _():
        o_ref[...] = jnp.zeros_like(o_ref)


def _expert_ffn(xg, blk_e, first, nused, layer, w_up, b_up, w_down, b_down, tm):
    R, D = xg.shape
    n_layers, E, _, N2 = w_up.shape
    d_exp = w_down.shape[2]
    n_blk = R // tm
    grid_spec = pltpu.PrefetchScalarGridSpec(
        num_scalar_prefetch=3,
        grid=(n_blk,),
        in_specs=[
            pl.BlockSpec((tm, D), lambda i, be, fi, nu: (i, 0)),
            pl.BlockSpec((None, 1, D, N2), lambda i, be, fi, nu: (layer, be[i], 0, 0)),
            pl.BlockSpec((None, 1, 1, N2), lambda i, be, fi, nu: (layer, be[i], 0, 0)),
            pl.BlockSpec((None, 1, d_exp, D), lambda i, be, fi, nu: (layer, be[i], 0, 0)),
            pl.BlockSpec((None, 1, 1, D), lambda i, be, fi, nu: (layer, be[i], 0, 0)),
        ],
        out_specs=pl.BlockSpec((tm, D), lambda i, be, fi, nu: (i, 0)),
        scratch_shapes=[pltpu.VMEM((D, N2), BF16), pltpu.VMEM((d_exp, D), BF16)],
    )
    return pl.pallas_call(
        _ffn_kernel,
        grid_spec=grid_spec,
        out_shape=jax.ShapeDtypeStruct((R, D), BF16),
        compiler_params=_cparams(("arbitrary",)),
        name="moe_expert_ffn",
    )(blk_e, first, nused, xg, w_up, b_up.reshape(n_layers, E, 1, N2), w_down, b_down.reshape(n_layers, E, 1, D))


def _combine_norm_kernel(x_ref, yg_ref, gate_ref, gc_ref, g_ref, b_ref, o_ref):
    gate = gate_ref[0]
    y = yg_ref[0, 0] * gate[:, 0:1]
    for k in range(1, TOP_K):
        y = y + yg_ref[k, 0] * gate[:, k:k + 1]
    v = ALPHA * x_ref[0] + (1.0 + gc_ref[0]) * y
    o_ref[0] = _layer_norm_rows(v, g_ref[...], b_ref[...])


def _combine_norm(x, yg, gate, gate_c, ln_g, ln_b, tm=256):
    B, S, D = x.shape
    return pl.pallas_call(
        _combine_norm_kernel,
        grid=(B, S // tm),
        in_specs=[
            _row_spec(tm, D),
            pl.BlockSpec((TOP_K, 1, tm, D), lambda b, s: (0, b, s, 0)),
            _row_spec(tm, LANES),
            _batch_vec_spec(D),
            _const_spec((1, D)),
            _const_spec((1, D)),
        ],
        out_specs=_row_spec(tm, D),
        out_shape=jax.ShapeDtypeStruct((B, S, D), F32),
        compiler_params=_cparams(("parallel", "parallel")),
        name="moe_combine_norm",
    )(x, yg, gate, gate_c, ln_g.reshape(1, D), ln_b.reshape(1, D))


def _moe_layer(x, sh, sc, gate_c, layer, router_w, router_b, w_up, b_up, w_down, b_down, ln_g, ln_b, row_block=256):
    B, S, D = x.shape
    T = B * S
    E = router_w.shape[-1]
    h, route, gate, cnt = _router(x, sc, sh, router_w, router_b)
    A = T * TOP_K
    eid = route[:, :, :TOP_K].reshape(A)
    rank = route[:, :, TOP_K:2 * TOP_K].reshape(A)
    counts = cnt[0].astype(jnp.int32)
    padded = (counts + row_block - 1) // row_block * row_block
    off = jnp.cumsum(counts) - counts
    cum_padded = jnp.cumsum(padded)
    poff = cum_padded - padded
    pos = poff[eid] + rank
    R_rows = A + E * row_block
    n_blk = R_rows // row_block
    nused = (cum_padded[-1] // row_block).astype(jnp.int32)
    blk_start = jnp.arange(n_blk, dtype=jnp.int32) * row_block
    blk_e = jnp.minimum(jnp.sum(cum_padded[None, :] <= blk_start[:, None], axis=1), E - 1).astype(jnp.int32)
    last_e = blk_e[jnp.maximum(nused - 1, 0)]
    blk_e = jnp.where(jnp.arange(n_blk) < nused, blk_e, last_e)
    first = jnp.concatenate([jnp.ones((1,), jnp.int32), (blk_e[1:] != blk_e[:-1]).astype(jnp.int32)])
    order = jnp.argsort(eid).astype(jnp.int32)
    row = jnp.arange(R_rows, dtype=jnp.int32)
    per_row = lambda table: jnp.repeat(table[blk_e], row_block)
    j = row - per_row(poff)
    row_tok = jnp.where(j < per_row(counts), order[jnp.minimum(per_row(off) + j, A - 1)] // TOP_K, row % T)
    xg = h.reshape(T, D)[row_tok]
    y = _expert_ffn(xg, blk_e, first, nused.reshape(1), layer, w_up, b_up, w_down, b_down, row_block)
    yg = y[pos.reshape(T, TOP_K).T].reshape(TOP_K, B, S, D)
    return _combine_norm(x, yg, gate, gate_c, ln_g, ln_b)


def _outproj_norm_kernel(y_ref, w_ref, x_ref, gt_ref, g_ref, b_ref, o_ref):
    y = _dot(y_ref[0], w_ref[...])
    v = ALPHA * x_ref[0] + (1.0 + gt_ref[0]) * y
    o_ref[0] = _layer_norm_rows(v, g_ref[...], b_ref[...])


def _outproj_norm(y, w, x, gate_t, ln_g, ln_b, tm=512):
    B, S, D = x.shape
    K = y.shape[-1]
    return pl.pallas_call(
        _outproj_norm_kernel,
        grid=(B, S // tm),
        in_specs=[_row_spec(tm, K), _const_spec((K, D)), _row_spec(tm, D), _batch_vec_spec(D),
                  _const_spec((1, D)), _const_spec((1, D))],
        out_specs=_row_spec(tm, D),
        out_shape=jax.ShapeDtypeStruct((B, S, D), F32),
        compiler_params=_cparams(("parallel", "parallel")),
        name="outproj_norm",
    )(y, w.astype(BF16), x, gate_t, ln_g.reshape(1, D), ln_b.reshape(1, D))


def _ssm_inproj_kernel(x_ref, sc_ref, sh_ref, wz_ref, wx_ref, wdt_ref, cw_ref, cb_ref, dtb_ref,
                       z_ref, xs_ref, bm_ref, cm_ref, dt_ref, xbuf, *, col_chunk):
    tm = x_ref.shape[1]
    d_inner = xs_ref.shape[-1]
    bc = bm_ref.shape[-1]
    conv_dim = wx_ref.shape[-1]
    halo = SUBLANES

    @pl.when(pl.program_id(1) == 0)
    def _():
        xbuf[0:halo, :] = jnp.zeros((halo, conv_dim), F32)

    hb = _modulated_bf16(x_ref, sc_ref, sh_ref)
    z_ref[0] = _dot(hb, wz_ref[...])
    dt_ref[0] = jax.nn.softplus(_dot(hb, wdt_ref[...]) + dtb_ref[...])
    for c0 in range(0, conv_dim, col_chunk):
        cs = slice(c0, c0 + col_chunk)
        xbuf[halo:halo + tm, cs] = _dot(hb, wx_ref[:, cs])
        acc = cb_ref[:, cs] + cw_ref[CONV_WIDTH - 1:CONV_WIDTH, cs] * xbuf[halo:halo + tm, cs]
        for w in range(CONV_WIDTH - 1):
            lo = halo - (CONV_WIDTH - 1) + w
            acc = acc + cw_ref[w:w + 1, cs] * xbuf[lo:lo + tm, cs]
        y = acc * jax.nn.sigmoid(acc)
        if c0 + col_chunk <= d_inner:
            xs_ref[0, :, cs] = y
        elif c0 + col_chunk <= d_inner + bc:
            bm_ref[0, :, c0 - d_inner:c0 - d_inner + col_chunk] = y.astype(BF16)
        else:
            cm_ref[0, :, c0 - d_inner - bc:c0 - d_inner - bc + col_chunk] = y.astype(BF16)
        xbuf[0:halo, cs] = xbuf[tm:tm + halo, cs]


def _ssm_inproj(x, sc, sh, in_w, conv_w, conv_b, dt_bias, d_inner, bc_dim, tm=256, col_chunk=512):
    B, S, D = x.shape
    conv_dim = d_inner + 2 * bc_dim
    heads = dt_bias.shape[0]
    wz = in_w[:, :d_inner].astype(BF16)
    wx = in_w[:, d_inner:d_inner + conv_dim].astype(BF16)
    wdt = jnp.pad(in_w[:, d_inner + conv_dim:], ((0, 0), (0, LANES - heads))).astype(BF16)
    dtb = jnp.pad(dt_bias, (0, LANES - heads)).reshape(1, LANES)
    assert d_inner % col_chunk == 0 and bc_dim % col_chunk == 0
    return pl.pallas_call(
        functools.partial(_ssm_inproj_kernel, col_chunk=col_chunk),
        grid=(B, S // tm),
        in_specs=[_row_spec(tm, D), _batch_vec_spec(D), _batch_vec_spec(D),
                  _const_spec((D, d_inner)), _const_spec((D, conv_dim)), _const_spec((D, LANES)),
                  _const_spec((CONV_WIDTH, conv_dim)), _const_spec((1, conv_dim)), _const_spec((1, LANES))],
        out_specs=[_row_spec(tm, d_inner), _row_spec(tm, d_inner), _row_spec(tm, bc_dim), _row_spec(tm, bc_dim),
                   _row_spec(tm, LANES)],
        out_shape=[
            jax.ShapeDtypeStruct((B, S, d_inner), F32),
            jax.ShapeDtypeStruct((B, S, d_inner), F32),
            jax.ShapeDtypeStruct((B, S, bc_dim), BF16),
            jax.ShapeDtypeStruct((B, S, bc_dim), BF16),
            jax.ShapeDtypeStruct((B, S, LANES), F32),
        ],
        scratch_shapes=[pltpu.VMEM((SUBLANES + tm + SUBLANES, conv_dim), F32)],
        compiler_params=_cparams(("parallel", "arbitrary")),
        name="ssm_inproj_conv",
    )(x, sc, sh, wz, wx, wdt, conv_w, conv_b.reshape(1, conv_dim), dtb)


def _ssd_kernel(xs_ref, bm_ref, cm_ref, z_ref, dt_ref, dtt_ref, aneg_ref, anegt_ref, dskip_ref, nw_ref, ex_ref,
                y_ref, state, *, heads_per_group):
    L = xs_ref.shape[1]
    P = SSM_HEAD_DIM
    N = SSM_STATE
    n_groups = bm_ref.shape[-1] // N
    gw = heads_per_group * P

    @pl.when(pl.program_id(1) == 0)
    def _():
        state[...] = jnp.zeros(state.shape, F32)

    row = lax.broadcasted_iota(jnp.int32, (L, L), 0)
    col = lax.broadcasted_iota(jnp.int32, (L, L), 1)
    lower = row >= col
    tri = jnp.where(lower, 1.0, 0.0).astype(BF16)
    tri_t = jnp.where(row <= col, 1.0, 0.0).astype(BF16)
    dt = dt_ref[0]
    a_cs = sum(_dot(tri, p) for p in _split_bf16(dt * aneg_ref[...], 3))
    a_cs_t = sum(_dot(p, tri_t) for p in _split_bf16(dtt_ref[0] * anegt_ref[...], 3))
    dt_parts = _split_bf16(dt, 2)
    acs_parts = _split_bf16(a_cs, 2)
    lane = lax.broadcasted_iota(jnp.int32, (1, 2 * P), 1)
    first_head = lane < P
    for g in range(n_groups):
        gs = slice(g * gw, (g + 1) * gw)
        ex = ex_ref[g]
        dt_x = _dot(dt_parts[0], ex) + _dot(dt_parts[1], ex)
        acs_x = _dot(acs_parts[0], ex) + _dot(acs_parts[1], ex)
        end_x = acs_x[L - 1:L, :]
        bg = bm_ref[0, :, g * N:(g + 1) * N]
        cg = cm_ref[0, :, g * N:(g + 1) * N]
        xs_g = xs_ref[0, :, gs]
        xdt = xs_g * dt_x
        xdt_b = xdt.astype(BF16)
        s_g = state[g]
        y = _dot(cg, s_g.astype(BF16)) * jnp.exp(acs_x) + xs_g * dskip_ref[:, gs]
        to_end = (xdt * jnp.exp(end_x - acs_x)).astype(BF16)
        state[g] = s_g * jnp.exp(end_x) + lax.dot_general(bg, to_end, TN_DIMS, preferred_element_type=F32)
        cb = lax.dot_general(cg, bg, NT_DIMS, preferred_element_type=F32)
        diag = []
        for u in range(heads_per_group // 2):
            ms = []
            for h in (g * heads_per_group + 2 * u, g * heads_per_group + 2 * u + 1):
                seg = a_cs[:, h:h + 1] - a_cs_t[h:h + 1, :]
                ms.append((cb * jnp.exp(jnp.where(lower, seg, NEG_BIG))).astype(BF16))
            xp = xdt_b[:, 2 * u * P:(2 * u + 2) * P]
            zero = jnp.zeros_like(xp)
            rhs = jnp.concatenate([jnp.where(first_head, xp, zero), jnp.where(first_head, zero, xp)], axis=0)
            diag.append(_dot(jnp.concatenate(ms, axis=1), rhs))
        y = y + jnp.concatenate(diag, axis=1)
        zz = z_ref[0, :, gs]
        y = y * (zz * jax.nn.sigmoid(zz))
        scale = lax.rsqrt(jnp.sum(y * y, axis=-1, keepdims=True) / gw + LN_EPS)
        y_ref[0, :, gs] = (y * scale * nw_ref[:, gs]).astype(BF16)


def _ssd(xs, bm, cm, z, dt, a_log, d_skip, norm_w):
    B, S, d_inner = xs.shape
    heads = a_log.shape[0]
    bc_dim = bm.shape[-1]
    L = SSM_CHUNK
    a_neg = -jnp.exp(a_log)
    dt_t = jnp.swapaxes(dt[:, :, :heads], 1, 2)
    aneg = jnp.pad(a_neg, (0, LANES - heads)).reshape(1, LANES)
    anegt = jnp.broadcast_to(a_neg[:, None], (heads, L))
    dskip = jnp.repeat(d_skip, SSM_HEAD_DIM).reshape(1, d_inner)
    hpg = heads // SSM_GROUPS
    gw = hpg * SSM_HEAD_DIM
    head_of_lane = jnp.arange(SSM_GROUPS)[:, None, None] * hpg + jnp.arange(gw)[None, None, :] // SSM_HEAD_DIM
    expand = (jnp.arange(LANES)[None, :, None] == head_of_lane).astype(BF16)
    return pl.pallas_call(
        functools.partial(_ssd_kernel, heads_per_group=hpg),
        grid=(B, S // L),
        in_specs=[_row_spec(L, d_inner), _row_spec(L, bc_dim), _row_spec(L, bc_dim), _row_spec(L, d_inner),
                  _row_spec(L, LANES), pl.BlockSpec((1, heads, L), lambda b, c: (b, 0, c)),
                  _const_spec((1, LANES)), _const_spec((heads, L)), _const_spec((1, d_inner)), _const_spec((1, d_inner)),
                  _const_spec((SSM_GROUPS, LANES, gw))],
        out_specs=_row_spec(L, d_inner),
        out_shape=jax.ShapeDtypeStruct((B, S, d_inner), BF16),
        scratch_shapes=[pltpu.VMEM((SSM_GROUPS, SSM_STATE, gw), F32)],
        compiler_params=_cparams(("parallel", "arbitrary")),
        name="ssd_chunk_scan",
    )(xs, bm, cm, z, dt, dt_t, aneg, anegt, dskip, norm_w.reshape(1, d_inner), expand)


def _mamba_layer(x, sh, sc, gate_t, in_w, conv_w, conv_b, dt_bias, a_log, d_skip, norm_w, out_w, ln_g, ln_b):
    d_inner = out_w.shape[0]
    bc_dim = SSM_GROUPS * SSM_STATE
    z, xs, bm, cm, dt = _ssm_inproj(x, sc, sh, in_w, conv_w, conv_b, dt_bias, d_inner, bc_dim)
    y = _ssd(xs, bm, cm, z, dt, a_log, d_skip, norm_w)
    return _outproj_norm(y, out_w, x, gate_t, ln_g, ln_b)


def _rope_tables(pos):
    half = NSA_HEAD_DIM // 2
    inv = ROPE_THETA ** (-jnp.arange(half, dtype=F32) / half)
    ang = pos.astype(F32)[..., None] * inv
    cos, sin = jnp.cos(ang), jnp.sin(ang)
    return jnp.concatenate([cos, cos, cos, cos], axis=-1), jnp.concatenate([-sin, sin, -sin, sin], axis=-1)


def _rope_slab(x, cos, sin):
    half = NSA_HEAD_DIM // 2
    lane = lax.broadcasted_iota(jnp.int32, (1, LANES), 1)
    first_half = (lane % NSA_HEAD_DIM) < half
    partner = jnp.where(first_half, pltpu.roll(x, LANES - half, 1), pltpu.roll(x, half, 1))
    return x * cos + partner * sin


def _kvproj_kernel(x_ref, sc_ref, sh_ref, w_ref, cos_ref, sin_ref, kc_ref, ka_ref):
    hb = _modulated_bf16(x_ref, sc_ref, sh_ref)
    cos, sin = cos_ref[0], sin_ref[0]
    dh = NSA_HEAD_DIM
    heads_per_slab = LANES // dh
    n_slab = w_ref.shape[-1] // LANES
    slabs_per_branch = NSA_KV_HEADS // heads_per_slab
    for j in range(n_slab):
        slab = _dot(hb, w_ref[:, j * LANES:(j + 1) * LANES])
        branch = j // slabs_per_branch
        if branch % 2 == 0:
            slab = _rope_slab(slab, cos, sin)
        for u in range(heads_per_slab):
            head = (j % slabs_per_branch) * heads_per_slab + u
            piece = slab[:, u * dh:(u + 1) * dh]
            if branch < 2:
                kc_ref[0, branch * NSA_KV_HEADS + head] = piece
            else:
                ka_ref[0, (branch - 2) * NSA_KV_HEADS + head] = piece.astype(BF16)


def _kvproj(x, sc, sh, kv_w, cos, sin, tm=512):
    B, S, D = x.shape
    N = kv_w.shape[-1]
    n_cmp_arr = 2 * NSA_KV_HEADS
    n_att_arr = 4 * NSA_KV_HEADS
    return pl.pallas_call(
        _kvproj_kernel,
        grid=(B, S // tm),
        in_specs=[_row_spec(tm, D), _batch_vec_spec(D), _batch_vec_spec(D), _const_spec((D, N)),
                  _row_spec(tm, LANES), _row_spec(tm, LANES)],
        out_specs=[pl.BlockSpec((1, n_cmp_arr, tm, NSA_HEAD_DIM), lambda b, s: (b, 0, s, 0)),
                   pl.BlockSpec((1, n_att_arr, tm, NSA_HEAD_DIM), lambda b, s: (b, 0, s, 0))],
        out_shape=[jax.ShapeDtypeStruct((B, n_cmp_arr, S, NSA_HEAD_DIM), F32),
                   jax.ShapeDtypeStruct((B, n_att_arr, S, NSA_HEAD_DIM), BF16)],
        compiler_params=_cparams(("parallel", "parallel")),
        name="nsa_kv_proj",
    )(x, sc, sh, kv_w.astype(BF16), cos, sin)


def _compress_kernel(k_ref, pos_ref, w1_ref, w2_ref, o_ref):
    ks = k_ref[0, 0]
    n_sub, half_in = ks.shape
    top = _dot((ks + pos_ref[0:1, :]).astype(BF16), w1_ref[0, :half_in, :])
    bot = _dot((ks + pos_ref[1:2, :]).astype(BF16), w1_ref[0, half_in:, :])
    hid = jax.nn.gelu(top + pltpu.roll(bot, n_sub - 1, 0))
    o_ref[0, 0] = _dot(hid.astype(BF16), w2_ref[0]).astype(o_ref.dtype)


def _compress(kc, cmp_pos, w1s, w2s):
    B, A, S, Dh = kc.shape
    n_sub = S // CMP_STRIDE
    half_in = CMP_STRIDE * Dh
    hid = w1s.shape[-1]
    ksub = kc.reshape(B, A, n_sub, half_in)
    pos2 = cmp_pos.reshape(2, half_in)
    return pl.pallas_call(
        _compress_kernel,
        grid=(B, A),
        in_specs=[pl.BlockSpec((1, 1, n_sub, half_in), lambda b, a: (b, a, 0, 0)),
                  pl.BlockSpec((2, half_in), lambda b, a: (0, 0)),
                  pl.BlockSpec((1, 2 * half_in, hid), lambda b, a: (a // NSA_KV_HEADS, 0, 0)),
                  pl.BlockSpec((1, hid, Dh), lambda b, a: (a // NSA_KV_HEADS, 0, 0))],
        out_specs=pl.BlockSpec((1, 1, n_sub, Dh), lambda b, a: (b, a, 0, 0)),
        out_shape=jax.ShapeDtypeStruct((B, A, n_sub, Dh), BF16),
        compiler_params=_cparams(("parallel", "parallel")),
        name="nsa_compress",
    )(ksub, pos2, w1s.astype(BF16), w2s.astype(BF16))


def _qproj_kernel(x_ref, sc_ref, sh_ref, wq_ref, wg_ref, cos_ref, sin_ref, q_ref, g_ref):
    hb = _modulated_bf16(x_ref, sc_ref, sh_ref)
    cos, sin = cos_ref[0], sin_ref[0]
    dh = NSA_HEAD_DIM
    heads_per_slab = LANES // dh
    for j in range(wq_ref.shape[-1] // LANES):
        slab = _rope_slab(_dot(hb, wq_ref[:, j * LANES:(j + 1) * LANES]), cos, sin) * (ATTN_SCALE * LOG2_E)
        for u in range(heads_per_slab):
            q_ref[0, j * heads_per_slab + u] = slab[:, u * dh:(u + 1) * dh].astype(BF16)
    g_ref[0] = jax.nn.sigmoid(_dot(hb, wg_ref[...]))


def _qproj(x, sc, sh, q_w, q_dim, cos, sin, tm=512):
    B, S, D = x.shape
    heads = q_dim // NSA_HEAD_DIM
    n_gate = q_w.shape[-1] - q_dim
    wq = q_w[:, :q_dim].astype(BF16)
    wg = jnp.pad(q_w[:, q_dim:], ((0, 0), (0, LANES - n_gate))).astype(BF16)
    return pl.pallas_call(
        _qproj_kernel,
        grid=(B, S // tm),
        in_specs=[_row_spec(tm, D), _batch_vec_spec(D), _batch_vec_spec(D), _const_spec((D, q_dim)),
                  _const_spec((D, LANES)), _row_spec(tm, LANES), _row_spec(tm, LANES)],
        out_specs=[pl.BlockSpec((1, heads, tm, NSA_HEAD_DIM), lambda b, s: (b, 0, s, 0)), _row_spec(tm, LANES)],
        out_shape=[jax.ShapeDtypeStruct((B, heads, S, NSA_HEAD_DIM), BF16),
                   jax.ShapeDtypeStruct((B, S, LANES), F32)],
        compiler_params=_cparams(("parallel", "parallel")),
        name="nsa_q_proj",
    )(x, sc, sh, wq, wg, cos, sin)


def _store_heads_as_rows(o_t, n_rep, o_ref, row0=0):
    dh = o_t.shape[0]
    tq = o_t.shape[1] // n_rep
    for r in range(0, n_rep, 2):
        pair = jnp.concatenate([o_t[:, r * tq:(r + 1) * tq], o_t[:, (r + 1) * tq:(r + 2) * tq]], axis=0)
        o_ref[0, row0:row0 + tq, r * dh:(r + 2) * dh] = pair.T


def _cmp_select_kernel(q_ref, k_ref, v_ref, ovt_ref, o_ref, selt_ref):
    i = pl.program_id(2)
    n_rep, tq, dh = q_ref.shape[1:]
    n_cmp = k_ref.shape[2]
    n_slc = ovt_ref.shape[0]
    q = q_ref[0].reshape(n_rep * tq, dh)
    t = i * tq + lax.broadcasted_iota(jnp.int32, (1, tq), 1)
    blk_end = lax.broadcasted_iota(jnp.int32, (n_cmp, 1), 0) * CMP_STRIDE + (CMP_BLOCK - 1)
    mask = jnp.concatenate([blk_end <= t] * n_rep, axis=1)
    s = lax.dot_general(k_ref[0, 0], q, NT_DIMS, preferred_element_type=F32)
    s = jnp.where(mask, s, NEG_BIG)
    m = jnp.max(s, axis=0, keepdims=True)
    p = jnp.where(mask, jnp.exp2(s - m), 0.0)
    p = p / jnp.maximum(jnp.sum(p, axis=0, keepdims=True), F32_TINY)
    pb = p.astype(BF16)
    _store_heads_as_rows(lax.dot_general(v_ref[0, 0], pb, TN_DIMS, preferred_element_type=F32), n_rep, o_ref)
    imp = _dot(ovt_ref[...], pb[:, 0:tq])
    for r in range(1, n_rep):
        imp = imp + _dot(ovt_ref[...], pb[:, r * tq:(r + 1) * tq])
    t_blk = t // SLC_BLOCK
    j = lax.broadcasted_iota(jnp.int32, (n_slc, 1), 0)
    forced = (j == 0) | (j == t_blk) | (j == t_blk - 1)
    valid = j <= t_blk
    score = jnp.where(valid, jnp.where(forced, FORCED_SCORE, imp), -jnp.inf)
    n_tiles = n_slc // SUBLANES
    tiles = [score[r * SUBLANES:(r + 1) * SUBLANES, :] for r in range(n_tiles)]
    j_tiles = [j[r * SUBLANES:(r + 1) * SUBLANES, :] for r in range(n_tiles)]
    ranks = [jnp.zeros((SUBLANES, tq), jnp.int32) for _ in range(n_tiles)]
    for jp in range(n_slc):
        c = jnp.broadcast_to(score[jp:jp + 1, :], (SUBLANES, tq))
        for r in range(n_tiles):
            if jp < r * SUBLANES:
                beats = (c >= tiles[r]).astype(jnp.int32)
            elif jp >= (r + 1) * SUBLANES:
                beats = (c > tiles[r]).astype(jnp.int32)
            else:
                beats = jnp.where(j_tiles[r] > jp, (c >= tiles[r]).astype(jnp.int32), (c > tiles[r]).astype(jnp.int32))
            ranks[r] = ranks[r] + beats
    rank = jnp.concatenate(ranks, axis=0)
    n_sel = min(N_SELECT, n_slc)
    selt_ref[0, 0] = jnp.where((rank < n_sel) & valid, 1.0, 0.0).astype(selt_ref.dtype)


def _cmp_select(q, kvc, n_rep, tq=512):
    B, H, S, Dh = q.shape
    G = H // n_rep
    n_cmp = kvc.shape[2]
    n_slc = S // SLC_BLOCK
    cs = jnp.arange(n_cmp)[None, :] * CMP_STRIDE
    js = jnp.arange(n_slc)[:, None] * SLC_BLOCK
    ovt = jnp.maximum(jnp.minimum(cs + CMP_BLOCK, js + SLC_BLOCK) - jnp.maximum(cs, js), 0).astype(F32) / CMP_BLOCK
    return pl.pallas_call(
        _cmp_select_kernel,
        grid=(B, G, S // tq),
        in_specs=[pl.BlockSpec((1, n_rep, tq, Dh), lambda b, g, i: (b, g, i, 0)),
                  pl.BlockSpec((1, 1, n_cmp, Dh), lambda b, g, i: (b, g, 0, 0)),
                  pl.BlockSpec((1, 1, n_cmp, Dh), lambda b, g, i: (b, G + g, 0, 0)),
                  pl.BlockSpec((n_slc, n_cmp), lambda b, g, i: (0, 0))],
        out_specs=[pl.BlockSpec((1, tq, n_rep * Dh), lambda b, g, i: (b, i, g)),
                   pl.BlockSpec((1, 1, n_slc, tq), lambda b, g, i: (b, g, 0, i))],
        out_shape=[jax.ShapeDtypeStruct((B, S, H * Dh), F32), jax.ShapeDtypeStruct((B, G, n_slc, S), BF16)],
        compiler_params=_cparams(("parallel", "parallel", "parallel")),
        name="nsa_cmp_select",
    )(q, kvc, kvc, ovt.astype(BF16))


def _slc_attn_kernel(q_ref, selt_ref, e_ref, k_ref, v_ref, o_ref, m_s, l_s, acc_s, *, tk):
    i = pl.program_id(2)
    n_rep, tq, dh = q_ref.shape[1:]
    q = q_ref[0].reshape(n_rep * tq, dh)
    m_s[...] = jnp.full(m_s.shape, NEG_BIG, F32)
    l_s[...] = jnp.zeros(l_s.shape, F32)
    acc_s[...] = jnp.zeros(acc_s.shape, F32)
    t = i * tq + lax.broadcasted_iota(jnp.int32, (1, tq), 1)
    selt = selt_ref[0, 0]

    def chunk(k0, width, diagonal):
        hit = _dot(e_ref[pl.ds(k0, width), :], selt)
        allowed = hit > 0.5
        if diagonal:
            allowed = allowed & (k0 + lax.broadcasted_iota(jnp.int32, (width, 1), 0) <= t)
        bias = jnp.where(allowed, 0.0, NEG_BIG)
        s = lax.dot_general(k_ref[0, 0, pl.ds(k0, width), :], q, NT_DIMS, preferred_element_type=F32)
        s = s + jnp.concatenate([bias] * n_rep, axis=1)
        m_old = m_s[...]
        m_new = jnp.maximum(m_old, jnp.max(s, axis=0, keepdims=True))
        p = jnp.exp2(s - m_new)
        alpha = jnp.exp2(m_old - m_new)
        l_s[...] = alpha * l_s[...] + jnp.sum(p, axis=0, keepdims=True)
        pv = lax.dot_general(v_ref[0, 0, pl.ds(k0, width), :], p.astype(BF16), TN_DIMS, preferred_element_type=F32)
        acc_s[...] = alpha * acc_s[...] + pv
        m_s[...] = m_new

    def body(c, carry):
        chunk(pl.multiple_of(c * tk, tk), tk, diagonal=False)
        return carry

    n_before = (i * tq) // tk
    rem = (i + 1) * tq - n_before * tk
    lax.fori_loop(0, n_before, body, 0)
    tail0 = pl.multiple_of(n_before * tk, tk)

    @pl.when(rem > tk // 2)
    def _():
        chunk(tail0, tk, diagonal=True)

    @pl.when(rem <= tk // 2)
    def _():
        chunk(tail0, tk // 2, diagonal=True)

    _store_heads_as_rows(acc_s[...] / jnp.maximum(l_s[...], F32_TINY), n_rep, o_ref)


def _win_attn_kernel(q_ref, k_ref, v_ref, o_ref, *, tq):
    n_rep, rows, dh = q_ref.shape[1:]
    span = WINDOW + tq
    for u in range(rows // tq):
        i = pl.program_id(2) * (rows // tq) + u
        q = q_ref[0, :, u * tq:(u + 1) * tq, :].reshape(n_rep * tq, dh)
        t = i * tq + lax.broadcasted_iota(jnp.int32, (1, tq), 1)
        k0 = pl.multiple_of(jnp.maximum(i * tq - WINDOW, 0), tq)
        kpos = k0 + lax.broadcasted_iota(jnp.int32, (span, 1), 0)
        bias = jnp.where((kpos <= t) & (t - kpos < WINDOW), 0.0, NEG_BIG)
        s = lax.dot_general(k_ref[0, 0, pl.ds(k0, span), :], q, NT_DIMS, preferred_element_type=F32)
        s = s + jnp.concatenate([bias] * n_rep, axis=1)
        p = jnp.exp2(s - jnp.max(s, axis=0, keepdims=True))
        l = jnp.sum(p, axis=0, keepdims=True)
        pv = lax.dot_general(v_ref[0, 0, pl.ds(k0, span), :], p.astype(BF16), TN_DIMS, preferred_element_type=F32)
        _store_heads_as_rows(pv / jnp.maximum(l, F32_TINY), n_rep, o_ref, row0=u * tq)


def _attention(q, kv, k_off, v_off, n_rep, selt=None, tq=128, tk=1024, win_tiles=4):
    B, H, S, Dh = q.shape
    G = H // n_rep
    rows = tq * win_tiles if selt is None else tq
    q_spec = pl.BlockSpec((1, n_rep, rows, Dh), lambda b, g, i: (b, g, i, 0))
    kv_specs = [pl.BlockSpec((1, 1, S, Dh), lambda b, g, i: (b, k_off + g, 0, 0)),
                pl.BlockSpec((1, 1, S, Dh), lambda b, g, i: (b, v_off + g, 0, 0))]
    common = dict(
        grid=(B, G, S // rows),
        out_specs=pl.BlockSpec((1, rows, n_rep * Dh), lambda b, g, i: (b, i, g)),
        out_shape=jax.ShapeDtypeStruct((B, S, H * Dh), F32),
        compiler_params=_cparams(("parallel", "parallel", "arbitrary")),
    )
    if selt is None:
        assert WINDOW + tq <= S
        return pl.pallas_call(functools.partial(_win_attn_kernel, tq=tq), in_specs=[q_spec] + kv_specs,
                              name="nsa_attn_win", **common)(q, kv, kv)
    n_slc = selt.shape[2]
    tk = min(tk, S)
    assert S % tk == 0 and tk % (2 * tq) == 0
    e = (jnp.arange(S, dtype=jnp.int32)[:, None] // SLC_BLOCK == jnp.arange(n_slc, dtype=jnp.int32)[None, :]).astype(BF16)
    return pl.pallas_call(
        functools.partial(_slc_attn_kernel, tk=tk),
        in_specs=[q_spec, pl.BlockSpec((1, 1, n_slc, tq), lambda b, g, i: (b, g, 0, i)),
                  pl.BlockSpec((S, n_slc), lambda b, g, i: (0, 0))] + kv_specs,
        scratch_shapes=[pltpu.VMEM((1, n_rep * tq), F32), pltpu.VMEM((1, n_rep * tq), F32),
                        pltpu.VMEM((Dh, n_rep * tq), F32)],
        name="nsa_attn_slc", **common)(q, selt, e, kv, kv)


def _gate_oproj_norm_kernel(oc_ref, os_ref, ow_ref, gate_ref, ex_ref, w_ref, x_ref, gt_ref, g_ref, b_ref, out_ref):
    g_parts = _split_bf16(gate_ref[0], 2)
    o = None
    for br, o_ref in enumerate((oc_ref, os_ref, ow_ref)):
        ge = _dot(g_parts[0], ex_ref[br]) + _dot(g_parts[1], ex_ref[br])
        o = ge * o_ref[0] if o is None else o + ge * o_ref[0]
    y = _dot(o.astype(BF16), w_ref[...])
    v = ALPHA * x_ref[0] + (1.0 + gt_ref[0]) * y
    out_ref[0] = _layer_norm_rows(v, g_ref[...], b_ref[...])


def _gate_oproj_norm(o_cmp, o_slc, o_win, gates, o_w, x, gate_t, ln_g, ln_b, tm=256):
    B, S, D = x.shape
    QD = o_w.shape[0]
    lane = jnp.arange(LANES)[:, None]
    head = jnp.arange(QD)[None, :] // NSA_HEAD_DIM
    expand = jnp.stack([(lane == head * NSA_N_BRANCH + br) for br in range(NSA_N_BRANCH)]).astype(BF16)
    return pl.pallas_call(
        _gate_oproj_norm_kernel,
        grid=(B, S // tm),
        in_specs=[_row_spec(tm, QD), _row_spec(tm, QD), _row_spec(tm, QD), _row_spec(tm, LANES),
                  _const_spec((NSA_N_BRANCH, LANES, QD)), _const_spec((QD, D)), _row_spec(tm, D),
                  _batch_vec_spec(D), _const_spec((1, D)), _const_spec((1, D))],
        out_specs=_row_spec(tm, D),
        out_shape=jax.ShapeDtypeStruct((B, S, D), F32),
        compiler_params=_cparams(("parallel", "parallel")),
        name="nsa_gate_oproj_norm",
    )(o_cmp, o_slc, o_win, gates, expand, o_w.astype(BF16), x, gate_t, ln_g.reshape(1, D), ln_b.reshape(1, D))


def _nsa_shared_kv(x, sc, sh, kv_w, cmp_pos, phi_k_w1, phi_k_w2, phi_v_w1, phi_v_w2, cos, sin):
    kc, ka = _kvproj(x, sc, sh, kv_w, cos, sin)
    kvc = _compress(kc, cmp_pos, jnp.stack([phi_k_w1, phi_v_w1]), jnp.stack([phi_k_w2, phi_v_w2]))
    return kvc, ka


def _nsa_layer(x, sh, sc, gate_t, q_w, o_w, kvc, ka, cos, sin, ln_g, ln_b):
    G = NSA_KV_HEADS
    q_dim = o_w.shape[0]
    n_rep = q_dim // NSA_HEAD_DIM // G
    q, gates = _qproj(x, sc, sh, q_w, q_dim, cos, sin)
    o_cmp, selt = _cmp_select(q, kvc, n_rep)
    o_slc = _attention(q, ka, 0, G, n_rep, selt=selt)
    o_win = _attention(q, ka, 2 * G, 3 * G, n_rep)
    return _gate_oproj_norm(o_cmp, o_slc, o_win, gates, o_w, x, gate_t, ln_g, ln_b)


def kernel(x, c, pos, ada_w, ada_b, ln_g, ln_b, ssm_in_w, ssm_conv_w, ssm_conv_b, ssm_dt_bias, ssm_a_log, ssm_d, ssm_norm_w, ssm_out_w, kv_ada_w, kv_ada_b, kv_w, cmp_pos, phi_k_w1, phi_k_w2, phi_v_w1, phi_v_w2, nsa_q_w, nsa_o_w, router_w, router_b, moe_w_up, moe_b_up, moe_w_down, moe_b_down):
    c_act = jax.nn.silu(c)
    cos, sin = _rope_tables(pos)
    shared = None
    for i in range(DEPTH):
        mod = c_act @ ada_w[i] + ada_b[i]
        sh_t, sc_t, g_t, sh_c, sc_c, g_c = [m[:, None, :] for m in jnp.split(mod, 6, axis=-1)]
        if i < N_A:
            x = _mamba_layer(x, sh_t, sc_t, g_t, ssm_in_w[i], ssm_conv_w[i], ssm_conv_b[i], ssm_dt_bias[i],
                             ssm_a_log[i], ssm_d[i], ssm_norm_w[i], ssm_out_w[i], ln_g[i, 0], ln_b[i, 0])
        else:
            j = i - N_A
            x = _nsa_layer(x, sh_t, sc_t, g_t, nsa_q_w[j], nsa_o_w[j], *shared, cos, sin, ln_g[i, 0], ln_b[i, 0])
        x = _moe_layer(x, sh_c, sc_c, g_c, i, router_w[i], router_b[i],
                       moe_w_up, moe_b_up, moe_w_down, moe_b_down, ln_g[i, 1], ln_b[i, 1])
        if i == N_A - 1:
            kv_sh, kv_sc = [m[:, None, :] for m in jnp.split(c_act @ kv_ada_w + kv_ada_b, 2, axis=-1)]
            shared = _nsa_shared_kv(x, kv_sc, kv_sh, kv_w, cmp_pos, phi_k_w1, phi_k_w2, phi_v_w1, phi_v_w2, cos, sin)
    return x
```

```python
import functools

import jax
import jax.numpy as jnp
import numpy as np
from jax import lax
from jax.experimental import pallas as pl
from jax.experimental.pallas import tpu as pltpu

F32 = jnp.float32
BF16 = jnp.bfloat16

DEPTH = 4
N_A = DEPTH // 2
ALPHA = (2.0 * DEPTH) ** 0.25
LN_EPS = 1e-5

SSM_HEAD_DIM = 64
SSM_GROUPS = 4
SSM_STATE = 128
CONV_WIDTH = 4
SSM_CHUNK = 128

NSA_HEAD_DIM = 64
NSA_KV_HEADS = 4
NSA_N_BRANCH = 3
CMP_BLOCK = 32
CMP_STRIDE = 16
SLC_BLOCK = 64
N_SELECT = 16
WINDOW = 512
ROPE_THETA = 10000.0
ATTN_SCALE = NSA_HEAD_DIM ** -0.5
FORCED_SCORE = 1e9

TOP_K = 4
SWIGLU_LIMIT = 7.0
SWIGLU_ALPHA = 1.702

VMEM_LIMIT_BYTES_V7X = 56 * 1024 * 1024
LANES = 128
SUBLANES = 8
NEG_BIG = -1e30
F32_TINY = float(np.finfo(np.float32).tiny)
LOG2_E = float(np.log2(np.e))
NT_DIMS = (((1,), (1,)), ((), ()))
TN_DIMS = (((0,), (0,)), ((), ()))


def _cparams(sem):
    return pltpu.CompilerParams(dimension_semantics=sem, vmem_limit_bytes=VMEM_LIMIT_BYTES_V7X)


def _dot(a, b):
    return jnp.dot(a, b, preferred_element_type=F32)


def _split_bf16(a, parts):
    out = []
    r = a
    for _ in range(parts):
        p = r.astype(BF16)
        out.append(p)
        r = r - p.astype(F32)
    return out


def _layer_norm_rows(v, g, b):
    mu = jnp.mean(v, axis=-1, keepdims=True)
    d = v - mu
    var = jnp.mean(d * d, axis=-1, keepdims=True)
    return d * lax.rsqrt(var + LN_EPS) * g + b


def _modulated_bf16(x_ref, sc_ref, sh_ref):
    return (x_ref[0] * (1.0 + sc_ref[0]) + sh_ref[0]).astype(BF16)


def _row_spec(tm, width):
    return pl.BlockSpec((1, tm, width), lambda b, s: (b, s, 0))


def _batch_vec_spec(width):
    return pl.BlockSpec((1, 1, width), lambda b, s: (b, 0, 0))


def _const_spec(shape):
    nd = len(shape)
    return pl.BlockSpec(shape, lambda b, s: (0,) * nd)


def _router_kernel(x_ref, sc_ref, sh_ref, rw_ref, rb_ref, h_ref, idx_ref, gate_ref, cnt_ref, running):
    first_step = (pl.program_id(0) == 0) & (pl.program_id(1) == 0)

    @pl.when(first_step)
    def _():
        running[...] = jnp.zeros(running.shape, F32)

    hb = _modulated_bf16(x_ref, sc_ref, sh_ref)
    h_ref[0] = hb
    logits = _dot(hb, rw_ref[...]) + rb_ref[...]
    tm, n_exp = logits.shape
    lane_e = lax.broadcasted_iota(jnp.int32, logits.shape, 1)
    lane_o = lax.broadcasted_iota(jnp.int32, idx_ref.shape[1:], 1)
    vals, idxs = [], []
    l = logits
    for _ in range(TOP_K):
        m = jnp.max(l, axis=-1, keepdims=True)
        idx = jnp.min(jnp.where(l == m, lane_e, n_exp), axis=-1, keepdims=True)
        vals.append(m)
        idxs.append(idx)
        l = jnp.where(lane_e == idx, -jnp.inf, l)
    es = [jnp.exp(v - vals[0]) for v in vals]
    denom = es[0] + es[1] + es[2] + es[3]
    idx_out = jnp.zeros(idx_ref.shape[1:], jnp.int32)
    gate_out = jnp.zeros(gate_ref.shape[1:], F32)
    chosen = sum(jnp.where(lane_e == ix, 1.0, 0.0) for ix in idxs)
    row = lax.broadcasted_iota(jnp.int32, (tm, tm), 0)
    col = lax.broadcasted_iota(jnp.int32, (tm, tm), 1)
    earlier = jnp.where(col < row, 1.0, 0.0).astype(BF16)
    before = _dot(earlier, chosen.astype(BF16)) + running[...]
    for k in range(TOP_K):
        rank_k = jnp.sum(jnp.where(lane_e == idxs[k], before, 0.0), axis=-1, keepdims=True).astype(jnp.int32)
        idx_out = jnp.where(lane_o == k, idxs[k], idx_out)
        idx_out = jnp.where(lane_o == TOP_K + k, rank_k, idx_out)
        gate_out = jnp.where(lane_o == k, es[k] / denom, gate_out)
    idx_ref[0] = idx_out
    gate_ref[0] = gate_out
    running[...] = running[...] + jnp.sum(chosen, axis=0, keepdims=True)
    cnt_ref[...] = jnp.broadcast_to(running[...], cnt_ref.shape)


def _router(x, sc, sh, rw, rb, tm=512):
    B, S, D = x.shape
    E = rw.shape[-1]
    return pl.pallas_call(
        _router_kernel,
        grid=(B, S // tm),
        in_specs=[_row_spec(tm, D), _batch_vec_spec(D), _batch_vec_spec(D), _const_spec((D, E)), _const_spec((1, E))],
        out_specs=[_row_spec(tm, D), _row_spec(tm, LANES), _row_spec(tm, LANES), _const_spec((SUBLANES, E))],
        out_shape=[
            jax.ShapeDtypeStruct((B, S, D), BF16),
            jax.ShapeDtypeStruct((B, S, LANES), jnp.int32),
            jax.ShapeDtypeStruct((B, S, LANES), F32),
            jax.ShapeDtypeStruct((SUBLANES, E), F32),
        ],
        scratch_shapes=[pltpu.VMEM((1, E), F32)],
        compiler_params=_cparams(("arbitrary", "arbitrary")),
        name="moe_router",
    )(x, sc, sh, rw.astype(BF16), rb.reshape(1, E))


def _ffn_kernel(blk_e_ref, first_ref, nused_ref, x_ref, wu_ref, bu_ref, wd_ref, bd_ref, o_ref, wu_s, wd_s):
    i = pl.program_id(0)
    d_exp = wd_s.shape[0]

    @pl.when(i < nused_ref[0])
    def _():
        @pl.when(first_ref[i] == 1)
        def _():
            wu_s[...] = wu_ref[0].astype(BF16)
            wd_s[...] = wd_ref[0].astype(BF16)

        u = _dot(x_ref[...], wu_s[...]) + bu_ref[0]
        glu = jnp.minimum(u[:, :d_exp], SWIGLU_LIMIT)
        lin = jnp.clip(u[:, d_exp:], -SWIGLU_LIMIT, SWIGLU_LIMIT)
        act = glu * jax.nn.sigmoid(SWIGLU_ALPHA * glu) * (lin + 1.0)
        o_ref[...] = (_dot(act.astype(BF16), wd_s[...]) + bd_ref[0]).astype(o_ref.dtype)

    @pl.when(i >= nused_ref[0])
    def _():
        o_ref[...] = jnp.zeros_like(o_ref)


def _expert_ffn(xg, blk_e, first, nused, layer, w_up, b_up, w_down, b_down, tm):
    R, D = xg.shape
    n_layers, E, _, N2 = w_up.shape
    d_exp = w_down.shape[2]
    n_blk = R // tm
    grid_spec = pltpu.PrefetchScalarGridSpec(
        num_scalar_prefetch=3,
        grid=(n_blk,),
        in_specs=[
            pl.BlockSpec((tm, D), lambda i, be, fi, nu: (i, 0)),
            pl.BlockSpec((None, 1, D, N2), lambda i, be, fi, nu: (layer, be[i], 0, 0)),
            pl.BlockSpec((None, 1, 1, N2), lambda i, be, fi, nu: (layer, be[i], 0, 0)),
            pl.BlockSpec((None, 1, d_exp, D), lambda i, be, fi, nu: (layer, be[i], 0, 0)),
            pl.BlockSpec((None, 1, 1, D), lambda i, be, fi, nu: (layer, be[i], 0, 0)),
        ],
        out_specs=pl.BlockSpec((tm, D), lambda i, be, fi, nu: (i, 0)),
        scratch_shapes=[pltpu.VMEM((D, N2), BF16), pltpu.VMEM((d_exp, D), BF16)],
    )
    return pl.pallas_call(
        _ffn_kernel,
        grid_spec=grid_spec,
        out_shape=jax.ShapeDtypeStruct((R, D), BF16),
        compiler_params=_cparams(("arbitrary",)),
        name="moe_expert_ffn",
    )(blk_e, first, nused, xg, w_up, b_up.reshape(n_layers, E, 1, N2), w_down, b_down.reshape(n_layers, E, 1, D))


def _combine_norm_kernel(x_ref, yg_ref, gate_ref, gc_ref, g_ref, b_ref, o_ref):
    gate = gate_ref[0]
    y = yg_ref[0, 0] * gate[:, 0:1]
    for k in range(1, TOP_K):
        y = y + yg_ref[k, 0] * gate[:, k:k + 1]
    v = ALPHA * x_ref[0] + (1.0 + gc_ref[0]) * y
    o_ref[0] = _layer_norm_rows(v, g_ref[...], b_ref[...])


def _combine_norm(x, yg, gate, gate_c, ln_g, ln_b, tm=256):
    B, S, D = x.shape
    return pl.pallas_call(
        _combine_norm_kernel,
        grid=(B, S // tm),
        in_specs=[
            _row_spec(tm, D),
            pl.BlockSpec((TOP_K, 1, tm, D), lambda b, s: (0, b, s, 0)),
            _row_spec(tm, LANES),
            _batch_vec_spec(D),
            _const_spec((1, D)),
            _const_spec((1, D)),
        ],
        out_specs=_row_spec(tm, D),
        out_shape=jax.ShapeDtypeStruct((B, S, D), F32),
        compiler_params=_cparams(("parallel", "parallel")),
        name="moe_combine_norm",
    )(x, yg, gate, gate_c, ln_g.reshape(1, D), ln_b.reshape(1, D))


def _moe_layer(x, sh, sc, gate_c, layer, router_w, router_b, w_up, b_up, w_down, b_down, ln_g, ln_b, row_block=256):
    B, S, D = x.shape
    T = B * S
    E = router_w.shape[-1]
    h, route, gate, cnt = _router(x, sc, sh, router_w, router_b)
    A = T * TOP_K
    eid = route[:, :, :TOP_K].reshape(A)
    rank = route[:, :, TOP_K:2 * TOP_K].reshape(A)
    counts = cnt[0].astype(jnp.int32)
    padded = (counts + row_block - 1) // row_block * row_block
    off = jnp.cumsum(counts) - counts
    cum_padded = jnp.cumsum(padded)
    poff = cum_padded - padded
    pos = poff[eid] + rank
    R_rows = A + E * row_block
    n_blk = R_rows // row_block
    nused = (cum_padded[-1] // row_block).astype(jnp.int32)
    blk_start = jnp.arange(n_blk, dtype=jnp.int32) * row_block
    blk_e = jnp.minimum(jnp.sum(cum_padded[None, :] <= blk_start[:, None], axis=1), E - 1).astype(jnp.int32)
    last_e = blk_e[jnp.maximum(nused - 1, 0)]
    blk_e = jnp.where(jnp.arange(n_blk) < nused, blk_e, last_e)
    first = jnp.concatenate([jnp.ones((1,), jnp.int32), (blk_e[1:] != blk_e[:-1]).astype(jnp.int32)])
    order = jnp.argsort(eid).astype(jnp.int32)
    row = jnp.arange(R_rows, dtype=jnp.int32)
    per_row = lambda table: jnp.repeat(table[blk_e], row_block)
    j = row - per_row(poff)
    row_tok = jnp.where(j < per_row(counts), order[jnp.minimum(per_row(off) + j, A - 1)] // TOP_K, row % T)
    xg = h.reshape(T, D)[row_tok]
    y = _expert_ffn(xg, blk_e, first, nused.reshape(1), layer, w_up, b_up, w_down, b_down, row_block)
    yg = y[pos.reshape(T, TOP_K).T].reshape(TOP_K, B, S, D)
    return _combine_norm(x, yg, gate, gate_c, ln_g, ln_b)


def _outproj_norm_kernel(y_ref, w_ref, x_ref, gt_ref, g_ref, b_ref, o_ref):
    y = _dot(y_ref[0], w_ref[...])
    v = ALPHA * x_ref[0] + (1.0 + gt_ref[0]) * y
    o_ref[0] = _layer_norm_rows(v, g_ref[...], b_ref[...])


def _outproj_norm(y, w, x, gate_t, ln_g, ln_b, tm=512):
    B, S, D = x.shape
    K = y.shape[-1]
    return pl.pallas_call(
        _outproj_norm_kernel,
        grid=(B, S // tm),
        in_specs=[_row_spec(tm, K), _const_spec((K, D)), _row_spec(tm, D), _batch_vec_spec(D),
                  _const_spec((1, D)), _const_spec((1, D))],
        out_specs=_row_spec(tm, D),
        out_shape=jax.ShapeDtypeStruct((B, S, D), F32),
        compiler_params=_cparams(("parallel", "parallel")),
        name="outproj_norm",
    )(y, w.astype(BF16), x, gate_t, ln_g.reshape(1, D), ln_b.reshape(1, D))


def _ssm_inproj_kernel(x_ref, sc_ref, sh_ref, wz_ref, wx_ref, wdt_ref, cw_ref, cb_ref, dtb_ref,
                       z_ref, xs_ref, bm_ref, cm_ref, dt_ref, xbuf, *, col_chunk):
    tm = x_ref.shape[1]
    d_inner = xs_ref.shape[-1]
    bc = bm_ref.shape[-1]
    conv_dim = wx_ref.shape[-1]
    halo = SUBLANES

    @pl.when(pl.program_id(1) == 0)
    def _():
        xbuf[0:halo, :] = jnp.zeros((halo, conv_dim), F32)

    hb = _modulated_bf16(x_ref, sc_ref, sh_ref)
    z_ref[0] = _dot(hb, wz_ref[...])
    dt_ref[0] = jax.nn.softplus(_dot(hb, wdt_ref[...]) + dtb_ref[...])
    for c0 in range(0, conv_dim, col_chunk):
        cs = slice(c0, c0 + col_chunk)
        xbuf[halo:halo + tm, cs] = _dot(hb, wx_ref[:, cs])
        acc = cb_ref[:, cs] + cw_ref[CONV_WIDTH - 1:CONV_WIDTH, cs] * xbuf[halo:halo + tm, cs]
        for w in range(CONV_WIDTH - 1):
            lo = halo - (CONV_WIDTH - 1) + w
            acc = acc + cw_ref[w:w + 1, cs] * xbuf[lo:lo + tm, cs]
        y = acc * jax.nn.sigmoid(acc)
        if c0 + col_chunk <= d_inner:
            xs_ref[0, :, cs] = y
        elif c0 + col_chunk <= d_inner + bc:
            bm_ref[0, :, c0 - d_inner:c0 - d_inner + col_chunk] = y.astype(BF16)
        else:
            cm_ref[0, :, c0 - d_inner - bc:c0 - d_inner - bc + col_chunk] = y.astype(BF16)
        xbuf[0:halo, cs] = xbuf[tm:tm + halo, cs]


def _ssm_inproj(x, sc, sh, in_w, conv_w, conv_b, dt_bias, d_inner, bc_dim, tm=256, col_chunk=512):
    B, S, D = x.shape
    conv_dim = d_inner + 2 * bc_dim
    heads = dt_bias.shape[0]
    wz = in_w[:, :d_inner].astype(BF16)
    wx = in_w[:, d_inner:d_inner + conv_dim].astype(BF16)
    wdt = jnp.pad(in_w[:, d_inner + conv_dim:], ((0, 0), (0, LANES - heads))).astype(BF16)
    dtb = jnp.pad(dt_bias, (0, LANES - heads)).reshape(1, LANES)
    assert d_inner % col_chunk == 0 and bc_dim % col_chunk == 0
    return pl.pallas_call(
        functools.partial(_ssm_inproj_kernel, col_chunk=col_chunk),
        grid=(B, S // tm),
        in_specs=[_row_spec(tm, D), _batch_vec_spec(D), _batch_vec_spec(D),
                  _const_spec((D, d_inner)), _const_spec((D, conv_dim)), _const_spec((D, LANES)),
                  _const_spec((CONV_WIDTH, conv_dim)), _const_spec((1, conv_dim)), _const_spec((1, LANES))],
        out_specs=[_row_spec(tm, d_inner), _row_spec(tm, d_inner), _row_spec(tm, bc_dim), _row_spec(tm, bc_dim),
                   _row_spec(tm, LANES)],
        out_shape=[
            jax.ShapeDtypeStruct((B, S, d_inner), F32),
            jax.ShapeDtypeStruct((B, S, d_inner), F32),
            jax.ShapeDtypeStruct((B, S, bc_dim), BF16),
            jax.ShapeDtypeStruct((B, S, bc_dim), BF16),
            jax.ShapeDtypeStruct((B, S, LANES), F32),
        ],
        scratch_shapes=[pltpu.VMEM((SUBLANES + tm + SUBLANES, conv_dim), F32)],
        compiler_params=_cparams(("parallel", "arbitrary")),
        name="ssm_inproj_conv",
    )(x, sc, sh, wz, wx, wdt, conv_w, conv_b.reshape(1, conv_dim), dtb)


def _ssd_kernel(xs_ref, bm_ref, cm_ref, z_ref, dt_ref, dtt_ref, aneg_ref, anegt_ref, dskip_ref, nw_ref, ex_ref,
                y_ref, state, *, heads_per_group):
    L = xs_ref.shape[1]
    P = SSM_HEAD_DIM
    N = SSM_STATE
    n_groups = bm_ref.shape[-1] // N
    gw = heads_per_group * P

    @pl.when(pl.program_id(1) == 0)
    def _():
        state[...] = jnp.zeros(state.shape, F32)

    row = lax.broadcasted_iota(jnp.int32, (L, L), 0)
    col = lax.broadcasted_iota(jnp.int32, (L, L), 1)
    lower = row >= col
    tri = jnp.where(lower, 1.0, 0.0).astype(BF16)
    tri_t = jnp.where(row <= col, 1.0, 0.0).astype(BF16)
    dt = dt_ref[0]
    a_cs = sum(_dot(tri, p) for p in _split_bf16(dt * aneg_ref[...], 3))
    a_cs_t = sum(_dot(p, tri_t) for p in _split_bf16(dtt_ref[0] * anegt_ref[...], 3))
    dt_parts = _split_bf16(dt, 2)
    acs_parts = _split_bf16(a_cs, 3)
    lane = lax.broadcasted_iota(jnp.int32, (1, 2 * P), 1)
    first_head = lane < P
    for g in range(n_groups):
        gs = slice(g * gw, (g + 1) * gw)
        ex = ex_ref[g]
        dt_x = sum(_dot(p, ex) for p in dt_parts)
        acs_x = sum(_dot(p, ex) for p in acs_parts)
        end_x = acs_x[L - 1:L, :]
        bg = bm_ref[0, :, g * N:(g + 1) * N]
        cg = cm_ref[0, :, g * N:(g + 1) * N]
        xs_g = xs_ref[0, :, gs]
        xdt = xs_g * dt_x
        xdt_b = xdt.astype(BF16)
        s_g = state[g]
        y = _dot(cg, s_g.astype(BF16)) * jnp.exp(acs_x) + xs_g * dskip_ref[:, gs]
        to_end = (xdt * jnp.exp(end_x - acs_x)).astype(BF16)
        state[g] = s_g * jnp.exp(end_x) + lax.dot_general(bg, to_end, TN_DIMS, preferred_element_type=F32)
        cb = lax.dot_general(cg, bg, NT_DIMS, preferred_element_type=F32)
        diag = []
        for u in range(heads_per_group // 2):
            ms = []
            for h in (g * heads_per_group + 2 * u, g * heads_per_group + 2 * u + 1):
                seg = a_cs[:, h:h + 1] - a_cs_t[h:h + 1, :]
                ms.append((cb * jnp.exp(jnp.where(lower, seg, NEG_BIG))).astype(BF16))
            xp = xdt_b[:, 2 * u * P:(2 * u + 2) * P]
            zero = jnp.zeros_like(xp)
            rhs = jnp.concatenate([jnp.where(first_head, xp, zero), jnp.where(first_head, zero, xp)], axis=0)
            diag.append(_dot(jnp.concatenate(ms, axis=1), rhs))
        y = y + jnp.concatenate(diag, axis=1)
        zz = z_ref[0, :, gs]
        y = y * (zz * jax.nn.sigmoid(zz))
        scale = lax.rsqrt(jnp.sum(y * y, axis=-1, keepdims=True) / gw + LN_EPS)
        y_ref[0, :, gs] = (y * scale * nw_ref[:, gs]).astype(BF16)


def _ssd(xs, bm, cm, z, dt, a_log, d_skip, norm_w):
    B, S, d_inner = xs.shape
    heads = a_log.shape[0]
    bc_dim = bm.shape[-1]
    L = SSM_CHUNK
    a_neg = -jnp.exp(a_log)
    dt_t = jnp.swapaxes(dt[:, :, :heads], 1, 2)
    aneg = jnp.pad(a_neg, (0, LANES - heads)).reshape(1, LANES)
    anegt = jnp.broadcast_to(a_neg[:, None], (heads, L))
    dskip = jnp.repeat(d_skip, SSM_HEAD_DIM).reshape(1, d_inner)
    hpg = heads // SSM_GROUPS
    gw = hpg * SSM_HEAD_DIM
    head_of_lane = jnp.arange(SSM_GROUPS)[:, None, None] * hpg + jnp.arange(gw)[None, None, :] // SSM_HEAD_DIM
    expand = (jnp.arange(LANES)[None, :, None] == head_of_lane).astype(BF16)
    return pl.pallas_call(
        functools.partial(_ssd_kernel, heads_per_group=hpg),
        grid=(B, S // L),
        in_specs=[_row_spec(L, d_inner), _row_spec(L, bc_dim), _row_spec(L, bc_dim), _row_spec(L, d_inner),
                  _row_spec(L, LANES), pl.BlockSpec((1, heads, L), lambda b, c: (b, 0, c)),
                  _const_spec((1, LANES)), _const_spec((heads, L)), _const_spec((1, d_inner)), _const_spec((1, d_inner)),
                  _const_spec((SSM_GROUPS, LANES, gw))],
        out_specs=_row_spec(L, d_inner),
        out_shape=jax.ShapeDtypeStruct((B, S, d_inner), BF16),
        scratch_shapes=[pltpu.VMEM((SSM_GROUPS, SSM_STATE, gw), F32)],
        compiler_params=_cparams(("parallel", "arbitrary")),
        name="ssd_chunk_scan",
    )(xs, bm, cm, z, dt, dt_t, aneg, anegt, dskip, norm_w.reshape(1, d_inner), expand)


def _mamba_layer(x, sh, sc, gate_t, in_w, conv_w, conv_b, dt_bias, a_log, d_skip, norm_w, out_w, ln_g, ln_b):
    d_inner = out_w.shape[0]
    bc_dim = SSM_GROUPS * SSM_STATE
    z, xs, bm, cm, dt = _ssm_inproj(x, sc, sh, in_w, conv_w, conv_b, dt_bias, d_inner, bc_dim)
    y = _ssd(xs, bm, cm, z, dt, a_log, d_skip, norm_w)
    return _outproj_norm(y, out_w, x, gate_t, ln_g, ln_b)


def _rope_tables(pos):
    half = NSA_HEAD_DIM // 2
    inv = ROPE_THETA ** (-jnp.arange(half, dtype=F32) / half)
    ang = pos.astype(F32)[..., None] * inv
    cos, sin = jnp.cos(ang), jnp.sin(ang)
    return jnp.concatenate([cos, cos, cos, cos], axis=-1), jnp.concatenate([-sin, sin, -sin, sin], axis=-1)


def _rope_slab(x, cos, sin):
    half = NSA_HEAD_DIM // 2
    lane = lax.broadcasted_iota(jnp.int32, (1, LANES), 1)
    first_half = (lane % NSA_HEAD_DIM) < half
    partner = jnp.where(first_half, pltpu.roll(x, LANES - half, 1), pltpu.roll(x, half, 1))
    return x * cos + partner * sin


def _kvproj_kernel(x_ref, sc_ref, sh_ref, w_ref, cos_ref, sin_ref, kc_ref, ka_ref):
    hb = _modulated_bf16(x_ref, sc_ref, sh_ref)
    cos, sin = cos_ref[0], sin_ref[0]
    dh = NSA_HEAD_DIM
    heads_per_slab = LANES // dh
    n_slab = w_ref.shape[-1] // LANES
    slabs_per_branch = NSA_KV_HEADS // heads_per_slab
    for j in range(n_slab):
        slab = _dot(hb, w_ref[:, j * LANES:(j + 1) * LANES])
        branch = j // slabs_per_branch
        if branch % 2 == 0:
            slab = _rope_slab(slab, cos, sin)
        for u in range(heads_per_slab):
            head = (j % slabs_per_branch) * heads_per_slab + u
            piece = slab[:, u * dh:(u + 1) * dh]
            if branch < 2:
                kc_ref[0, branch * NSA_KV_HEADS + head] = piece
            else:
                ka_ref[0, (branch - 2) * NSA_KV_HEADS + head] = piece.astype(BF16)


def _kvproj(x, sc, sh, kv_w, cos, sin, tm=512):
    B, S, D = x.shape
    N = kv_w.shape[-1]
    n_cmp_arr = 2 * NSA_KV_HEADS
    n_att_arr = 4 * NSA_KV_HEADS
    return pl.pallas_call(
        _kvproj_kernel,
        grid=(B, S // tm),
        in_specs=[_row_spec(tm, D), _batch_vec_spec(D), _batch_vec_spec(D), _const_spec((D, N)),
                  _row_spec(tm, LANES), _row_spec(tm, LANES)],
        out_specs=[pl.BlockSpec((1, n_cmp_arr, tm, NSA_HEAD_DIM), lambda b, s: (b, 0, s, 0)),
                   pl.BlockSpec((1, n_att_arr, tm, NSA_HEAD_DIM), lambda b, s: (b, 0, s, 0))],
        out_shape=[jax.ShapeDtypeStruct((B, n_cmp_arr, S, NSA_HEAD_DIM), F32),
                   jax.ShapeDtypeStruct((B, n_att_arr, S, NSA_HEAD_DIM), BF16)],
        compiler_params=_cparams(("parallel", "parallel")),
        name="nsa_kv_proj",
    )(x, sc, sh, kv_w.astype(BF16), cos, sin)


def _compress_kernel(k_ref, pos_ref, w1_ref, w2_ref, o_ref):
    ks = k_ref[0, 0]
    n_sub, half_in = ks.shape
    top = _dot((ks + pos_ref[0:1, :]).astype(BF16), w1_ref[0, :half_in, :])
    bot = _dot((ks + pos_ref[1:2, :]).astype(BF16), w1_ref[0, half_in:, :])
    hid = jax.nn.gelu(top + pltpu.roll(bot, n_sub - 1, 0))
    o_ref[0, 0] = _dot(hid.astype(BF16), w2_ref[0]).astype(o_ref.dtype)


def _compress(kc, cmp_pos, w1s, w2s):
    B, A, S, Dh = kc.shape
    n_sub = S // CMP_STRIDE
    half_in = CMP_STRIDE * Dh
    hid = w1s.shape[-1]
    ksub = kc.reshape(B, A, n_sub, half_in)
    pos2 = cmp_pos.reshape(2, half_in)
    return pl.pallas_call(
        _compress_kernel,
        grid=(B, A),
        in_specs=[pl.BlockSpec((1, 1, n_sub, half_in), lambda b, a: (b, a, 0, 0)),
                  pl.BlockSpec((2, half_in), lambda b, a: (0, 0)),
                  pl.BlockSpec((1, 2 * half_in, hid), lambda b, a: (a // NSA_KV_HEADS, 0, 0)),
                  pl.BlockSpec((1, hid, Dh), lambda b, a: (a // NSA_KV_HEADS, 0, 0))],
        out_specs=pl.BlockSpec((1, 1, n_sub, Dh), lambda b, a: (b, a, 0, 0)),
        out_shape=jax.ShapeDtypeStruct((B, A, n_sub, Dh), BF16),
        compiler_params=_cparams(("parallel", "parallel")),
        name="nsa_compress",
    )(ksub, pos2, w1s.astype(BF16), w2s.astype(BF16))


def _qproj_kernel(x_ref, sc_ref, sh_ref, wq_ref, wg_ref, cos_ref, sin_ref, q_ref, g_ref):
    hb = _modulated_bf16(x_ref, sc_ref, sh_ref)
    cos, sin = cos_ref[0], sin_ref[0]
    dh = NSA_HEAD_DIM
    heads_per_slab = LANES // dh
    for j in range(wq_ref.shape[-1] // LANES):
        slab = _rope_slab(_dot(hb, wq_ref[:, j * LANES:(j + 1) * LANES]), cos, sin) * (ATTN_SCALE * LOG2_E)
        for u in range(heads_per_slab):
            q_ref[0, j * heads_per_slab + u] = slab[:, u * dh:(u + 1) * dh].astype(BF16)
    g_ref[0] = jax.nn.sigmoid(_dot(hb, wg_ref[...]))


def _qproj(x, sc, sh, q_w, q_dim, cos, sin, tm=512):
    B, S, D = x.shape
    heads = q_dim // NSA_HEAD_DIM
    n_gate = q_w.shape[-1] - q_dim
    wq = q_w[:, :q_dim].astype(BF16)
    wg = jnp.pad(q_w[:, q_dim:], ((0, 0), (0, LANES - n_gate))).astype(BF16)
    return pl.pallas_call(
        _qproj_kernel,
        grid=(B, S // tm),
        in_specs=[_row_spec(tm, D), _batch_vec_spec(D), _batch_vec_spec(D), _const_spec((D, q_dim)),
                  _const_spec((D, LANES)), _row_spec(tm, LANES), _row_spec(tm, LANES)],
        out_specs=[pl.BlockSpec((1, heads, tm, NSA_HEAD_DIM), lambda b, s: (b, 0, s, 0)), _row_spec(tm, LANES)],
        out_shape=[jax.ShapeDtypeStruct((B, heads, S, NSA_HEAD_DIM), BF16),
                   jax.ShapeDtypeStruct((B, S, LANES), F32)],
        compiler_params=_cparams(("parallel", "parallel")),
        name="nsa_q_proj",
    )(x, sc, sh, wq, wg, cos, sin)


def _store_heads_as_rows(o_t, n_rep, o_ref, row0=0):
    dh = o_t.shape[0]
    tq = o_t.shape[1] // n_rep
    for r in range(0, n_rep, 2):
        pair = jnp.concatenate([o_t[:, r * tq:(r + 1) * tq], o_t[:, (r + 1) * tq:(r + 2) * tq]], axis=0)
        o_ref[0, row0:row0 + tq, r * dh:(r + 2) * dh] = pair.T


def _cmp_select_kernel(q_ref, k_ref, v_ref, ovt_ref, o_ref, selt_ref):
    i = pl.program_id(2)
    n_rep, tq, dh = q_ref.shape[1:]
    n_cmp = k_ref.shape[2]
    n_slc = ovt_ref.shape[0]
    q = q_ref[0].reshape(n_rep * tq, dh)
    t = i * tq + lax.broadcasted_iota(jnp.int32, (1, tq), 1)
    blk_end = lax.broadcasted_iota(jnp.int32, (n_cmp, 1), 0) * CMP_STRIDE + (CMP_BLOCK - 1)
    mask = jnp.concatenate([blk_end <= t] * n_rep, axis=1)
    s = lax.dot_general(k_ref[0, 0], q, NT_DIMS, preferred_element_type=F32)
    s = jnp.where(mask, s, NEG_BIG)
    m = jnp.max(s, axis=0, keepdims=True)
    p = jnp.where(mask, jnp.exp2(s - m), 0.0)
    p = p / jnp.maximum(jnp.sum(p, axis=0, keepdims=True), F32_TINY)
    pb = p.astype(BF16)
    _store_heads_as_rows(lax.dot_general(v_ref[0, 0], pb, TN_DIMS, preferred_element_type=F32), n_rep, o_ref)
    imp = _dot(ovt_ref[...], pb[:, 0:tq])
    for r in range(1, n_rep):
        imp = imp + _dot(ovt_ref[...], pb[:, r * tq:(r + 1) * tq])
    t_blk = t // SLC_BLOCK
    j = lax.broadcasted_iota(jnp.int32, (n_slc, 1), 0)
    forced = (j == 0) | (j == t_blk) | (j == t_blk - 1)
    valid = j <= t_blk
    score = jnp.where(valid, jnp.where(forced, FORCED_SCORE, imp), -jnp.inf)
    n_tiles = n_slc // SUBLANES
    tiles = [score[r * SUBLANES:(r + 1) * SUBLANES, :] for r in range(n_tiles)]
    j_tiles = [j[r * SUBLANES:(r + 1) * SUBLANES, :] for r in range(n_tiles)]
    ranks = [jnp.zeros((SUBLANES, tq), jnp.int32) for _ in range(n_tiles)]
    for jp in range(n_slc):
        c = jnp.broadcast_to(score[jp:jp + 1, :], (SUBLANES, tq))
        for r in range(n_tiles):
            if jp < r * SUBLANES:
                beats = (c >= tiles[r]).astype(jnp.int32)
            elif jp >= (r + 1) * SUBLANES:
                beats = (c > tiles[r]).astype(jnp.int32)
            else:
                beats = jnp.where(j_tiles[r] > jp, (c >= tiles[r]).astype(jnp.int32), (c > tiles[r]).astype(jnp.int32))
            ranks[r] = ranks[r] + beats
    rank = jnp.concatenate(ranks, axis=0)
    n_sel = min(N_SELECT, n_slc)
    selt_ref[0, 0] = jnp.where((rank < n_sel) & valid, 1.0, 0.0).astype(selt_ref.dtype)


def _cmp_select(q, kvc, n_rep, tq=512):
    B, H, S, Dh = q.shape
    G = H // n_rep
    n_cmp = kvc.shape[2]
    n_slc = S // SLC_BLOCK
    cs = jnp.arange(n_cmp)[None, :] * CMP_STRIDE
    js = jnp.arange(n_slc)[:, None] * SLC_BLOCK
    ovt = jnp.maximum(jnp.minimum(cs + CMP_BLOCK, js + SLC_BLOCK) - jnp.maximum(cs, js), 0).astype(F32) / CMP_BLOCK
    return pl.pallas_call(
        _cmp_select_kernel,
        grid=(B, G, S // tq),
        in_specs=[pl.BlockSpec((1, n_rep, tq, Dh), lambda b, g, i: (b, g, i, 0)),
                  pl.BlockSpec((1, 1, n_cmp, Dh), lambda b, g, i: (b, g, 0, 0)),
                  pl.BlockSpec((1, 1, n_cmp, Dh), lambda b, g, i: (b, G + g, 0, 0)),
                  pl.BlockSpec((n_slc, n_cmp), lambda b, g, i: (0, 0))],
        out_specs=[pl.BlockSpec((1, tq, n_rep * Dh), lambda b, g, i: (b, i, g)),
                   pl.BlockSpec((1, 1, n_slc, tq), lambda b, g, i: (b, g, 0, i))],
        out_shape=[jax.ShapeDtypeStruct((B, S, H * Dh), F32), jax.ShapeDtypeStruct((B, G, n_slc, S), BF16)],
        compiler_params=_cparams(("parallel", "parallel", "parallel")),
        name="nsa_cmp_select",
    )(q, kvc, kvc, ovt.astype(BF16))


def _slc_attn_kernel(q_ref, selt_ref, e_ref, k_ref, v_ref, o_ref, m_s, l_s, acc_s, *, tk):
    i = pl.program_id(2)
    n_rep, tq, dh = q_ref.shape[1:]
    q = q_ref[0].reshape(n_rep * tq, dh)
    m_s[...] = jnp.full(m_s.shape, NEG_BIG, F32)
    l_s[...] = jnp.zeros(l_s.shape, F32)
    acc_s[...] = jnp.zeros(acc_s.shape, F32)
    t = i * tq + lax.broadcasted_iota(jnp.int32, (1, tq), 1)
    selt = selt_ref[0, 0]

    def chunk(k0, width, diagonal):
        hit = _dot(e_ref[pl.ds(k0, width), :], selt)
        allowed = hit > 0.5
        if diagonal:
            allowed = allowed & (k0 + lax.broadcasted_iota(jnp.int32, (width, 1), 0) <= t)
        bias = jnp.where(allowed, 0.0, NEG_BIG)
        s = lax.dot_general(k_ref[0, 0, pl.ds(k0, width), :], q, NT_DIMS, preferred_element_type=F32)
        s = s + jnp.concatenate([bias] * n_rep, axis=1)
        m_old = m_s[...]
        m_new = jnp.maximum(m_old, jnp.max(s, axis=0, keepdims=True))
        p = jnp.exp2(s - m_new)
        alpha = jnp.exp2(m_old - m_new)
        l_s[...] = alpha * l_s[...] + jnp.sum(p, axis=0, keepdims=True)
        pv = lax.dot_general(v_ref[0, 0, pl.ds(k0, width), :], p.astype(BF16), TN_DIMS, preferred_element_type=F32)
        acc_s[...] = alpha * acc_s[...] + pv
        m_s[...] = m_new

    def body(c, carry):
        chunk(pl.multiple_of(c * tk, tk), tk, diagonal=False)
        return carry

    n_before = (i * tq) // tk
    rem = (i + 1) * tq - n_before * tk
    lax.fori_loop(0, n_before, body, 0)
    tail0 = pl.multiple_of(n_before * tk, tk)

    @pl.when(rem > tk // 2)
    def _():
        chunk(tail0, tk, diagonal=True)

    @pl.when(rem <= tk // 2)
    def _():
        chunk(tail0, tk // 2, diagonal=True)

    _store_heads_as_rows(acc_s[...] / jnp.maximum(l_s[...], F32_TINY), n_rep, o_ref)


def _win_attn_kernel(q_ref, k_ref, v_ref, o_ref, *, tq):
    n_rep, rows, dh = q_ref.shape[1:]
    span = WINDOW + tq
    for u in range(rows // tq):
        i = pl.program_id(2) * (rows // tq) + u
        q = q_ref[0, :, u * tq:(u + 1) * tq, :].reshape(n_rep * tq, dh)
        t = i * tq + lax.broadcasted_iota(jnp.int32, (1, tq), 1)
        k0 = pl.multiple_of(jnp.maximum(i * tq - WINDOW, 0), tq)
        kpos = k0 + lax.broadcasted_iota(jnp.int32, (span, 1), 0)
        bias = jnp.where((kpos <= t) & (t - kpos < WINDOW), 0.0, NEG_BIG)
        s = lax.dot_general(k_ref[0, 0, pl.ds(k0, span), :], q, NT_DIMS, preferred_element_type=F32)
        s = s + jnp.concatenate([bias] * n_rep, axis=1)
        p = jnp.exp2(s - jnp.max(s, axis=0, keepdims=True))
        l = jnp.sum(p, axis=0, keepdims=True)
        pv = lax.dot_general(v_ref[0, 0, pl.ds(k0, span), :], p.astype(BF16), TN_DIMS, preferred_element_type=F32)
        _store_heads_as_rows(pv / jnp.maximum(l, F32_TINY), n_rep, o_ref, row0=u * tq)


def _attention(q, kv, k_off, v_off, n_rep, selt=None, tq=128, tk=1024, win_tiles=4):
    B, H, S, Dh = q.shape
    G = H // n_rep
    rows = tq * win_tiles if selt is None else tq
    q_spec = pl.BlockSpec((1, n_rep, rows, Dh), lambda b, g, i: (b, g, i, 0))
    kv_specs = [pl.BlockSpec((1, 1, S, Dh), lambda b, g, i: (b, k_off + g, 0, 0)),
                pl.BlockSpec((1, 1, S, Dh), lambda b, g, i: (b, v_off + g, 0, 0))]
    common = dict(
        grid=(B, G, S // rows),
        out_specs=pl.BlockSpec((1, rows, n_rep * Dh), lambda b, g, i: (b, i, g)),
        out_shape=jax.ShapeDtypeStruct((B, S, H * Dh), F32),
        compiler_params=_cparams(("parallel", "parallel", "arbitrary")),
    )
    if selt is None:
        assert WINDOW + tq <= S
        return pl.pallas_call(functools.partial(_win_attn_kernel, tq=tq), in_specs=[q_spec] + kv_specs,
                              name="nsa_attn_win", **common)(q, kv, kv)
    n_slc = selt.shape[2]
    tk = min(tk, S)
    assert S % tk == 0 and tk % (2 * tq) == 0
    e = (jnp.arange(S, dtype=jnp.int32)[:, None] // SLC_BLOCK == jnp.arange(n_slc, dtype=jnp.int32)[None, :]).astype(BF16)
    return pl.pallas_call(
        functools.partial(_slc_attn_kernel, tk=tk),
        in_specs=[q_spec, pl.BlockSpec((1, 1, n_slc, tq), lambda b, g, i: (b, g, 0, i)),
                  pl.BlockSpec((S, n_slc), lambda b, g, i: (0, 0))] + kv_specs,
        scratch_shapes=[pltpu.VMEM((1, n_rep * tq), F32), pltpu.VMEM((1, n_rep * tq), F32),
                        pltpu.VMEM((Dh, n_rep * tq), F32)],
        name="nsa_attn_slc", **common)(q, selt, e, kv, kv)


def _gate_oproj_norm_kernel(oc_ref, os_ref, ow_ref, gate_ref, ex_ref, w_ref, x_ref, gt_ref, g_ref, b_ref, out_ref):
    g_parts = _split_bf16(gate_ref[0], 2)
    o = None
    for br, o_ref in enumerate((oc_ref, os_ref, ow_ref)):
        ge = _dot(g_parts[0], ex_ref[br]) + _dot(g_parts[1], ex_ref[br])
        o = ge * o_ref[0] if o is None else o + ge * o_ref[0]
    y = _dot(o.astype(BF16), w_ref[...])
    v = ALPHA * x_ref[0] + (1.0 + gt_ref[0]) * y
    out_ref[0] = _layer_norm_rows(v, g_ref[...], b_ref[...])


def _gate_oproj_norm(o_cmp, o_slc, o_win, gates, o_w, x, gate_t, ln_g, ln_b, tm=256):
    B, S, D = x.shape
    QD = o_w.shape[0]
    lane = jnp.arange(LANES)[:, None]
    head = jnp.arange(QD)[None, :] // NSA_HEAD_DIM
    expand = jnp.stack([(lane == head * NSA_N_BRANCH + br) for br in range(NSA_N_BRANCH)]).astype(BF16)
    return pl.pallas_call(
        _gate_oproj_norm_kernel,
        grid=(B, S // tm),
        in_specs=[_row_spec(tm, QD), _row_spec(tm, QD), _row_spec(tm, QD), _row_spec(tm, LANES),
                  _const_spec((NSA_N_BRANCH, LANES, QD)), _const_spec((QD, D)), _row_spec(tm, D),
                  _batch_vec_spec(D), _const_spec((1, D)), _const_spec((1, D))],
        out_specs=_row_spec(tm, D),
        out_shape=jax.ShapeDtypeStruct((B, S, D), F32),
        compiler_params=_cparams(("parallel", "parallel")),
        name="nsa_gate_oproj_norm",
    )(o_cmp, o_slc, o_win, gates, expand, o_w.astype(BF16), x, gate_t, ln_g.reshape(1, D), ln_b.reshape(1, D))


def _nsa_shared_kv(x, sc, sh, kv_w, cmp_pos, phi_k_w1, phi_k_w2, phi_v_w1, phi_v_w2, cos, sin):
    kc, ka = _kvproj(x, sc, sh, kv_w, cos, sin)
    kvc = _compress(kc, cmp_pos, jnp.stack([phi_k_w1, phi_v_w1]), jnp.stack([phi_k_w2, phi_v_w2]))
    return kvc, ka


def _nsa_layer(x, sh, sc, gate_t, q_w, o_w, kvc, ka, cos, sin, ln_g, ln_b):
    G = NSA_KV_HEADS
    q_dim = o_w.shape[0]
    n_rep = q_dim // NSA_HEAD_DIM // G
    q, gates = _qproj(x, sc, sh, q_w, q_dim, cos, sin)
    o_cmp, selt = _cmp_select(q, kvc, n_rep)
    o_slc = _attention(q, ka, 0, G, n_rep, selt=selt)
    o_win = _attention(q, ka, 2 * G, 3 * G, n_rep)
    return _gate_oproj_norm(o_cmp, o_slc, o_win, gates, o_w, x, gate_t, ln_g, ln_b)


def kernel(x, c, pos, ada_w, ada_b, ln_g, ln_b, ssm_in_w, ssm_conv_w, ssm_conv_b, ssm_dt_bias, ssm_a_log, ssm_d, ssm_norm_w, ssm_out_w, kv_ada_w, kv_ada_b, kv_w, cmp_pos, phi_k_w1, phi_k_w2, phi_v_w1, phi_v_w2, nsa_q_w, nsa_o_w, router_w, router_b, moe_w_up, moe_b_up, moe_w_down, moe_b_down):
    c_act = jax.nn.silu(c)
    cos, sin = _rope_tables(pos)
    shared = None
    for i in range(DEPTH):
        mod = c_act @ ada_w[i] + ada_b[i]
        sh_t, sc_t, g_t, sh_c, sc_c, g_c = [m[:, None, :] for m in jnp.split(mod, 6, axis=-1)]
        if i < N_A:
            x = _mamba_layer(x, sh_t, sc_t, g_t, ssm_in_w[i], ssm_conv_w[i], ssm_conv_b[i], ssm_dt_bias[i],
                             ssm_a_log[i], ssm_d[i], ssm_norm_w[i], ssm_out_w[i], ln_g[i, 0], ln_b[i, 0])
        else:
            j = i - N_A
            x = _nsa_layer(x, sh_t, sc_t, g_t, nsa_q_w[j], nsa_o_w[j], *shared, cos, sin, ln_g[i, 0], ln_b[i, 0])
        x = _moe_layer(x, sh_c, sc_c, g_c, i, router_w[i], router_b[i],
                       moe_w_up, moe_b_up, moe_w_down, moe_b_down, ln_g[i, 1], ln_b[i, 1])
        if i == N_A - 1:
            kv_sh, kv_sc = [m[:, None, :] for m in jnp.split(c_act @ kv_ada_w + kv_ada_b, 2, axis=-1)]
            shared = _nsa_shared_kv(x, kv_sc, kv_sh, kv_w, cmp_pos, phi_k_w1, phi_k_w2, phi_v_w1, phi_v_w2, cos, sin)
    return x
```

```python
import functools

import jax
import jax.numpy as jnp
import numpy as np
from jax import lax
from jax.experimental import pallas as pl
from jax.experimental.pallas import tpu as pltpu

F32 = jnp.float32
BF16 = jnp.bfloat16

DEPTH = 4
N_A = DEPTH // 2
ALPHA = (2.0 * DEPTH) ** 0.25
LN_EPS = 1e-5

SSM_HEAD_DIM = 64
SSM_GROUPS = 4
SSM_STATE = 128
CONV_WIDTH = 4
SSM_CHUNK = 128

NSA_HEAD_DIM = 64
NSA_KV_HEADS = 4
NSA_N_BRANCH = 3
CMP_BLOCK = 32
CMP_STRIDE = 16
SLC_BLOCK = 64
N_SELECT = 16
WINDOW = 512
ROPE_THETA = 10000.0
ATTN_SCALE = NSA_HEAD_DIM ** -0.5
FORCED_SCORE = 1e9

TOP_K = 4
SWIGLU_LIMIT = 7.0
SWIGLU_ALPHA = 1.702

VMEM_LIMIT_BYTES_V7X = 56 * 1024 * 1024
LANES = 128
SUBLANES = 8
NEG_BIG = -1e30
F32_TINY = float(np.finfo(np.float32).tiny)
LOG2_E = float(np.log2(np.e))
NT_DIMS = (((1,), (1,)), ((), ()))
TN_DIMS = (((0,), (0,)), ((), ()))


def _cparams(sem):
    return pltpu.CompilerParams(dimension_semantics=sem, vmem_limit_bytes=VMEM_LIMIT_BYTES_V7X)


def _dot(a, b):
    return jnp.dot(a, b, preferred_element_type=F32)


def _split_bf16(a, parts):
    out = []
    r = a
    for _ in range(parts):
        p = r.astype(BF16)
        out.append(p)
        r = r - p.astype(F32)
    return out


def _layer_norm_rows(v, g, b):
    mu = jnp.mean(v, axis=-1, keepdims=True)
    d = v - mu
    var = jnp.mean(d * d, axis=-1, keepdims=True)
    return d * lax.rsqrt(var + LN_EPS) * g + b


def _modulated_bf16(x_ref, sc_ref, sh_ref):
    return (x_ref[0] * (1.0 + sc_ref[0]) + sh_ref[0]).astype(BF16)


def _row_spec(tm, width):
    return pl.BlockSpec((1, tm, width), lambda b, s: (b, s, 0))


def _batch_vec_spec(width):
    return pl.BlockSpec((1, 1, width), lambda b, s: (b, 0, 0))


def _const_spec(shape):
    nd = len(shape)
    return pl.BlockSpec(shape, lambda b, s: (0,) * nd)


def _router_kernel(x_ref, sc_ref, sh_ref, rw_ref, rb_ref, h_ref, idx_ref, gate_ref, cnt_ref, running):
    first_step = (pl.program_id(0) == 0) & (pl.program_id(1) == 0)

    @pl.when(first_step)
    def _():
        running[...] = jnp.zeros(running.shape, F32)

    hb = _modulated_bf16(x_ref, sc_ref, sh_ref)
    h_ref[0] = hb
    logits = _dot(hb, rw_ref[...]) + rb_ref[...]
    tm, n_exp = logits.shape
    lane_e = lax.broadcasted_iota(jnp.int32, logits.shape, 1)
    lane_o = lax.broadcasted_iota(jnp.int32, idx_ref.shape[1:], 1)
    vals, idxs = [], []
    l = logits
    for _ in range(TOP_K):
        m = jnp.max(l, axis=-1, keepdims=True)
        idx = jnp.min(jnp.where(l == m, lane_e, n_exp), axis=-1, keepdims=True)
        vals.append(m)
        idxs.append(idx)
        l = jnp.where(lane_e == idx, -jnp.inf, l)
    es = [jnp.exp(v - vals[0]) for v in vals]
    denom = es[0] + es[1] + es[2] + es[3]
    idx_out = jnp.zeros(idx_ref.shape[1:], jnp.int32)
    gate_out = jnp.zeros(gate_ref.shape[1:], F32)
    chosen = sum(jnp.where(lane_e == ix, 1.0, 0.0) for ix in idxs)
    row = lax.broadcasted_iota(jnp.int32, (tm, tm), 0)
    col = lax.broadcasted_iota(jnp.int32, (tm, tm), 1)
    earlier = jnp.where(col < row, 1.0, 0.0).astype(BF16)
    before = _dot(earlier, chosen.astype(BF16)) + running[...]
    for k in range(TOP_K):
        rank_k = jnp.sum(jnp.where(lane_e == idxs[k], before, 0.0), axis=-1, keepdims=True).astype(jnp.int32)
        idx_out = jnp.where(lane_o == k, idxs[k], idx_out)
        idx_out = jnp.where(lane_o == TOP_K + k, rank_k, idx_out)
        gate_out = jnp.where(lane_o == k, es[k] / denom, gate_out)
    idx_ref[0] = idx_out
    gate_ref[0] = gate_out
    running[...] = running[...] + jnp.sum(chosen, axis=0, keepdims=True)
    cnt_ref[...] = jnp.broadcast_to(running[...], cnt_ref.shape)


def _router(x, sc, sh, rw, rb, tm=512):
    B, S, D = x.shape
    E = rw.shape[-1]
    return pl.pallas_call(
        _router_kernel,
        grid=(B, S // tm),
        in_specs=[_row_spec(tm, D), _batch_vec_spec(D), _batch_vec_spec(D), _const_spec((D, E)), _const_spec((1, E))],
        out_specs=[_row_spec(tm, D), _row_spec(tm, LANES), _row_spec(tm, LANES), _const_spec((SUBLANES, E))],
        out_shape=[
            jax.ShapeDtypeStruct((B, S, D), BF16),
            jax.ShapeDtypeStruct((B, S, LANES), jnp.int32),
            jax.ShapeDtypeStruct((B, S, LANES), F32),
            jax.ShapeDtypeStruct((SUBLANES, E), F32),
        ],
        scratch_shapes=[pltpu.VMEM((1, E), F32)],
        compiler_params=_cparams(("arbitrary", "arbitrary")),
        name="moe_router",
    )(x, sc, sh, rw.astype(BF16), rb.reshape(1, E))


def _ffn_kernel(blk_e_ref, first_ref, nused_ref, x_ref, wu_ref, bu_ref, wd_ref, bd_ref, o_ref, wu_s, wd_s):
    i = pl.program_id(0)
    d_exp = wd_s.shape[0]

    @pl.when(i < nused_ref[0])
    def _():
        @pl.when(first_ref[i] == 1)
        def _():
            wu_s[...] = wu_ref[0].astype(BF16)
            wd_s[...] = wd_ref[0].astype(BF16)

        u = _dot(x_ref[...], wu_s[...]) + bu_ref[0]
        glu = jnp.minimum(u[:, :d_exp], SWIGLU_LIMIT)
        lin = jnp.clip(u[:, d_exp:], -SWIGLU_LIMIT, SWIGLU_LIMIT)
        act = glu * jax.nn.sigmoid(SWIGLU_ALPHA * glu) * (lin + 1.0)
        o_ref[...] = (_dot(act.astype(BF16), wd_s[...]) + bd_ref[0]).astype(o_ref.dtype)

    @pl.when(i >= nused_ref[0])
    def _():
        o_ref[...] = jnp.zeros_like(o_ref)


def _expert_ffn(xg, blk_e, first, nused, layer, w_up, b_up, w_down, b_down, tm):
    R, D = xg.shape
    n_layers, E, _, N2 = w_up.shape
    d_exp = w_down.shape[2]
    n_blk = R // tm
    grid_spec = pltpu.PrefetchScalarGridSpec(
        num_scalar_prefetch=3,
        grid=(n_blk,),
        in_specs=[
            pl.BlockSpec((tm, D), lambda i, be, fi, nu: (i, 0)),
            pl.BlockSpec((None, 1, D, N2), lambda i, be, fi, nu: (layer, be[i], 0, 0)),
            pl.BlockSpec((None, 1, 1, N2), lambda i, be, fi, nu: (layer, be[i], 0, 0)),
            pl.BlockSpec((None, 1, d_exp, D), lambda i, be, fi, nu: (layer, be[i], 0, 0)),
            pl.BlockSpec((None, 1, 1, D), lambda i, be, fi, nu: (layer, be[i], 0, 0)),
        ],
        out_specs=pl.BlockSpec((tm, D), lambda i, be, fi, nu: (i, 0)),
        scratch_shapes=[pltpu.VMEM((D, N2), BF16), pltpu.VMEM((d_exp, D), BF16)],
    )
    return pl.pallas_call(
        _ffn_kernel,
        grid_spec=grid_spec,
        out_shape=jax.ShapeDtypeStruct((R, D), BF16),
        compiler_params=_cparams(("arbitrary",)),
        name="moe_expert_ffn",
    )(blk_e, first, nused, xg, w_up, b_up.reshape(n_layers, E, 1, N2), w_down, b_down.reshape(n_layers, E, 1, D))


def _combine_norm_kernel(x_ref, yg_ref, gate_ref, gc_ref, g_ref, b_ref, o_ref):
    gate = gate_ref[0]
    y = yg_ref[0, 0] * gate[:, 0:1]
    for k in range(1, TOP_K):
        y = y + yg_ref[k, 0] * gate[:, k:k + 1]
    v = ALPHA * x_ref[0] + (1.0 + gc_ref[0]) * y
    o_ref[0] = _layer_norm_rows(v, g_ref[...], b_ref[...])


def _combine_norm(x, yg, gate, gate_c, ln_g, ln_b, tm=256):
    B, S, D = x.shape
    return pl.pallas_call(
        _combine_norm_kernel,
        grid=(B, S // tm),
        in_specs=[
            _row_spec(tm, D),
            pl.BlockSpec((TOP_K, 1, tm, D), lambda b, s: (0, b, s, 0)),
            _row_spec(tm, LANES),
            _batch_vec_spec(D),
            _const_spec((1, D)),
            _const_spec((1, D)),
        ],
        out_specs=_row_spec(tm, D),
        out_shape=jax.ShapeDtypeStruct((B, S, D), F32),
        compiler_params=_cparams(("parallel", "parallel")),
        name="moe_combine_norm",
    )(x, yg, gate, gate_c, ln_g.reshape(1, D), ln_b.reshape(1, D))


def _moe_layer(x, sh, sc, gate_c, layer, router_w, router_b, w_up, b_up, w_down, b_down, ln_g, ln_b, row_block=512):
    B, S, D = x.shape
    T = B * S
    E = router_w.shape[-1]
    h, route, gate, cnt = _router(x, sc, sh, router_w, router_b)
    A = T * TOP_K
    eid = route[:, :, :TOP_K].reshape(A)
    rank = route[:, :, TOP_K:2 * TOP_K].reshape(A)
    counts = cnt[0].astype(jnp.int32)
    padded = (counts + row_block - 1) // row_block * row_block
    off = jnp.cumsum(counts) - counts
    cum_padded = jnp.cumsum(padded)
    poff = cum_padded - padded
    pos = poff[eid] + rank
    R_rows = A + E * row_block
    n_blk = R_rows // row_block
    nused = (cum_padded[-1] // row_block).astype(jnp.int32)
    blk_start = jnp.arange(n_blk, dtype=jnp.int32) * row_block
    blk_e = jnp.minimum(jnp.sum(cum_padded[None, :] <= blk_start[:, None], axis=1), E - 1).astype(jnp.int32)
    last_e = blk_e[jnp.maximum(nused - 1, 0)]
    blk_e = jnp.where(jnp.arange(n_blk) < nused, blk_e, last_e)
    first = jnp.concatenate([jnp.ones((1,), jnp.int32), (blk_e[1:] != blk_e[:-1]).astype(jnp.int32)])
    order = jnp.argsort(eid).astype(jnp.int32)
    row = jnp.arange(R_rows, dtype=jnp.int32)
    per_row = lambda table: jnp.repeat(table[blk_e], row_block)
    j = row - per_row(poff)
    row_tok = jnp.where(j < per_row(counts), order[jnp.minimum(per_row(off) + j, A - 1)] // TOP_K, row % T)
    xg = h.reshape(T, D)[row_tok]
    y = _expert_ffn(xg, blk_e, first, nused.reshape(1), layer, w_up, b_up, w_down, b_down, row_block)
    yg = y[pos.reshape(T, TOP_K).T].reshape(TOP_K, B, S, D)
    return _combine_norm(x, yg, gate, gate_c, ln_g, ln_b)


def _outproj_norm_kernel(y_ref, w_ref, x_ref, gt_ref, g_ref, b_ref, o_ref):
    y = _dot(y_ref[0], w_ref[...])
    v = ALPHA * x_ref[0] + (1.0 + gt_ref[0]) * y
    o_ref[0] = _layer_norm_rows(v, g_ref[...], b_ref[...])


def _outproj_norm(y, w, x, gate_t, ln_g, ln_b, tm=512):
    B, S, D = x.shape
    K = y.shape[-1]
    return pl.pallas_call(
        _outproj_norm_kernel,
        grid=(B, S // tm),
        in_specs=[_row_spec(tm, K), _const_spec((K, D)), _row_spec(tm, D), _batch_vec_spec(D),
                  _const_spec((1, D)), _const_spec((1, D))],
        out_specs=_row_spec(tm, D),
        out_shape=jax.ShapeDtypeStruct((B, S, D), F32),
        compiler_params=_cparams(("parallel", "parallel")),
        name="outproj_norm",
    )(y, w.astype(BF16), x, gate_t, ln_g.reshape(1, D), ln_b.reshape(1, D))


def _ssm_inproj_kernel(x_ref, sc_ref, sh_ref, wz_ref, wx_ref, wdt_ref, cw_ref, cb_ref, dtb_ref,
                       z_ref, xs_ref, bm_ref, cm_ref, dt_ref, xbuf, *, col_chunk):
    tm = x_ref.shape[1]
    d_inner = xs_ref.shape[-1]
    bc = bm_ref.shape[-1]
    conv_dim = wx_ref.shape[-1]
    halo = SUBLANES

    @pl.when(pl.program_id(1) == 0)
    def _():
        xbuf[0:halo, :] = jnp.zeros((halo, conv_dim), F32)

    hb = _modulated_bf16(x_ref, sc_ref, sh_ref)
    z_ref[0] = _dot(hb, wz_ref[...])
    dt_ref[0] = jax.nn.softplus(_dot(hb, wdt_ref[...]) + dtb_ref[...])
    for c0 in range(0, conv_dim, col_chunk):
        cs = slice(c0, c0 + col_chunk)
        xbuf[halo:halo + tm, cs] = _dot(hb, wx_ref[:, cs])
        acc = cb_ref[:, cs] + cw_ref[CONV_WIDTH - 1:CONV_WIDTH, cs] * xbuf[halo:halo + tm, cs]
        for w in range(CONV_WIDTH - 1):
            lo = halo - (CONV_WIDTH - 1) + w
            acc = acc + cw_ref[w:w + 1, cs] * xbuf[lo:lo + tm, cs]
        y = acc * jax.nn.sigmoid(acc)
        if c0 + col_chunk <= d_inner:
            xs_ref[0, :, cs] = y
        elif c0 + col_chunk <= d_inner + bc:
            bm_ref[0, :, c0 - d_inner:c0 - d_inner + col_chunk] = y.astype(BF16)
        else:
            cm_ref[0, :, c0 - d_inner - bc:c0 - d_inner - bc + col_chunk] = y.astype(BF16)
        xbuf[0:halo, cs] = xbuf[tm:tm + halo, cs]


def _ssm_inproj(x, sc, sh, in_w, conv_w, conv_b, dt_bias, d_inner, bc_dim, tm=256, col_chunk=512):
    B, S, D = x.shape
    conv_dim = d_inner + 2 * bc_dim
    heads = dt_bias.shape[0]
    wz = in_w[:, :d_inner].astype(BF16)
    wx = in_w[:, d_inner:d_inner + conv_dim].astype(BF16)
    wdt = jnp.pad(in_w[:, d_inner + conv_dim:], ((0, 0), (0, LANES - heads))).astype(BF16)
    dtb = jnp.pad(dt_bias, (0, LANES - heads)).reshape(1, LANES)
    assert d_inner % col_chunk == 0 and bc_dim % col_chunk == 0
    return pl.pallas_call(
        functools.partial(_ssm_inproj_kernel, col_chunk=col_chunk),
        grid=(B, S // tm),
        in_specs=[_row_spec(tm, D), _batch_vec_spec(D), _batch_vec_spec(D),
                  _const_spec((D, d_inner)), _const_spec((D, conv_dim)), _const_spec((D, LANES)),
                  _const_spec((CONV_WIDTH, conv_dim)), _const_spec((1, conv_dim)), _const_spec((1, LANES))],
        out_specs=[_row_spec(tm, d_inner), _row_spec(tm, d_inner), _row_spec(tm, bc_dim), _row_spec(tm, bc_dim),
                   _row_spec(tm, LANES)],
        out_shape=[
            jax.ShapeDtypeStruct((B, S, d_inner), F32),
            jax.ShapeDtypeStruct((B, S, d_inner), F32),
            jax.ShapeDtypeStruct((B, S, bc_dim), BF16),
            jax.ShapeDtypeStruct((B, S, bc_dim), BF16),
            jax.ShapeDtypeStruct((B, S, LANES), F32),
        ],
        scratch_shapes=[pltpu.VMEM((SUBLANES + tm + SUBLANES, conv_dim), F32)],
        compiler_params=_cparams(("parallel", "arbitrary")),
        name="ssm_inproj_conv",
    )(x, sc, sh, wz, wx, wdt, conv_w, conv_b.reshape(1, conv_dim), dtb)


def _ssd_kernel(xs_ref, bm_ref, cm_ref, z_ref, dt_ref, dtt_ref, aneg_ref, anegt_ref, dskip_ref, nw_ref, ex_ref,
                y_ref, state, *, heads_per_group):
    L = xs_ref.shape[1]
    P = SSM_HEAD_DIM
    N = SSM_STATE
    n_groups = bm_ref.shape[-1] // N
    gw = heads_per_group * P

    @pl.when(pl.program_id(1) == 0)
    def _():
        state[...] = jnp.zeros(state.shape, F32)

    row = lax.broadcasted_iota(jnp.int32, (L, L), 0)
    col = lax.broadcasted_iota(jnp.int32, (L, L), 1)
    lower = row >= col
    tri = jnp.where(lower, 1.0, 0.0).astype(BF16)
    tri_t = jnp.where(row <= col, 1.0, 0.0).astype(BF16)
    dt = dt_ref[0]
    a_cs = sum(_dot(tri, p) for p in _split_bf16(dt * aneg_ref[...], 3))
    a_cs_t = sum(_dot(p, tri_t) for p in _split_bf16(dtt_ref[0] * anegt_ref[...], 3))
    dt_parts = _split_bf16(dt, 2)
    acs_parts = _split_bf16(a_cs, 3)
    lane = lax.broadcasted_iota(jnp.int32, (1, 2 * P), 1)
    first_head = lane < P
    for g in range(n_groups):
        gs = slice(g * gw, (g + 1) * gw)
        ex = ex_ref[g]
        dt_x = sum(_dot(p, ex) for p in dt_parts)
        acs_x = sum(_dot(p, ex) for p in acs_parts)
        end_x = acs_x[L - 1:L, :]
        bg = bm_ref[0, :, g * N:(g + 1) * N]
        cg = cm_ref[0, :, g * N:(g + 1) * N]
        xs_g = xs_ref[0, :, gs]
        xdt = xs_g * dt_x
        xdt_b = xdt.astype(BF16)
        s_g = state[g]
        y = _dot(cg, s_g.astype(BF16)) * jnp.exp(acs_x) + xs_g * dskip_ref[:, gs]
        to_end = (xdt * jnp.exp(end_x - acs_x)).astype(BF16)
        state[g] = s_g * jnp.exp(end_x) + lax.dot_general(bg, to_end, TN_DIMS, preferred_element_type=F32)
        cb = lax.dot_general(cg, bg, NT_DIMS, preferred_element_type=F32)
        diag = []
        for u in range(heads_per_group // 2):
            ms = []
            for h in (g * heads_per_group + 2 * u, g * heads_per_group + 2 * u + 1):
                seg = a_cs[:, h:h + 1] - a_cs_t[h:h + 1, :]
                ms.append((cb * jnp.exp(jnp.where(lower, seg, NEG_BIG))).astype(BF16))
            xp = xdt_b[:, 2 * u * P:(2 * u + 2) * P]
            zero = jnp.zeros_like(xp)
            rhs = jnp.concatenate([jnp.where(first_head, xp, zero), jnp.where(first_head, zero, xp)], axis=0)
            diag.append(_dot(jnp.concatenate(ms, axis=1), rhs))
        y = y + jnp.concatenate(diag, axis=1)
        zz = z_ref[0, :, gs]
        y = y * (zz * jax.nn.sigmoid(zz))
        scale = lax.rsqrt(jnp.sum(y * y, axis=-1, keepdims=True) / gw + LN_EPS)
        y_ref[0, :, gs] = (y * scale * nw_ref[:, gs]).astype(BF16)


def _ssd(xs, bm, cm, z, dt, a_log, d_skip, norm_w):
    B, S, d_inner = xs.shape
    heads = a_log.shape[0]
    bc_dim = bm.shape[-1]
    L = SSM_CHUNK
    a_neg = -jnp.exp(a_log)
    dt_t = jnp.swapaxes(dt[:, :, :heads], 1, 2)
    aneg = jnp.pad(a_neg, (0, LANES - heads)).reshape(1, LANES)
    anegt = jnp.broadcast_to(a_neg[:, None], (heads, L))
    dskip = jnp.repeat(d_skip, SSM_HEAD_DIM).reshape(1, d_inner)
    hpg = heads // SSM_GROUPS
    gw = hpg * SSM_HEAD_DIM
    head_of_lane = jnp.arange(SSM_GROUPS)[:, None, None] * hpg + jnp.arange(gw)[None, None, :] // SSM_HEAD_DIM
    expand = (jnp.arange(LANES)[None, :, None] == head_of_lane).astype(BF16)
    return pl.pallas_call(
        functools.partial(_ssd_kernel, heads_per_group=hpg),
        grid=(B, S // L),
        in_specs=[_row_spec(L, d_inner), _row_spec(L, bc_dim), _row_spec(L, bc_dim), _row_spec(L, d_inner),
                  _row_spec(L, LANES), pl.BlockSpec((1, heads, L), lambda b, c: (b, 0, c)),
                  _const_spec((1, LANES)), _const_spec((heads, L)), _const_spec((1, d_inner)), _const_spec((1, d_inner)),
                  _const_spec((SSM_GROUPS, LANES, gw))],
        out_specs=_row_spec(L, d_inner),
        out_shape=jax.ShapeDtypeStruct((B, S, d_inner), BF16),
        scratch_shapes=[pltpu.VMEM((SSM_GROUPS, SSM_STATE, gw), F32)],
        compiler_params=_cparams(("parallel", "arbitrary")),
        name="ssd_chunk_scan",
    )(xs, bm, cm, z, dt, dt_t, aneg, anegt, dskip, norm_w.reshape(1, d_inner), expand)


def _mamba_layer(x, sh, sc, gate_t, in_w, conv_w, conv_b, dt_bias, a_log, d_skip, norm_w, out_w, ln_g, ln_b):
    d_inner = out_w.shape[0]
    bc_dim = SSM_GROUPS * SSM_STATE
    z, xs, bm, cm, dt = _ssm_inproj(x, sc, sh, in_w, conv_w, conv_b, dt_bias, d_inner, bc_dim)
    y = _ssd(xs, bm, cm, z, dt, a_log, d_skip, norm_w)
    return _outproj_norm(y, out_w, x, gate_t, ln_g, ln_b)


def _rope_tables(pos):
    half = NSA_HEAD_DIM // 2
    inv = ROPE_THETA ** (-jnp.arange(half, dtype=F32) / half)
    ang = pos.astype(F32)[..., None] * inv
    cos, sin = jnp.cos(ang), jnp.sin(ang)
    heads_per_slab = LANES // NSA_HEAD_DIM
    return (jnp.concatenate([cos, cos] * heads_per_slab, axis=-1),
            jnp.concatenate([-sin, sin] * heads_per_slab, axis=-1))


def _rope_slab(x, cos, sin):
    half = NSA_HEAD_DIM // 2
    lane = lax.broadcasted_iota(jnp.int32, (1, LANES), 1)
    first_half = (lane % NSA_HEAD_DIM) < half
    partner = jnp.where(first_half, pltpu.roll(x, LANES - half, 1), pltpu.roll(x, half, 1))
    return x * cos + partner * sin


def _kvproj_kernel(x_ref, sc_ref, sh_ref, w_ref, cos_ref, sin_ref, kc_ref, ka_ref):
    hb = _modulated_bf16(x_ref, sc_ref, sh_ref)
    cos, sin = cos_ref[0], sin_ref[0]
    dh = NSA_HEAD_DIM
    heads_per_slab = LANES // dh
    n_slab = w_ref.shape[-1] // LANES
    slabs_per_branch = NSA_KV_HEADS // heads_per_slab
    for j in range(n_slab):
        slab = _dot(hb, w_ref[:, j * LANES:(j + 1) * LANES])
        branch = j // slabs_per_branch
        if branch % 2 == 0:
            slab = _rope_slab(slab, cos, sin)
        for u in range(heads_per_slab):
            head = (j % slabs_per_branch) * heads_per_slab + u
            piece = slab[:, u * dh:(u + 1) * dh]
            if branch < 2:
                kc_ref[0, branch * NSA_KV_HEADS + head] = piece
            else:
                ka_ref[0, (branch - 2) * NSA_KV_HEADS + head] = piece.astype(BF16)


def _kvproj(x, sc, sh, kv_w, cos, sin, tm=512):
    B, S, D = x.shape
    N = kv_w.shape[-1]
    n_cmp_arr = 2 * NSA_KV_HEADS
    n_att_arr = 4 * NSA_KV_HEADS
    return pl.pallas_call(
        _kvproj_kernel,
        grid=(B, S // tm),
        in_specs=[_row_spec(tm, D), _batch_vec_spec(D), _batch_vec_spec(D), _const_spec((D, N)),
                  _row_spec(tm, LANES), _row_spec(tm, LANES)],
        out_specs=[pl.BlockSpec((1, n_cmp_arr, tm, NSA_HEAD_DIM), lambda b, s: (b, 0, s, 0)),
                   pl.BlockSpec((1, n_att_arr, tm, NSA_HEAD_DIM), lambda b, s: (b, 0, s, 0))],
        out_shape=[jax.ShapeDtypeStruct((B, n_cmp_arr, S, NSA_HEAD_DIM), F32),
                   jax.ShapeDtypeStruct((B, n_att_arr, S, NSA_HEAD_DIM), BF16)],
        compiler_params=_cparams(("parallel", "parallel")),
        name="nsa_kv_proj",
    )(x, sc, sh, kv_w.astype(BF16), cos, sin)


def _compress_kernel(k_ref, pos_ref, w1_ref, w2_ref, o_ref):
    ks = k_ref[0, 0]
    n_sub, half_in = ks.shape
    top = _dot((ks + pos_ref[0:1, :]).astype(BF16), w1_ref[0, :half_in, :])
    bot = _dot((ks + pos_ref[1:2, :]).astype(BF16), w1_ref[0, half_in:, :])
    hid = jax.nn.gelu(top + pltpu.roll(bot, n_sub - 1, 0))
    o_ref[0, 0] = _dot(hid.astype(BF16), w2_ref[0]).astype(o_ref.dtype)


def _compress(kc, cmp_pos, w1s, w2s):
    B, A, S, Dh = kc.shape
    n_sub = S // CMP_STRIDE
    half_in = CMP_STRIDE * Dh
    hid = w1s.shape[-1]
    ksub = kc.reshape(B, A, n_sub, half_in)
    pos2 = cmp_pos.reshape(2, half_in)
    return pl.pallas_call(
        _compress_kernel,
        grid=(B, A),
        in_specs=[pl.BlockSpec((1, 1, n_sub, half_in), lambda b, a: (b, a, 0, 0)),
                  pl.BlockSpec((2, half_in), lambda b, a: (0, 0)),
                  pl.BlockSpec((1, 2 * half_in, hid), lambda b, a: (a // NSA_KV_HEADS, 0, 0)),
                  pl.BlockSpec((1, hid, Dh), lambda b, a: (a // NSA_KV_HEADS, 0, 0))],
        out_specs=pl.BlockSpec((1, 1, n_sub, Dh), lambda b, a: (b, a, 0, 0)),
        out_shape=jax.ShapeDtypeStruct((B, A, n_sub, Dh), BF16),
        compiler_params=_cparams(("parallel", "parallel")),
        name="nsa_compress",
    )(ksub, pos2, w1s.astype(BF16), w2s.astype(BF16))


def _qproj_kernel(x_ref, sc_ref, sh_ref, wq_ref, wg_ref, cos_ref, sin_ref, q_ref, g_ref):
    hb = _modulated_bf16(x_ref, sc_ref, sh_ref)
    cos, sin = cos_ref[0], sin_ref[0]
    dh = NSA_HEAD_DIM
    heads_per_slab = LANES // dh
    for j in range(wq_ref.shape[-1] // LANES):
        slab = _rope_slab(_dot(hb, wq_ref[:, j * LANES:(j + 1) * LANES]), cos, sin) * (ATTN_SCALE * LOG2_E)
        for u in range(heads_per_slab):
            q_ref[0, j * heads_per_slab + u] = slab[:, u * dh:(u + 1) * dh].astype(BF16)
    g_ref[0] = jax.nn.sigmoid(_dot(hb, wg_ref[...]))


def _qproj(x, sc, sh, q_w, q_dim, cos, sin, tm=512):
    B, S, D = x.shape
    heads = q_dim // NSA_HEAD_DIM
    n_gate = q_w.shape[-1] - q_dim
    wq = q_w[:, :q_dim].astype(BF16)
    wg = jnp.pad(q_w[:, q_dim:], ((0, 0), (0, LANES - n_gate))).astype(BF16)
    return pl.pallas_call(
        _qproj_kernel,
        grid=(B, S // tm),
        in_specs=[_row_spec(tm, D), _batch_vec_spec(D), _batch_vec_spec(D), _const_spec((D, q_dim)),
                  _const_spec((D, LANES)), _row_spec(tm, LANES), _row_spec(tm, LANES)],
        out_specs=[pl.BlockSpec((1, heads, tm, NSA_HEAD_DIM), lambda b, s: (b, 0, s, 0)), _row_spec(tm, LANES)],
        out_shape=[jax.ShapeDtypeStruct((B, heads, S, NSA_HEAD_DIM), BF16),
                   jax.ShapeDtypeStruct((B, S, LANES), F32)],
        compiler_params=_cparams(("parallel", "parallel")),
        name="nsa_q_proj",
    )(x, sc, sh, wq, wg, cos, sin)


def _store_heads_as_rows(o_t, n_rep, o_ref, row0=0):
    dh = o_t.shape[0]
    tq = o_t.shape[1] // n_rep
    for r in range(0, n_rep, 2):
        pair = jnp.concatenate([o_t[:, r * tq:(r + 1) * tq], o_t[:, (r + 1) * tq:(r + 2) * tq]], axis=0)
        o_ref[0, row0:row0 + tq, r * dh:(r + 2) * dh] = pair.T


def _cmp_select_kernel(q_ref, k_ref, v_ref, ovt_ref, o_ref, selt_ref):
    i = pl.program_id(2)
    n_rep, tq, dh = q_ref.shape[1:]
    n_cmp = k_ref.shape[2]
    n_slc = ovt_ref.shape[0]
    q = q_ref[0].reshape(n_rep * tq, dh)
    t = i * tq + lax.broadcasted_iota(jnp.int32, (1, tq), 1)
    blk_end = lax.broadcasted_iota(jnp.int32, (n_cmp, 1), 0) * CMP_STRIDE + (CMP_BLOCK - 1)
    mask = jnp.concatenate([blk_end <= t] * n_rep, axis=1)
    s = lax.dot_general(k_ref[0, 0], q, NT_DIMS, preferred_element_type=F32)
    s = jnp.where(mask, s, NEG_BIG)
    m = jnp.max(s, axis=0, keepdims=True)
    p = jnp.where(mask, jnp.exp2(s - m), 0.0)
    p = p / jnp.maximum(jnp.sum(p, axis=0, keepdims=True), F32_TINY)
    pb = p.astype(BF16)
    _store_heads_as_rows(lax.dot_general(v_ref[0, 0], pb, TN_DIMS, preferred_element_type=F32), n_rep, o_ref)
    imp = _dot(ovt_ref[...], pb[:, 0:tq])
    for r in range(1, n_rep):
        imp = imp + _dot(ovt_ref[...], pb[:, r * tq:(r + 1) * tq])
    t_blk = t // SLC_BLOCK
    j = lax.broadcasted_iota(jnp.int32, (n_slc, 1), 0)
    forced = (j == 0) | (j == t_blk) | (j == t_blk - 1)
    valid = j <= t_blk
    score = jnp.where(valid, jnp.where(forced, FORCED_SCORE, imp), -jnp.inf)
    n_tiles = n_slc // SUBLANES
    tiles = [score[r * SUBLANES:(r + 1) * SUBLANES, :] for r in range(n_tiles)]
    j_tiles = [j[r * SUBLANES:(r + 1) * SUBLANES, :] for r in range(n_tiles)]
    ranks = [jnp.zeros((SUBLANES, tq), jnp.int32) for _ in range(n_tiles)]
    for jp in range(n_slc):
        c = jnp.broadcast_to(score[jp:jp + 1, :], (SUBLANES, tq))
        for r in range(n_tiles):
            if jp < r * SUBLANES:
                beats = (c >= tiles[r]).astype(jnp.int32)
            elif jp >= (r + 1) * SUBLANES:
                beats = (c > tiles[r]).astype(jnp.int32)
            else:
                beats = jnp.where(j_tiles[r] > jp, (c >= tiles[r]).astype(jnp.int32), (c > tiles[r]).astype(jnp.int32))
            ranks[r] = ranks[r] + beats
    rank = jnp.concatenate(ranks, axis=0)
    n_sel = min(N_SELECT, n_slc)
    selt_ref[0, 0] = jnp.where((rank < n_sel) & valid, 1.0, 0.0).astype(selt_ref.dtype)


def _cmp_select(q, kvc, n_rep, tq=512):
    B, H, S, Dh = q.shape
    G = H // n_rep
    n_cmp = kvc.shape[2]
    n_slc = S // SLC_BLOCK
    cs = jnp.arange(n_cmp)[None, :] * CMP_STRIDE
    js = jnp.arange(n_slc)[:, None] * SLC_BLOCK
    ovt = jnp.maximum(jnp.minimum(cs + CMP_BLOCK, js + SLC_BLOCK) - jnp.maximum(cs, js), 0).astype(F32) / CMP_BLOCK
    return pl.pallas_call(
        _cmp_select_kernel,
        grid=(B, G, S // tq),
        in_specs=[pl.BlockSpec((1, n_rep, tq, Dh), lambda b, g, i: (b, g, i, 0)),
                  pl.BlockSpec((1, 1, n_cmp, Dh), lambda b, g, i: (b, g, 0, 0)),
                  pl.BlockSpec((1, 1, n_cmp, Dh), lambda b, g, i: (b, G + g, 0, 0)),
                  pl.BlockSpec((n_slc, n_cmp), lambda b, g, i: (0, 0))],
        out_specs=[pl.BlockSpec((1, tq, n_rep * Dh), lambda b, g, i: (b, i, g)),
                   pl.BlockSpec((1, 1, n_slc, tq), lambda b, g, i: (b, g, 0, i))],
        out_shape=[jax.ShapeDtypeStruct((B, S, H * Dh), F32), jax.ShapeDtypeStruct((B, G, n_slc, S), BF16)],
        compiler_params=_cparams(("parallel", "parallel", "parallel")),
        name="nsa_cmp_select",
    )(q, kvc, kvc, ovt.astype(BF16))


def _slc_attn_kernel(q_ref, selt_ref, e_ref, k_ref, v_ref, o_ref, m_s, l_s, acc_s, *, tk):
    i = pl.program_id(2)
    n_rep, tq, dh = q_ref.shape[1:]
    q = q_ref[0].reshape(n_rep * tq, dh)
    m_s[...] = jnp.full(m_s.shape, NEG_BIG, F32)
    l_s[...] = jnp.zeros(l_s.shape, F32)
    acc_s[...] = jnp.zeros(acc_s.shape, F32)
    t = i * tq + lax.broadcasted_iota(jnp.int32, (1, tq), 1)
    selt = selt_ref[0, 0]

    def chunk(k0, width, diagonal):
        hit = _dot(e_ref[pl.ds(k0, width), :], selt)
        allowed = hit > 0.5
        if diagonal:
            allowed = allowed & (k0 + lax.broadcasted_iota(jnp.int32, (width, 1), 0) <= t)
        bias = jnp.where(allowed, 0.0, NEG_BIG)
        s = lax.dot_general(k_ref[0, 0, pl.ds(k0, width), :], q, NT_DIMS, preferred_element_type=F32)
        s = s + jnp.concatenate([bias] * n_rep, axis=1)
        m_old = m_s[...]
        m_new = jnp.maximum(m_old, jnp.max(s, axis=0, keepdims=True))
        p = jnp.exp2(s - m_new)
        alpha = jnp.exp2(m_old - m_new)
        l_s[...] = alpha * l_s[...] + jnp.sum(p, axis=0, keepdims=True)
        pv = lax.dot_general(v_ref[0, 0, pl.ds(k0, width), :], p.astype(BF16), TN_DIMS, preferred_element_type=F32)
        acc_s[...] = alpha * acc_s[...] + pv
        m_s[...] = m_new

    def body(c, carry):
        chunk(pl.multiple_of(c * tk, tk), tk, diagonal=False)
        return carry

    n_before = (i * tq) // tk
    rem = (i + 1) * tq - n_before * tk
    lax.fori_loop(0, n_before, body, 0)
    tail0 = pl.multiple_of(n_before * tk, tk)

    @pl.when(rem > tk // 2)
    def _():
        chunk(tail0, tk, diagonal=True)

    @pl.when(rem <= tk // 2)
    def _():
        chunk(tail0, tk // 2, diagonal=True)

    _store_heads_as_rows(acc_s[...] / jnp.maximum(l_s[...], F32_TINY), n_rep, o_ref)


def _win_attn_kernel(q_ref, k_ref, v_ref, o_ref, *, tq):
    n_rep, rows, dh = q_ref.shape[1:]
    span = WINDOW + tq
    for u in range(rows // tq):
        i = pl.program_id(2) * (rows // tq) + u
        q = q_ref[0, :, u * tq:(u + 1) * tq, :].reshape(n_rep * tq, dh)
        t = i * tq + lax.broadcasted_iota(jnp.int32, (1, tq), 1)
        k0 = pl.multiple_of(jnp.maximum(i * tq - WINDOW, 0), tq)
        kpos = k0 + lax.broadcasted_iota(jnp.int32, (span, 1), 0)
        bias = jnp.where((kpos <= t) & (t - kpos < WINDOW), 0.0, NEG_BIG)
        s = lax.dot_general(k_ref[0, 0, pl.ds(k0, span), :], q, NT_DIMS, preferred_element_type=F32)
        s = s + jnp.concatenate([bias] * n_rep, axis=1)
        p = jnp.exp2(s - jnp.max(s, axis=0, keepdims=True))
        l = jnp.sum(p, axis=0, keepdims=True)
        pv = lax.dot_general(v_ref[0, 0, pl.ds(k0, span), :], p.astype(BF16), TN_DIMS, preferred_element_type=F32)
        _store_heads_as_rows(pv / jnp.maximum(l, F32_TINY), n_rep, o_ref, row0=u * tq)


def _attention(q, kv, k_off, v_off, n_rep, selt=None, tq=128, tk=1024, win_tiles=4):
    B, H, S, Dh = q.shape
    G = H // n_rep
    rows = tq * win_tiles if selt is None else tq
    q_spec = pl.BlockSpec((1, n_rep, rows, Dh), lambda b, g, i: (b, g, i, 0))
    kv_specs = [pl.BlockSpec((1, 1, S, Dh), lambda b, g, i: (b, k_off + g, 0, 0)),
                pl.BlockSpec((1, 1, S, Dh), lambda b, g, i: (b, v_off + g, 0, 0))]
    common = dict(
        grid=(B, G, S // rows),
        out_specs=pl.BlockSpec((1, rows, n_rep * Dh), lambda b, g, i: (b, i, g)),
        out_shape=jax.ShapeDtypeStruct((B, S, H * Dh), F32),
        compiler_params=_cparams(("parallel", "parallel", "arbitrary")),
    )
    if selt is None:
        assert WINDOW + tq <= S
        return pl.pallas_call(functools.partial(_win_attn_kernel, tq=tq), in_specs=[q_spec] + kv_specs,
                              name="nsa_attn_win", **common)(q, kv, kv)
    n_slc = selt.shape[2]
    tk = min(tk, S)
    assert S % tk == 0 and tk % (2 * tq) == 0
    e = (jnp.arange(S, dtype=jnp.int32)[:, None] // SLC_BLOCK == jnp.arange(n_slc, dtype=jnp.int32)[None, :]).astype(BF16)
    return pl.pallas_call(
        functools.partial(_slc_attn_kernel, tk=tk),
        in_specs=[q_spec, pl.BlockSpec((1, 1, n_slc, tq), lambda b, g, i: (b, g, 0, i)),
                  pl.BlockSpec((S, n_slc), lambda b, g, i: (0, 0))] + kv_specs,
        scratch_shapes=[pltpu.VMEM((1, n_rep * tq), F32), pltpu.VMEM((1, n_rep * tq), F32),
                        pltpu.VMEM((Dh, n_rep * tq), F32)],
        name="nsa_attn_slc", **common)(q, selt, e, kv, kv)


def _gate_oproj_norm_kernel(oc_ref, os_ref, ow_ref, gate_ref, ex_ref, w_ref, x_ref, gt_ref, g_ref, b_ref, out_ref):
    g_parts = _split_bf16(gate_ref[0], 2)
    o = None
    for br, o_ref in enumerate((oc_ref, os_ref, ow_ref)):
        ge = _dot(g_parts[0], ex_ref[br]) + _dot(g_parts[1], ex_ref[br])
        o = ge * o_ref[0] if o is None else o + ge * o_ref[0]
    y = _dot(o.astype(BF16), w_ref[...])
    v = ALPHA * x_ref[0] + (1.0 + gt_ref[0]) * y
    out_ref[0] = _layer_norm_rows(v, g_ref[...], b_ref[...])


def _gate_oproj_norm(o_cmp, o_slc, o_win, gates, o_w, x, gate_t, ln_g, ln_b, tm=256):
    B, S, D = x.shape
    QD = o_w.shape[0]
    lane = jnp.arange(LANES)[:, None]
    head = jnp.arange(QD)[None, :] // NSA_HEAD_DIM
    expand = jnp.stack([(lane == head * NSA_N_BRANCH + br) for br in range(NSA_N_BRANCH)]).astype(BF16)
    return pl.pallas_call(
        _gate_oproj_norm_kernel,
        grid=(B, S // tm),
        in_specs=[_row_spec(tm, QD), _row_spec(tm, QD), _row_spec(tm, QD), _row_spec(tm, LANES),
                  _const_spec((NSA_N_BRANCH, LANES, QD)), _const_spec((QD, D)), _row_spec(tm, D),
                  _batch_vec_spec(D), _const_spec((1, D)), _const_spec((1, D))],
        out_specs=_row_spec(tm, D),
        out_shape=jax.ShapeDtypeStruct((B, S, D), F32),
        compiler_params=_cparams(("parallel", "parallel")),
        name="nsa_gate_oproj_norm",
    )(o_cmp, o_slc, o_win, gates, expand, o_w.astype(BF16), x, gate_t, ln_g.reshape(1, D), ln_b.reshape(1, D))


def _nsa_shared_kv(x, sc, sh, kv_w, cmp_pos, phi_k_w1, phi_k_w2, phi_v_w1, phi_v_w2, cos, sin):
    kc, ka = _kvproj(x, sc, sh, kv_w, cos, sin)
    kvc = _compress(kc, cmp_pos, jnp.stack([phi_k_w1, phi_v_w1]), jnp.stack([phi_k_w2, phi_v_w2]))
    return kvc, ka


def _nsa_layer(x, sh, sc, gate_t, q_w, o_w, kvc, ka, cos, sin, ln_g, ln_b):
    G = NSA_KV_HEADS
    q_dim = o_w.shape[0]
    n_rep = q_dim // NSA_HEAD_DIM // G
    q, gates = _qproj(x, sc, sh, q_w, q_dim, cos, sin)
    o_cmp, selt = _cmp_select(q, kvc, n_rep)
    o_slc = _attention(q, ka, 0, G, n_rep, selt=selt)
    o_win = _attention(q, ka, 2 * G, 3 * G, n_rep)
    return _gate_oproj_norm(o_cmp, o_slc, o_win, gates, o_w, x, gate_t, ln_g, ln_b)


def kernel(x, c, pos, ada_w, ada_b, ln_g, ln_b, ssm_in_w, ssm_conv_w, ssm_conv_b, ssm_dt_bias, ssm_a_log, ssm_d, ssm_norm_w, ssm_out_w, kv_ada_w, kv_ada_b, kv_w, cmp_pos, phi_k_w1, phi_k_w2, phi_v_w1, phi_v_w2, nsa_q_w, nsa_o_w, router_w, router_b, moe_w_up, moe_b_up, moe_w_down, moe_b_down):
    c_act = jax.nn.silu(c)
    cos, sin = _rope_tables(pos)
    shared = None
    for i in range(DEPTH):
        mod = c_act @ ada_w[i] + ada_b[i]
        sh_t, sc_t, g_t, sh_c, sc_c, g_c = [m[:, None, :] for m in jnp.split(mod, 6, axis=-1)]
        if i < N_A:
            x = _mamba_layer(x, sh_t, sc_t, g_t, ssm_in_w[i], ssm_conv_w[i], ssm_conv_b[i], ssm_dt_bias[i],
                             ssm_a_log[i], ssm_d[i], ssm_norm_w[i], ssm_out_w[i], ln_g[i, 0], ln_b[i, 0])
        else:
            j = i - N_A
            x = _nsa_layer(x, sh_t, sc_t, g_t, nsa_q_w[j], nsa_o_w[j], *shared, cos, sin, ln_g[i, 0], ln_b[i, 0])
        x = _moe_layer(x, sh_c, sc_c, g_c, i, router_w[i], router_b[i],
                       moe_w_up, moe_b_up, moe_w_down, moe_b_down, ln_g[i, 1], ln_b[i, 1])
        if i == N_A - 1:
            kv_sh, kv_sc = [m[:, None, :] for m in jnp.split(c_act @ kv_ada_w + kv_ada_b, 2, axis=-1)]
            shared = _nsa_shared_kv(x, kv_sc, kv_sh, kv_w, cmp_pos, phi_k_w1, phi_k_w2, phi_v_w1, phi_v_w2, cos, sin)
    return x
```
